```python
import jax
import jax.numpy as jnp
from jax import lax
import numpy as np

D_MODEL = 1024
BATCH = 16
SEQ = 256
DEPTH = 4
DEC_BATCH = 2
DEC_SEQ = 4096
PAST_LEN = 512

GRID_W = 64
HEAD_DIM = 64
A_HEADS = 8
A_KV_HEADS = 2
B_HEADS = 4
NB_ROWS = 8
NB_COLS = 16
C_HEADS = 4
C_DK = 64
C_DV = 64
D_HEADS = 4
D_DK = 64
D_DV = 64
CONV_K = 5
CHUNK = 64
Q_BLOCK = 128
N_EXPERTS = 32
TOP_K = 4
D_FF = 1024
MOE_BLOCK = 128
SWIGLU_LIMIT = 7.0
SWIGLU_ALPHA = 1.702
ROPE_THETA = 10000.0
EPS = 1e-6
MAX_INPUT_GATE = 1.0 - 1e-6
N_MOD = 6
N_BRANCH = 4

A_Q_W = A_HEADS * HEAD_DIM
A_KV_W = A_KV_HEADS * HEAD_DIM
B_W = B_HEADS * HEAD_DIM
C_K_W = C_HEADS * C_DK
C_V_W = C_HEADS * C_DV
D_K_W = D_HEADS * D_DK
D_V_W = D_HEADS * D_DV
IN_WIDTHS = (A_Q_W, A_KV_W, A_KV_W, B_W, B_W, B_W, C_K_W, C_V_W, C_K_W, C_K_W, C_V_W,
             2 * D_K_W + D_V_W, 2 * D_HEADS, 2 * D_HEADS, D_V_W)
IN_W = A_Q_W + 2 * A_KV_W + 3 * B_W + 3 * C_K_W + 2 * C_V_W + 2 * D_K_W + 2 * D_V_W + 4 * D_HEADS
BRANCH_WIDTHS = (A_Q_W, B_W, C_V_W, D_V_W)
MIX_W = A_Q_W + B_W + C_V_W + D_V_W
F32 = jnp.float32

kernel_name = 'hybrid_flow_prefix_trunk_step'


def rms_norm(x, g):
    xf = x.astype(F32)
    y = xf * lax.rsqrt(jnp.mean(xf * xf, axis=-1, keepdims=True) + EPS)
    return (y * g.astype(F32)).astype(x.dtype)


def l2_norm(x):
    xf = x.astype(F32)
    return (xf * lax.rsqrt(jnp.sum(xf * xf, axis=-1, keepdims=True) + EPS)).astype(x.dtype)


def split_cols(z, widths, axis=-1):
    return jnp.split(z, np.cumsum(widths)[:-1].tolist(), axis=axis)


def flip_t(a):
    return jnp.flip(a, axis=1)


def masked_exp(mask, x):
    return jnp.where(mask, jnp.exp(jnp.where(mask, x, 0.0)), 0.0)


def axial_rope_tables(n_tokens):
    t = jnp.arange(n_tokens, dtype=jnp.int32)
    row = (t // GRID_W).astype(F32)
    col = (t % GRID_W).astype(F32)
    half = HEAD_DIM // 2
    inv_freq = 1.0 / (ROPE_THETA ** (jnp.arange(0, half, 2, dtype=F32) / half))
    ang_r = row[:, None] * inv_freq[None, :]
    ang_c = col[:, None] * inv_freq[None, :]
    ang = jnp.concatenate([ang_r, ang_r, ang_c, ang_c], axis=-1)
    return jnp.cos(ang), jnp.sin(ang)


def rotate_half(h):
    h1, h2 = jnp.split(h, 2, axis=-1)
    return jnp.concatenate([-h2, h1], axis=-1)


def apply_axial_rope(x, cos, sin):
    xr, xc = jnp.split(x, 2, axis=-1)
    x_rot = jnp.concatenate([rotate_half(xr), rotate_half(xc)], axis=-1)
    return x * cos[None, :, None, :].astype(x.dtype) + x_rot * sin[None, :, None, :].astype(x.dtype)


def blocked_attention(q, k, v):
    b, t, h, dh = q.shape
    hkv = k.shape[2]
    g = h // hkv
    nblk = t // Q_BLOCK
    qb = q.reshape(b, nblk, Q_BLOCK, hkv, g, dh).transpose(1, 0, 2, 3, 4, 5)

    def one_block(qi):
        s = jnp.einsum('bqhgd,bshd->bhgqs', qi, k).astype(F32)
        p = jax.nn.softmax(s, axis=-1).astype(v.dtype)
        return jnp.einsum('bhgqs,bshd->bqhgd', p, v)

    o = lax.map(one_block, qb)
    return o.transpose(1, 0, 2, 3, 4, 5).reshape(b, t, h, dh)


def neighbourhood_attention(q, k, v, ck, cv, rpb):
    b, t, h, dh = q.shape
    rows = t // GRID_W
    wr = min(NB_ROWS, rows)
    n_loc = wr * NB_COLS
    qg = q.reshape(b, rows, GRID_W, h, dh)
    kg = k.reshape(b, rows, GRID_W, h, dh)
    vg = v.reshape(b, rows, GRID_W, h, dh)
    cols = np.arange(GRID_W)
    col_start = np.clip(cols - NB_COLS // 2, 0, GRID_W - NB_COLS)
    col_idx = col_start[:, None] + np.arange(NB_COLS)[None, :]
    dc_idx = col_idx - cols[:, None] + NB_COLS - 1
    rpb32 = rpb.astype(F32)

    def one_row(r):
        rs = jnp.clip(r - wr // 2, 0, rows - wr)
        krow = lax.dynamic_slice_in_dim(kg, rs, wr, axis=1)
        vrow = lax.dynamic_slice_in_dim(vg, rs, wr, axis=1)
        kwin = krow[:, :, col_idx]
        vwin = vrow[:, :, col_idx]
        qr = lax.dynamic_index_in_dim(qg, r, axis=1, keepdims=False)
        s_loc = jnp.einsum('bchd,bicjhd->bhcij', qr, kwin).astype(F32)
        dr_idx = rs + jnp.arange(wr) - r + NB_ROWS - 1
        bias = rpb32[:, dr_idx][:, :, dc_idx].transpose(0, 2, 1, 3)
        s_loc = (s_loc + bias[None]).reshape(b, h, GRID_W, n_loc)
        s_ctx = jnp.einsum('bchd,bshd->bhcs', qr, ck).astype(F32)
        p = jax.nn.softmax(jnp.concatenate([s_loc, s_ctx], axis=-1), axis=-1).astype(v.dtype)
        p_loc = p[..., :n_loc].reshape(b, h, GRID_W, wr, NB_COLS)
        p_ctx = p[..., n_loc:]
        return (jnp.einsum('bhcij,bicjhd->bchd', p_loc, vwin)
                + jnp.einsum('bhcs,bshd->bchd', p_ctx, cv))

    o = lax.map(one_row, jnp.arange(rows))
    return o.transpose(1, 0, 2, 3, 4).reshape(b, t, h, dh)


def to_chunks(a):
    b, t, h, x = a.shape
    return a.reshape(b, t // CHUNK, CHUNK, h, x).transpose(1, 0, 3, 2, 4)


def from_chunks(o):
    n, b, h, c, x = o.shape
    return o.transpose(1, 0, 3, 2, 4).reshape(b, n * c, h, x)


def gla_chunk_scan(q, k, v, log_f, s0):
    incl = jnp.tril(jnp.ones((CHUNK, CHUNK), bool))[:, :, None]

    def step(s, inp):
        qi, ki, vi, gi = [a.astype(F32) for a in inp]
        bcum = jnp.cumsum(gi, axis=-2)
        o_inter = jnp.einsum('bhtk,bhkv->bhtv', qi * jnp.exp(bcum), s)
        diff = bcum[:, :, :, None, :] - bcum[:, :, None, :, :]
        decay = masked_exp(incl, diff)
        attn = jnp.einsum('bhtk,bhsk,bhtsk->bhts', qi, ki, decay)
        o = o_inter + jnp.einsum('bhts,bhsv->bhtv', attn, vi)
        blast = bcum[:, :, -1:, :]
        s_new = (jnp.exp(blast[:, :, 0])[..., None] * s
                 + jnp.einsum('bhsk,bhsv->bhkv', ki * jnp.exp(blast - bcum), vi))
        return s_new, o.astype(q.dtype)

    s_fin, o = lax.scan(step, s0.astype(F32),
                        (to_chunks(q), to_chunks(k), to_chunks(v), to_chunks(log_f)))
    return from_chunks(o), s_fin


def delta_chunk_scan(q, k, v, g, beta, s0):
    incl = jnp.tril(jnp.ones((CHUNK, CHUNK), bool))
    strict = jnp.tril(jnp.ones((CHUNK, CHUNK), F32), -1)
    eye = jnp.eye(CHUNK, dtype=F32)

    def step(s, inp):
        qi, ki, vi, gi, bi = [a.astype(F32) for a in inp]
        gc = jnp.cumsum(gi[..., 0], axis=-1)
        gamma = masked_exp(incl, gc[..., :, None] - gc[..., None, :])
        kb = ki * bi
        a = jnp.einsum('bhtk,bhsk->bhts', kb, ki) * gamma * strict
        eye_b = jnp.broadcast_to(eye, a.shape)
        tmat = lax.linalg.triangular_solve(eye_b + a, eye_b, left_side=True, lower=True, unit_diagonal=True)
        u = tmat @ (vi * bi)
        w = tmat @ (kb * jnp.exp(gc)[..., None])
        v_new = u - w @ s
        o = (qi * jnp.exp(gc)[..., None]) @ s + (jnp.einsum('bhtk,bhsk->bhts', qi, ki) * gamma) @ v_new
        g_last = gc[..., -1:]
        s_new = (jnp.exp(g_last)[..., None] * s
                 + jnp.einsum('bhsk,bhsv->bhkv', ki * jnp.exp(g_last - gc)[..., None], v_new))
        return s_new, o.astype(q.dtype)

    s_fin, o = lax.scan(step, s0.astype(F32),
                        (to_chunks(q), to_chunks(k), to_chunks(v), to_chunks(g), to_chunks(beta)))
    return from_chunks(o), s_fin


def centred_depthwise_conv(x, w):
    ch = x.shape[-1]
    return lax.conv_general_dilated(x, w[:, None, :].astype(x.dtype), window_strides=(1,),
                                    padding=[(CONV_K // 2, CONV_K // 2)],
                                    dimension_numbers=('NWC', 'WIO', 'NWC'), feature_group_count=ch)


def hgrn2_mixer(cq, ci, cf_f, cf_b, cg, lb, norm_w, s0_f, s0_b):
    b, t, _ = cq.shape

    def heads(a):
        return a.reshape(b, t, C_HEADS, -1)

    q = heads(jax.nn.silu(cq))
    v = heads(ci)

    def gates(f_raw, lbd):
        f_raw = f_raw.astype(F32)
        lbd = lbd.astype(F32)
        k = jnp.minimum((1.0 - lbd) * jax.nn.sigmoid(-f_raw), MAX_INPUT_GATE)
        log_f = jnp.log1p(-k)
        return heads(log_f), heads(k).astype(cq.dtype)

    lf_f, k_f = gates(cf_f, lb[0])
    lf_b, k_b = gates(cf_b, lb[1])
    o_f, s_f = gla_chunk_scan(q, k_f, v, lf_f, s0_f)
    o_b, s_b = gla_chunk_scan(flip_t(q), flip_t(k_b), flip_t(v), flip_t(lf_b), s0_b)
    o = rms_norm(o_f + flip_t(o_b), norm_w) * jax.nn.silu(heads(cg))
    return o.reshape(b, t, C_V_W), jnp.stack([s_f, s_b], axis=1)


def gated_delta_mixer(dqkv, da, db, dg, conv_w, a_log, dt_bias, norm_w, s0_f, s0_b):
    b, t, _ = dqkv.shape
    qkv = jax.nn.silu(centred_depthwise_conv(dqkv, conv_w))
    q, k, v = split_cols(qkv, (D_K_W, D_K_W, D_V_W))
    q = l2_norm(q.reshape(b, t, D_HEADS, D_DK)) * (D_DK ** -0.5)
    k = l2_norm(k.reshape(b, t, D_HEADS, D_DK))
    v = v.reshape(b, t, D_HEADS, D_DV)
    da = da.astype(F32).reshape(b, t, 2, D_HEADS)
    g = -jnp.exp(a_log.astype(F32)) * jax.nn.softplus(da + dt_bias.astype(F32))
    beta = jax.nn.sigmoid(db.astype(F32)).reshape(b, t, 2, D_HEADS)
    o_f, s_f = delta_chunk_scan(q, k, v, g[:, :, 0, :, None], beta[:, :, 0, :, None], s0_f)
    o_b, s_b = delta_chunk_scan(flip_t(q), flip_t(k), flip_t(v), flip_t(g[:, :, 1, :, None]),
                                flip_t(beta[:, :, 1, :, None]), s0_b)
    o = rms_norm(o_f + flip_t(o_b), norm_w) * jax.nn.silu(dg.reshape(b, t, D_HEADS, D_DV))
    return o.reshape(b, t, D_V_W), jnp.stack([s_f, s_b], axis=1)


def moe_ffn(h, w_router, b_router, w_gu, b_gu, w_down, b_down):
    n, d = h.shape
    logits = jnp.matmul(h, w_router).astype(F32) + b_router.astype(F32)
    top_v, top_i = lax.top_k(logits, TOP_K)
    gates = jax.nn.softmax(top_v, axis=-1)
    n_assign = n * TOP_K
    flat_e = top_i.reshape(n_assign)
    flat_tok = jnp.arange(n_assign, dtype=jnp.int32) // TOP_K
    order = jnp.argsort(flat_e)
    sorted_e = flat_e[order]
    counts = jnp.bincount(flat_e, length=N_EXPERTS)
    padded = (counts + MOE_BLOCK - 1) // MOE_BLOCK * MOE_BLOCK
    pad_end = jnp.cumsum(padded)
    pad_start = pad_end - padded
    start = jnp.cumsum(counts) - counts
    dest = pad_start[sorted_e] + jnp.arange(n_assign, dtype=jnp.int32) - start[sorted_e]
    n_blocks = -(-n_assign // MOE_BLOCK) + N_EXPERTS
    n_slots = n_blocks * MOE_BLOCK
    slot_tok = jnp.zeros((n_slots,), jnp.int32).at[dest].set(flat_tok[order])
    slot_w = jnp.zeros((n_slots,), h.dtype).at[dest].set(gates.reshape(n_assign)[order].astype(h.dtype))
    block_e = jnp.minimum(jnp.searchsorted(pad_end, jnp.arange(n_blocks, dtype=jnp.int32) * MOE_BLOCK,
                                           side='right'), N_EXPERTS - 1)
    x_blocks = h[slot_tok].reshape(n_blocks, MOE_BLOCK, d)

    def expert_block(args):
        xb, e = args
        gu = jnp.matmul(xb, w_gu[e]) + b_gu[e]
        gate, up = gu[:, :D_FF], gu[:, D_FF:]
        gate = jnp.minimum(gate, SWIGLU_LIMIT)
        up = jnp.clip(up, -SWIGLU_LIMIT, SWIGLU_LIMIT)
        act = gate * jax.nn.sigmoid(SWIGLU_ALPHA * gate) * (up + 1.0)
        return jnp.matmul(act, w_down[e]) + b_down[e]

    y_slots = lax.map(expert_block, (x_blocks, block_e)).reshape(n_slots, d)
    return jnp.zeros_like(h).at[slot_tok].add(y_slots * slot_w[:, None])


def token_mixers(h, lp, lb, rope=None, ctx=None):
    b, t, _ = h.shape
    (aq, ak, av, bq, bk, bv, cq, ci, cf_f, cf_b, cg, dqkv, da, db, dg) = split_cols(
        jnp.matmul(h, lp['w_in']), IN_WIDTHS)
    scale = HEAD_DIM ** -0.5
    qa = rms_norm(aq.reshape(b, t, A_HEADS, HEAD_DIM), lp['a_qn'])
    ka = rms_norm(ak.reshape(b, t, A_KV_HEADS, HEAD_DIM), lp['a_kn'])
    va = av.reshape(b, t, A_KV_HEADS, HEAD_DIM)
    qb = rms_norm(bq.reshape(b, t, B_HEADS, HEAD_DIM), lp['b_qn'])
    kb = rms_norm(bk.reshape(b, t, B_HEADS, HEAD_DIM), lp['b_kn'])
    vb = bv.reshape(b, t, B_HEADS, HEAD_DIM)
    if ctx is None:
        oa = blocked_attention(qa * scale, ka, va)
        ob = blocked_attention(qb * scale, kb, vb)
        zc = jnp.zeros((b, C_HEADS, C_DK, C_DV), F32)
        zd = jnp.zeros((b, D_HEADS, D_DK, D_DV), F32)
        sc0, sd0 = (zc, zc), (zd, zd)
    else:
        cos, sin = rope
        cak, cav, cbk, cbv, sc_l, sd_l = ctx
        qa_r = apply_axial_rope(qa, cos, sin)
        ka_r = apply_axial_rope(ka, cos, sin)
        oa = blocked_attention(qa_r * scale, jnp.concatenate([cak, ka_r], axis=1),
                               jnp.concatenate([cav, va], axis=1))
        ob = neighbourhood_attention(qb * scale, kb, vb, cbk, cbv, lp['b_rpb'])
        sc0, sd0 = (sc_l[:, 0], sc_l[:, 1]), (sd_l[:, 0], sd_l[:, 1])
    oc, sc = hgrn2_mixer(cq, ci, cf_f, cf_b, cg, lb, lp['c_norm'], sc0[0], sc0[1])
    od, sd = gated_delta_mixer(dqkv, da, db, dg, lp['d_conv'], lp['d_alog'], lp['d_dtbias'],
                               lp['d_norm'], sd0[0], sd0[1])
    outs = (oa.reshape(b, t, A_Q_W), ob.reshape(b, t, B_W), oc, od)
    gate_list = split_cols(jax.nn.sigmoid(jnp.matmul(h, lp['w_gate'])), (D_MODEL,) * N_BRANCH)
    w_br = split_cols(lp['w_branch'], BRANCH_WIDTHS, axis=0)
    merged = gate_list[0] * jnp.matmul(outs[0], w_br[0])
    for i in range(1, N_BRANCH):
        merged = merged + gate_list[i] * jnp.matmul(outs[i], w_br[i])
    return jnp.matmul(merged, lp['w_out']), (ka, va, kb, vb, sc, sd)


def trunk_layer(x, cond, lp, lb, rope=None, ctx=None):
    sh1, sc1, g1, sh2, sc2, g2 = jnp.split(
        jnp.matmul(jax.nn.silu(cond), lp['w_mod']) + lp['b_mod'], N_MOD, axis=-1)
    h = rms_norm(x, lp['norm1']) * (1.0 + sc1[:, None]) + sh1[:, None]
    y, ctx_out = token_mixers(h, lp, lb, rope, ctx)
    x = x + g1[:, None] * y
    h = rms_norm(x, lp['norm2']) * (1.0 + sc2[:, None]) + sh2[:, None]
    b, t, d = h.shape
    f = moe_ffn(h.reshape(b * t, d), lp['w_router'], lp['b_router'], lp['w_gu'], lp['b_gu'],
                lp['w_down'], lp['b_down']).reshape(b, t, d)
    return x + g2[:, None] * f, ctx_out


def hgrn_lower_bounds(raw):
    p = jax.nn.softmax(raw.astype(F32), axis=0)
    return jnp.cumsum(p, axis=0) - p[0:1]


def setup_inputs(seed: int = 0) -> dict:
    key = jax.random.key(seed)
    ks = jax.random.split(key, 40)

    def nrm(k, shape, s):
        return s * jax.random.normal(k, shape, F32)

    d = D_MODEL
    dt = jnp.exp(jax.random.uniform(ks[30], (DEPTH, 2, D_HEADS), F32, np.log(1e-3), np.log(1e-1)))
    return {
        'x_prompt': nrm(ks[0], (BATCH, SEQ, d), 1.0),
        'x_sample': nrm(ks[1], (DEC_BATCH, DEC_SEQ, d), 1.0),
        'cache_a_k': nrm(ks[2], (DEC_BATCH, DEPTH, PAST_LEN, A_KV_HEADS, HEAD_DIM), 1.0),
        'cache_a_v': nrm(ks[3], (DEC_BATCH, DEPTH, PAST_LEN, A_KV_HEADS, HEAD_DIM), 1.0),
        'cache_b_k': nrm(ks[4], (DEC_BATCH, DEPTH, PAST_LEN, B_HEADS, HEAD_DIM), 1.0),
        'cache_b_v': nrm(ks[5], (DEC_BATCH, DEPTH, PAST_LEN, B_HEADS, HEAD_DIM), 1.0),
        'state_hgrn': nrm(ks[6], (DEC_BATCH, DEPTH, 2, C_HEADS, C_DK, C_DV), 0.1),
        'state_delta': nrm(ks[7], (DEC_BATCH, DEPTH, 2, D_HEADS, D_DK, D_DV), 0.1),
        'c': nrm(ks[8], (DEC_BATCH, d), 1.0),
        'c_ctx': nrm(ks[9], (d,), 1.0),
        'w_mod': nrm(ks[10], (DEPTH, d, N_MOD * d), 0.5 * d ** -0.5),
        'b_mod': nrm(ks[11], (DEPTH, N_MOD * d), 0.01),
        'norm1': 1.0 + nrm(ks[12], (DEPTH, d), 0.1),
        'norm2': 1.0 + nrm(ks[13], (DEPTH, d), 0.1),
        'w_in': nrm(ks[14], (DEPTH, d, IN_W), d ** -0.5),
        'w_gate': nrm(ks[15], (DEPTH, d, N_BRANCH * d), d ** -0.5),
        'a_qn': 1.0 + nrm(ks[16], (DEPTH, HEAD_DIM), 0.1),
        'a_kn': 1.0 + nrm(ks[17], (DEPTH, HEAD_DIM), 0.1),
        'b_qn': 1.0 + nrm(ks[18], (DEPTH, HEAD_DIM), 0.1),
        'b_kn': 1.0 + nrm(ks[19], (DEPTH, HEAD_DIM), 0.1),
        'b_rpb': nrm(ks[20], (DEPTH, B_HEADS, 2 * NB_ROWS - 1, 2 * NB_COLS - 1), 0.1),
        'c_lb': nrm(ks[21], (DEPTH, 2, C_K_W), 0.5),
        'c_norm': 1.0 + nrm(ks[22], (DEPTH, C_DV), 0.1),
        'd_conv': nrm(ks[23], (DEPTH, CONV_K, 2 * D_K_W + D_V_W), CONV_K ** -0.5),
        'd_alog': jnp.log(jax.random.uniform(ks[24], (DEPTH, 2, D_HEADS), F32, 1.0, 16.0)),
        'd_dtbias': jnp.log(jnp.expm1(dt)),
        'd_norm': 1.0 + nrm(ks[25], (DEPTH, D_DV), 0.1),
        'w_branch': nrm(ks[26], (DEPTH, MIX_W, d), (MIX_W // N_BRANCH) ** -0.5),
        'w_out': nrm(ks[27], (DEPTH, d, d), d ** -0.5),
        'w_router': nrm(ks[28], (DEPTH, d, N_EXPERTS), d ** -0.5),
        'b_router': nrm(ks[29], (DEPTH, N_EXPERTS), 0.01),
        'w_gu': nrm(ks[31], (DEPTH, N_EXPERTS, d, 2 * D_FF), d ** -0.5),
        'b_gu': nrm(ks[32], (DEPTH, N_EXPERTS, 2 * D_FF), 0.01),
        'w_down': nrm(ks[33], (DEPTH, N_EXPERTS, D_FF, d), D_FF ** -0.5),
        'b_down': nrm(ks[34], (DEPTH, N_EXPERTS, d), 0.01),
    }


def reference(x_prompt, x_sample, cache_a_k, cache_a_v, cache_b_k, cache_b_v, state_hgrn, state_delta,
              c, c_ctx, w_mod, b_mod, norm1, norm2, w_in, w_gate, a_qn, a_kn, b_qn, b_kn, b_rpb,
              c_lb, c_norm, d_conv, d_alog, d_dtbias, d_norm, w_branch, w_out, w_router, b_router,
              w_gu, b_gu, w_down, b_down):
    lbs = hgrn_lower_bounds(c_lb)
    cos, sin = axial_rope_tables(x_sample.shape[1])
    xp, xs = x_prompt, x_sample
    ak_l, av_l, bk_l, bv_l, sc_l, sd_l = [], [], [], [], [], []
    for l in range(DEPTH):
        lp = {'w_mod': w_mod[l], 'b_mod': b_mod[l], 'norm1': norm1[l], 'norm2': norm2[l],
              'w_in': w_in[l], 'w_gate': w_gate[l], 'a_qn': a_qn[l], 'a_kn': a_kn[l],
              'b_qn': b_qn[l], 'b_kn': b_kn[l], 'b_rpb': b_rpb[l], 'c_norm': c_norm[l],
              'd_conv': d_conv[l], 'd_alog': d_alog[l], 'd_dtbias': d_dtbias[l], 'd_norm': d_norm[l],
              'w_branch': w_branch[l], 'w_out': w_out[l], 'w_router': w_router[l],
              'b_router': b_router[l], 'w_gu': w_gu[l], 'b_gu': b_gu[l], 'w_down': w_down[l],
              'b_down': b_down[l]}
        xp, (ka, va, kb, vb, sc, sd) = trunk_layer(xp, c_ctx[None, :], lp, lbs[l])
        ak_l.append(ka)
        av_l.append(va)
        bk_l.append(kb)
        bv_l.append(vb)
        sc_l.append(sc)
        sd_l.append(sd)
        layer_ctx = (cache_a_k[:, l], cache_a_v[:, l], cache_b_k[:, l], cache_b_v[:, l],
                     state_hgrn[:, l], state_delta[:, l])
        xs, _ = trunk_layer(xs, c, lp, lbs[l], rope=(cos, sin), ctx=layer_ctx)
    new_cache_a_k = jnp.stack(ak_l, axis=1)
    new_cache_a_v = jnp.stack(av_l, axis=1)
    new_cache_b_k = jnp.stack(bk_l, axis=1)
    new_cache_b_v = jnp.stack(bv_l, axis=1)
    new_state_hgrn = jnp.stack(sc_l, axis=1)
    new_state_delta = jnp.stack(sd_l, axis=1)
    return (xp, xs, new_cache_a_k, new_cache_a_v, new_cache_b_k, new_cache_b_v, new_state_hgrn, new_state_delta)
```

```python
import functools

import numpy as np
import jax
import jax.numpy as jnp
from jax import lax
from jax.experimental import pallas as pl
from jax.experimental.pallas import tpu as pltpu

F32 = jnp.float32
BF16 = jnp.bfloat16
HIGHEST = lax.Precision.HIGHEST

GRID_W = 64
HEAD_DIM = 64
A_HEADS = 8
A_KV_HEADS = 2
B_HEADS = 4
NB_ROWS = 8
NB_COLS = 16
C_HEADS = 4
D_HEADS = 4
CONV_K = 5
CHUNK = 64
N_EXPERTS = 32
TOP_K = 4
SWIGLU_LIMIT = 7.0
SWIGLU_ALPHA = 1.702
ROPE_THETA = 10000.0
EPS = 1e-6
MAX_INPUT_GATE = 1.0 - 1e-6
N_MOD = 6
HW = 256

LANES = 128
VMEM_LIMIT = 56 * 1024 * 1024
NEG = -1e30

TM = 256
TM_MM = 512
TN_MM = 1024
TQ = 128
MOE_TM = 256

COL = dict(aq=0, ak=512, av=640, bq=768, bk=1024, bv=1280, cq=1536, ci=1792, cff=2048, cfb=2304,
           cg=2560, dg=2816, dq=3072, dk=3328, dv=3584, dab=3840)
IN_W_PAD = 4096


def _cparams(sem, vmem=VMEM_LIMIT):
    return pltpu.CompilerParams(dimension_semantics=sem, vmem_limit_bytes=vmem)


def _silu(x):
    return x * jax.nn.sigmoid(x)


def _dot(a, b):
    return jnp.dot(a, b, preferred_element_type=F32)


def _dot_t(a, b):
    return lax.dot_general(a, b, (((1,), (1,)), ((), ())), preferred_element_type=F32)


def _mod_kernel(c_ref, w_ref, b_ref, o_ref):
    a = _silu(c_ref[...]).astype(BF16)
    o_ref[...] = _dot(a, w_ref[...].astype(BF16)) + b_ref[...]


def _mod_call(conds, w_mod, b_mod):
    depth, d, n = w_mod.shape
    tn = 1024
    return pl.pallas_call(
        _mod_kernel,
        grid=(depth, n // tn),
        in_specs=[pl.BlockSpec((8, d), lambda l, j: (0, 0)),
                  pl.BlockSpec((None, d, tn), lambda l, j: (l, 0, j)),
                  pl.BlockSpec((None, 1, tn), lambda l, j: (l, 0, j))],
        out_specs=pl.BlockSpec((None, 8, tn), lambda l, j: (l, 0, j)),
        out_shape=jax.ShapeDtypeStruct((depth, 8, n), F32),
        compiler_params=_cparams(("parallel", "parallel")),
        name="mod",
    )(conds, w_mod, b_mod.reshape(depth, 1, n))


def _mod_norm(x, g, m, k_shift, k_scale, d):
    y = x * lax.rsqrt(jnp.mean(x * x, axis=-1, keepdims=True) + EPS) * g
    return y * (1.0 + m[:, k_scale * d:(k_scale + 1) * d]) + m[:, k_shift * d:(k_shift + 1) * d]


def _norm_kernel(x_ref, g_ref, mod_ref, h_ref, *, d):
    h_ref[...] = _mod_norm(x_ref[...], g_ref[...], mod_ref[...], 0, 1, d).astype(BF16)


def _norm_call(x, g, mod3, cond_of_tile):
    n, d = x.shape
    return pl.pallas_call(
        functools.partial(_norm_kernel, d=d),
        grid=(n // TM,),
        in_specs=[pl.BlockSpec((TM, d), lambda i: (i, 0)),
                  pl.BlockSpec((1, d), lambda i: (0, 0)),
                  pl.BlockSpec((None, 1, N_MOD * d), lambda i: (cond_of_tile(i), 0, 0))],
        out_specs=pl.BlockSpec((TM, d), lambda i: (i, 0)),
        out_shape=jax.ShapeDtypeStruct((n, d), BF16),
        compiler_params=_cparams(("parallel",)),
        name="norm1",
    )(x, g, mod3)


def _mm_kernel(a_ref, w_ref, o_ref):
    o_ref[...] = _dot(a_ref[...], w_ref[...]).astype(o_ref.dtype)


def _mm_call(a, w, out_dtype=F32):
    m, k = a.shape
    n = w.shape[1]
    return pl.pallas_call(
        _mm_kernel,
        grid=(n // TN_MM, m // TM_MM),
        in_specs=[pl.BlockSpec((TM_MM, k), lambda j, i: (i, 0)),
                  pl.BlockSpec((k, TN_MM), lambda j, i: (0, j))],
        out_specs=pl.BlockSpec((TM_MM, TN_MM), lambda j, i: (i, j)),
        out_shape=jax.ShapeDtypeStruct((m, n), out_dtype),
        compiler_params=_cparams(("parallel", "parallel")),
        name="in_proj",
    )(a, w)


def _merge_kernel(x_ref, h_ref, oa_ref, ob_ref, oc_ref, od_ref, wg_ref, wb_ref, wo_ref, mod_ref, o_ref, *, d):
    h = h_ref[...]
    outs = (oa_ref, ob_ref, oc_ref, od_ref)
    merged = None
    off = 0
    for i in range(4):
        wdt = outs[i].shape[1]
        gate = jax.nn.sigmoid(_dot(h, wg_ref[:, i * d:(i + 1) * d]))
        br = gate * _dot(outs[i][...], wb_ref[off:off + wdt, :])
        merged = br if merged is None else merged + br
        off += wdt
    y = _dot(merged.astype(BF16), wo_ref[...])
    o_ref[...] = x_ref[...] + mod_ref[...][:, 2 * d:3 * d] * y


def _merge_call(x, h, oa, ob, oc, od, w_gate, w_branch, w_out, mod3, cond_of_tile):
    n, d = x.shape
    row = lambda i: (i, 0)
    const = lambda i: (0, 0)
    once = pl.Buffered(1)
    return pl.pallas_call(
        functools.partial(_merge_kernel, d=d),
        grid=(n // TM,),
        in_specs=[pl.BlockSpec((TM, d), row), pl.BlockSpec((TM, d), row),
                  pl.BlockSpec((TM, oa.shape[1]), row), pl.BlockSpec((TM, ob.shape[1]), row),
                  pl.BlockSpec((TM, oc.shape[1]), row), pl.BlockSpec((TM, od.shape[1]), row),
                  pl.BlockSpec(w_gate.shape, const, pipeline_mode=once),
                  pl.BlockSpec(w_branch.shape, const, pipeline_mode=once),
                  pl.BlockSpec(w_out.shape, const, pipeline_mode=once),
                  pl.BlockSpec((None, 1, N_MOD * d), lambda i: (cond_of_tile(i), 0, 0))],
        out_specs=pl.BlockSpec((TM, d), row),
        out_shape=jax.ShapeDtypeStruct((n, d), F32),
        compiler_params=_cparams(("parallel",)),
        name="merge",
    )(x, h, oa, ob, oc, od, w_gate, w_branch, w_out, mod3)


def _router_kernel(x_ref, g_ref, mod_ref, wr_ref, br_ref, h_ref, gate_ref, idx_ref, *, d):
    h2 = _mod_norm(x_ref[...], g_ref[...], mod_ref[...], 3, 4, d)
    h_ref[...] = h2.astype(BF16)
    logits = jnp.dot(h2, wr_ref[...], preferred_element_type=F32, precision=HIGHEST) + br_ref[...]
    lane = lax.broadcasted_iota(jnp.int32, logits.shape, 1)
    vals, idxs = [], []
    cur = logits
    for _ in range(TOP_K):
        m = jnp.max(cur, axis=-1, keepdims=True)
        i = jnp.min(jnp.where(cur == m, lane, LANES), axis=-1, keepdims=True)
        vals.append(m)
        idxs.append(i)
        cur = jnp.where(lane == i, -jnp.inf, cur)
    es = [jnp.exp(v - vals[0]) for v in vals]
    tot = es[0] + es[1] + es[2] + es[3]
    gates = jnp.zeros(logits.shape, F32)
    sel = jnp.zeros(logits.shape, jnp.int32)
    for k in range(TOP_K):
        gates = jnp.where(lane == k, es[k] / tot, gates)
        sel = jnp.where(lane == k, idxs[k], sel)
    gate_ref[...] = gates
    idx_ref[...] = sel


def _router_call(x, g, mod3, cond_of_tile, w_router_p, b_router_p):
    n, d = x.shape
    row = lambda i: (i, 0)
    return pl.pallas_call(
        functools.partial(_router_kernel, d=d),
        grid=(n // TM,),
        in_specs=[pl.BlockSpec((TM, d), row),
                  pl.BlockSpec((1, d), lambda i: (0, 0)),
                  pl.BlockSpec((None, 1, N_MOD * d), lambda i: (cond_of_tile(i), 0, 0)),
                  pl.BlockSpec((d, LANES), lambda i: (0, 0)),
                  pl.BlockSpec((1, LANES), lambda i: (0, 0))],
        out_specs=[pl.BlockSpec((TM, d), row), pl.BlockSpec((TM, LANES), row), pl.BlockSpec((TM, LANES), row)],
        out_shape=[jax.ShapeDtypeStruct((n, d), BF16), jax.ShapeDtypeStruct((n, LANES), F32),
                   jax.ShapeDtypeStruct((n, LANES), jnp.int32)],
        compiler_params=_cparams(("parallel",)),
        name="router",
    )(x, g, mod3, w_router_p, b_router_p)


def _gmm_kernel(be_ref, nu_ref, x_ref, wgu_ref, bgu_ref, wd_ref, bd_ref, y_ref, wgu_s, wd_s, *, f):
    i = pl.program_id(0)
    used = i < nu_ref[0]
    changed = (i == 0) | (be_ref[i] != be_ref[jnp.maximum(i - 1, 0)])

    @pl.when(used & changed)
    def _():
        wgu_s[...] = wgu_ref[...].astype(BF16)
        wd_s[...] = wd_ref[...].astype(BF16)

    @pl.when(used)
    def _():
        gu = _dot(x_ref[...], wgu_s[...]) + bgu_ref[...]
        gate = jnp.minimum(gu[:, :f], SWIGLU_LIMIT)
        up = jnp.clip(gu[:, f:], -SWIGLU_LIMIT, SWIGLU_LIMIT)
        act = gate * jax.nn.sigmoid(SWIGLU_ALPHA * gate) * (up + 1.0)
        y_ref[...] = _dot(act.astype(BF16), wd_s[...]) + bd_ref[...]

    @pl.when(jnp.logical_not(used))
    def _():
        y_ref[...] = jnp.zeros(y_ref.shape, y_ref.dtype)


def _gmm_call(block_e, n_used, x_slots, w_gu, b_gu, w_down, b_down):
    n_slots, d = x_slots.shape
    e, _, f2 = w_gu.shape
    f = f2 // 2
    n_blocks = n_slots // MOE_TM
    grid_spec = pltpu.PrefetchScalarGridSpec(
        num_scalar_prefetch=2,
        grid=(n_blocks,),
        in_specs=[pl.BlockSpec((MOE_TM, d), lambda i, be, nu: (i, 0)),
                  pl.BlockSpec((None, d, f2), lambda i, be, nu: (be[i], 0, 0)),
                  pl.BlockSpec((None, 1, f2), lambda i, be, nu: (be[i], 0, 0)),
                  pl.BlockSpec((None, f, d), lambda i, be, nu: (be[i], 0, 0)),
                  pl.BlockSpec((None, 1, d), lambda i, be, nu: (be[i], 0, 0))],
        out_specs=pl.BlockSpec((MOE_TM, d), lambda i, be, nu: (i, 0)),
        scratch_shapes=[pltpu.VMEM((d, f2), BF16), pltpu.VMEM((f, d), BF16)],
    )
    return pl.pallas_call(
        functools.partial(_gmm_kernel, f=f),
        grid_spec=grid_spec,
        out_shape=jax.ShapeDtypeStruct((n_slots, d), F32),
        compiler_params=_cparams(("arbitrary",)),
        name="experts",
    )(block_e, n_used, x_slots, w_gu, b_gu.reshape(e, 1, f2), w_down, b_down.reshape(e, 1, d))


def _combine_kernel(x_ref, y_ref, gate_ref, mod_ref, o_ref, *, d):
    g = gate_ref[...]
    acc = g[:, 0:1] * y_ref[0]
    for k in range(1, TOP_K):
        acc = acc + g[:, k:k + 1] * y_ref[k]
    o_ref[...] = x_ref[...] + mod_ref[...][:, 5 * d:6 * d] * acc


def _combine_call(x, yg, gates, mod3, cond_of_tile):
    n, d = x.shape
    row = lambda i: (i, 0)
    return pl.pallas_call(
        functools.partial(_combine_kernel, d=d),
        grid=(n // TM,),
        in_specs=[pl.BlockSpec((TM, d), row),
                  pl.BlockSpec((TOP_K, TM, d), lambda i: (0, i, 0)),
                  pl.BlockSpec((TM, LANES), row),
                  pl.BlockSpec((None, 1, N_MOD * d), lambda i: (cond_of_tile(i), 0, 0))],
        out_specs=pl.BlockSpec((TM, d), row),
        out_shape=jax.ShapeDtypeStruct((n, d), F32),
        compiler_params=_cparams(("parallel",)),
        name="combine",
    )(x, yg, gates, mod3)


def _route(top_i, n_tok):
    n_assign = n_tok * TOP_K
    flat_e = top_i.reshape(n_assign)
    onehot = (flat_e[:, None] == jnp.arange(N_EXPERTS, dtype=jnp.int32)[None, :]).astype(jnp.int32)
    csum = jnp.cumsum(onehot, axis=0)
    pos = jnp.take_along_axis(csum, flat_e[:, None], axis=1)[:, 0] - 1
    counts = csum[-1]
    padded = (counts + MOE_TM - 1) // MOE_TM * MOE_TM
    pad_end = jnp.cumsum(padded)
    pad_start = pad_end - padded
    dest = pad_start[flat_e] + pos
    n_blocks = n_assign // MOE_TM + N_EXPERTS
    slot_tok = jnp.zeros((n_blocks * MOE_TM,), jnp.int32).at[dest].set(
        jnp.arange(n_assign, dtype=jnp.int32) // TOP_K)
    block_e = jnp.minimum(jnp.searchsorted(pad_end, jnp.arange(n_blocks, dtype=jnp.int32) * MOE_TM,
                                           side='right'), N_EXPERTS - 1).astype(jnp.int32)
    n_used = (pad_end[-1:] // MOE_TM).astype(jnp.int32)
    return dest, slot_tok, block_e, n_used


def _seg_matrix(w):
    i = np.arange(w) // HEAD_DIM
    return jnp.asarray((i[:, None] == i[None, :]).astype(np.float32))


def _head_mean_sq(x, p):
    return jnp.dot(x * x, p, preferred_element_type=F32, precision=HIGHEST) * (1.0 / HEAD_DIM)


def _prep_kernel(z_ref, nw_ref, p_ref, *rest, scale, rope, want_norm):
    if rope:
        cos_ref, sin_ref = rest[0], rest[1]
        rest = rest[2:]
    x = z_ref[...]
    w = x.shape[1]
    y = x * lax.rsqrt(_head_mean_sq(x, p_ref[...]) + EPS) * nw_ref[...]
    if want_norm:
        rest[0][...] = y
    out_ref = rest[-1]
    if rope:
        reps = w // LANES
        cos = jnp.concatenate([cos_ref[...]] * reps, axis=1) if reps > 1 else cos_ref[...]
        sin = jnp.concatenate([sin_ref[...]] * reps, axis=1) if reps > 1 else sin_ref[...]
        lane = lax.broadcasted_iota(jnp.int32, y.shape, 1)
        first = (lane % 32) < 16
        rot = jnp.where(first, pltpu.roll(y, w - 16, axis=1), pltpu.roll(y, 16, axis=1))
        y = y * cos + rot * sin
    out_ref[...] = (y * scale).astype(BF16)


def _prep_call(z, col, w, nw, scale, rope_tabs, want_norm):
    n = z.shape[0]
    cb = col // w
    row = lambda i: (i, 0)
    in_specs = [pl.BlockSpec((TM, w), lambda i: (i, cb)),
                pl.BlockSpec((1, w), lambda i: (0, 0)),
                pl.BlockSpec((w, w), lambda i: (0, 0))]
    args = [z, jnp.tile(nw, w // HEAD_DIM).reshape(1, w), _seg_matrix(w)]
    if rope_tabs is not None:
        in_specs += [pl.BlockSpec((TM, LANES), row), pl.BlockSpec((TM, LANES), row)]
        args += list(rope_tabs)
    out_specs, out_shape = [], []
    if want_norm:
        out_specs.append(pl.BlockSpec((TM, w), row))
        out_shape.append(jax.ShapeDtypeStruct((n, w), F32))
    out_specs.append(pl.BlockSpec((TM, w), row))
    out_shape.append(jax.ShapeDtypeStruct((n, w), BF16))
    return pl.pallas_call(
        functools.partial(_prep_kernel, scale=scale, rope=rope_tabs is not None, want_norm=want_norm),
        grid=(n // TM,),
        in_specs=in_specs, out_specs=out_specs, out_shape=out_shape,
        compiler_params=_cparams(("parallel",)),
        name="qk_prep",
    )(*args)


def _attn_kernel(q_ref, k_ref, v_ref, *rest, hq, hk, has_ctx):
    if has_ctx:
        ck_ref, cv_ref, o_ref = rest
    else:
        (o_ref,) = rest
    g = hq // hk
    tq = q_ref.shape[0]
    dh = HEAD_DIM
    outs = []
    for j in range(hk):
        q4 = jnp.concatenate([q_ref[:, (j * g + gi) * dh:(j * g + gi + 1) * dh] for gi in range(g)], axis=0)
        s = _dot_t(q4, k_ref[:, j * dh:(j + 1) * dh])
        m = jnp.max(s, axis=-1, keepdims=True)
        if has_ctx:
            sc = _dot_t(q4, ck_ref[:, j * dh:(j + 1) * dh])
            m = jnp.maximum(m, jnp.max(sc, axis=-1, keepdims=True))
        p = jnp.exp(s - m)
        l = jnp.sum(p, axis=-1, keepdims=True)
        o = _dot(p.astype(BF16), v_ref[:, j * dh:(j + 1) * dh].astype(BF16))
        if has_ctx:
            pc = jnp.exp(sc - m)
            l = l + jnp.sum(pc, axis=-1, keepdims=True)
            o = o + _dot(pc.astype(BF16), cv_ref[:, j * dh:(j + 1) * dh])
        o = o / l
        outs += [o[gi * tq:(gi + 1) * tq] for gi in range(g)]
    o_ref[...] = jnp.concatenate(outs, axis=1).astype(o_ref.dtype)


def _attn_call(q, k, v, vcol, row0, nb, t, hq, hk, ctx=None):
    tq = min(TQ, t)
    wq, wk = hq * HEAD_DIM, hk * HEAD_DIM
    qb0, kb0, vb = row0 // tq, row0 // t, vcol // wk
    nq = t // tq
    in_specs = [pl.BlockSpec((tq, wq), lambda b, i: (qb0 + b * nq + i, 0)),
                pl.BlockSpec((t, wk), lambda b, i: (kb0 + b, 0)),
                pl.BlockSpec((t, wk), lambda b, i: (kb0 + b, vb))]
    args = [q, k, v]
    if ctx is not None:
        p = ctx[0].shape[1]
        in_specs += [pl.BlockSpec((None, p, wk), lambda b, i: (b, 0, 0))] * 2
        args += list(ctx)
    return pl.pallas_call(
        functools.partial(_attn_kernel, hq=hq, hk=hk, has_ctx=ctx is not None),
        grid=(nb, nq),
        in_specs=in_specs,
        out_specs=pl.BlockSpec((tq, wq), lambda b, i: (b * nq + i, 0)),
        out_shape=jax.ShapeDtypeStruct((nb * t, wq), BF16),
        compiler_params=_cparams(("parallel", "parallel")),
        name="attn",
    )(*args)


def _nbr_bias_table(rpb):
    o = np.arange(NB_ROWS)[:, None, None, None]
    cc = np.arange(GRID_W)[None, :, None, None]
    i = np.arange(NB_ROWS)[None, None, :, None]
    j = np.arange(GRID_W)[None, None, None, :]
    col_start = np.clip(cc - NB_COLS // 2, 0, GRID_W - NB_COLS)
    valid = (j >= col_start) & (j < col_start + NB_COLS)
    full = (NB_ROWS, GRID_W, NB_ROWS, GRID_W)
    dr = np.broadcast_to(i - o + NB_ROWS - 1, full)
    dc = np.clip(np.broadcast_to(j - cc + NB_COLS - 1, full), 0, 2 * NB_COLS - 2)
    tab = rpb.astype(F32)[:, dr, dc]
    tab = jnp.where(jnp.asarray(np.broadcast_to(valid, full))[None], tab, NEG)
    return tab.reshape(rpb.shape[0], NB_ROWS, GRID_W, NB_ROWS * GRID_W)


def _nbr_kernel(q_ref, k_ref, v_ref, ck_ref, cv_ref, bias_ref, o_ref, *, rows):
    r = pl.program_id(1)
    rs = jnp.clip(r - NB_ROWS // 2, 0, rows - NB_ROWS)
    start = pl.multiple_of(rs * GRID_W, GRID_W)
    kw = k_ref[pl.ds(start, NB_ROWS * GRID_W), :]
    vw = v_ref[pl.ds(start, NB_ROWS * GRID_W), :].astype(BF16)
    dh = HEAD_DIM
    outs = []
    for h in range(B_HEADS):
        sl = slice(h * dh, (h + 1) * dh)
        qh = q_ref[:, sl]
        s_loc = _dot_t(qh, kw[:, sl]) + bias_ref[h]
        s_ctx = _dot_t(qh, ck_ref[:, sl])
        m = jnp.maximum(jnp.max(s_loc, axis=-1, keepdims=True), jnp.max(s_ctx, axis=-1, keepdims=True))
        p_loc = jnp.exp(s_loc - m)
        p_ctx = jnp.exp(s_ctx - m)
        l = jnp.sum(p_loc, axis=-1, keepdims=True) + jnp.sum(p_ctx, axis=-1, keepdims=True)
        o = _dot(p_loc.astype(BF16), vw[:, sl]) + _dot(p_ctx.astype(BF16), cv_ref[:, sl])
        outs.append(o / l)
    o_ref[...] = jnp.concatenate(outs, axis=1).astype(o_ref.dtype)


def _nbr_call(q, k, z, vcol, row0, nb, t, ck, cv, bias):
    rows = t // GRID_W
    assert rows >= NB_ROWS
    w = B_HEADS * HEAD_DIM
    p = ck.shape[1]
    nwin = NB_ROWS * GRID_W

    def bias_map(b, r):
        return (0, r - jnp.clip(r - NB_ROWS // 2, 0, rows - NB_ROWS), 0, 0)

    return pl.pallas_call(
        functools.partial(_nbr_kernel, rows=rows),
        grid=(nb, rows),
        in_specs=[pl.BlockSpec((GRID_W, w), lambda b, r: (row0 // GRID_W + b * rows + r, 0)),
                  pl.BlockSpec((t, w), lambda b, r: (row0 // t + b, 0)),
                  pl.BlockSpec((t, w), lambda b, r: (row0 // t + b, vcol // w)),
                  pl.BlockSpec((None, p, w), lambda b, r: (b, 0, 0)),
                  pl.BlockSpec((None, p, w), lambda b, r: (b, 0, 0)),
                  pl.BlockSpec((B_HEADS, None, GRID_W, nwin), bias_map)],
        out_specs=pl.BlockSpec((GRID_W, w), lambda b, r: (b * rows + r, 0)),
        out_shape=jax.ShapeDtypeStruct((nb * t, w), BF16),
        compiler_params=_cparams(("parallel", "arbitrary")),
        name="nbr_attn",
    )(q, k, z, ck, cv, bias)


def _head_mask():
    i = np.arange(HW) // HEAD_DIM
    return jnp.asarray((i[:, None] == i[None, :]).astype(np.float32))


def _scan_rows(rev):
    t = lax.broadcasted_iota(jnp.int32, (CHUNK, HW), 0)
    return (CHUNK - 1 - t) if rev else t


def _scan_lanes(rev):
    s = lax.broadcasted_iota(jnp.int32, (CHUNK, HW), 1) % CHUNK
    return (CHUNK - 1 - s) if rev else s


def _sh(x, d, rev):
    s = (-d if rev else d) % CHUNK
    return x if s == 0 else pltpu.roll(x, s, axis=0)


def _cumsum_scan(x, pt, rev):
    d = 1
    while d < CHUNK:
        x = x + jnp.where(pt >= d, _sh(x, d, rev), 0.0)
        d *= 2
    return x


def _block_end(x, pt, n, rev):
    r = (n - 1) - pt % n
    bit = 1
    while bit < n:
        x = jnp.where((r & bit) != 0, _sh(x, -bit, rev), x)
        bit *= 2
    return x


def _pick16(x, j, rev):
    jj = (15 - j) if rev else j
    x3 = x.reshape(CHUNK // 16, 16, HW)
    return jnp.broadcast_to(x3[:, jj:jj + 1, :], x3.shape).reshape(CHUNK, HW)


def _row(x, p, rev):
    t = (CHUNK - 1 - p) if rev else p
    return x[t:t + 1, :]


def _bd(x, mask_bf):
    xb = x.astype(BF16)
    return jnp.concatenate([xb] * (HW // CHUNK), axis=0) * mask_bf


def _gla_chunk(q_raw, v, f_raw, lb, st_ref, mask, rev):
    mask_bf = mask.astype(BF16)
    pt = _scan_rows(rev)
    ps = _scan_lanes(rev)
    q = _silu(q_raw)
    kk = jnp.minimum((1.0 - lb) * jax.nn.sigmoid(-f_raw), MAX_INPUT_GATE)
    b = _cumsum_scan(jnp.log1p(-kk), pt, rev)
    b_last = _row(b, CHUNK - 1, rev)
    e4 = _block_end(b, pt, 4, rev)
    r4 = jnp.where(pt >= 4, _sh(e4, 4, rev), 0.0)
    r16 = jnp.where(pt >= 16, _sh(_pick16(b, 15, rev), 16, rev), 0.0)
    q0 = (q * jnp.exp(b - r4)).astype(BF16)
    q2 = (q * jnp.exp(b - r16)).astype(BF16)
    qe = (q * jnp.exp(b)).astype(BF16)
    ks01 = [_bd(kk * jnp.exp(r4 - b), mask_bf)]
    ks2 = []
    for i in range(1, 4):
        ks01.append(_bd(kk * jnp.exp(jnp.minimum(_pick16(r4, 4 * i, rev) - b, 0.0)), mask_bf))
        ks2.append(_bd(kk * jnp.exp(jnp.minimum(_row(b, 16 * i - 1, rev) - b, 0.0)), mask_bf))
    r01 = _dot_t(q0, jnp.concatenate(ks01, axis=0))
    r2 = _dot_t(q2, jnp.concatenate(ks2, axis=0))
    pt4, ps4, pt16, ps16 = pt // 4, ps // 4, pt // 16, ps // 16
    attn = jnp.where((pt4 == ps4) & (ps <= pt), r01[:, :HW], 0.0)
    for i in range(1, 4):
        attn = jnp.where((pt16 == ps16) & (ps4 < pt4) & (pt4 % 4 == i), r01[:, i * HW:(i + 1) * HW], attn)
        attn = jnp.where((ps16 < pt16) & (pt16 == i), r2[:, (i - 1) * HW:i * HW], attn)
    st = st_ref[...]
    o = _dot_t(qe, st.astype(BF16)) + _dot(attn.astype(BF16), _bd(v, mask_bf))
    ke = (kk * jnp.exp(b_last - b)).astype(BF16)
    upd = lax.dot_general(v.astype(BF16), ke, (((0,), (0,)), ((), ())), preferred_element_type=F32)
    st_ref[...] = jnp.exp(b_last) * st + mask * upd
    return o


def _hgrn_kernel(qf_ref, vf_ref, ff_ref, qb_ref, vb_ref, fb_ref, lb_ref, mask_ref, *rest, has_state):
    if has_state:
        s0_ref, of_ref, ob_ref, sout_ref, st_s = rest
    else:
        of_ref, ob_ref, sout_ref, st_s = rest
    c = pl.program_id(1)

    @pl.when(c == 0)
    def _():
        st_s[...] = s0_ref[...] if has_state else jnp.zeros(st_s.shape, F32)

    mask = mask_ref[...]
    of_ref[...] = _gla_chunk(qf_ref[...], vf_ref[...], ff_ref[...], lb_ref[0:1, :], st_s.at[0], mask, False)
    ob_ref[...] = _gla_chunk(qb_ref[...], vb_ref[...], fb_ref[...], lb_ref[1:2, :], st_s.at[1], mask, True)

    @pl.when(c == pl.num_programs(1) - 1)
    def _():
        sout_ref[...] = st_s[...]


def _state_to_bd(s):
    nb = s.shape[0]
    eye = jnp.eye(C_HEADS, dtype=F32)
    return jnp.einsum('bdhkv,hg->bdhvgk', s.astype(F32), eye).reshape(nb, 2, HW, HW)


def _state_from_bd(sbd):
    nb = sbd.shape[0]
    s6 = sbd.reshape(nb, 2, C_HEADS, HEAD_DIM, C_HEADS, HEAD_DIM)
    return jnp.stack([s6[:, :, h, :, h, :] for h in range(C_HEADS)], axis=2).transpose(0, 1, 2, 4, 3)


def _scan_specs(row0, nc, cols_f, cols_b):
    fwd = [pl.BlockSpec((CHUNK, HW), functools.partial(lambda b, c, cb: (row0 // CHUNK + b * nc + c, cb), cb=col // HW))
           for col in cols_f]
    bwd = [pl.BlockSpec((CHUNK, HW), functools.partial(lambda b, c, cb: (row0 // CHUNK + b * nc + nc - 1 - c, cb),
                                                      cb=col // HW))
           for col in cols_b]
    return fwd + bwd


def _hgrn_call(z, row0, nb, t, lb, s0_bd):
    nc = t // CHUNK
    has_state = s0_bd is not None
    const2 = lambda b, c: (0, 0)
    in_specs = _scan_specs(row0, nc, (COL['cq'], COL['ci'], COL['cff']), (COL['cq'], COL['ci'], COL['cfb']))
    in_specs += [pl.BlockSpec((2, HW), const2), pl.BlockSpec((HW, HW), const2)]
    args = [z] * 6 + [lb, _head_mask()]
    if has_state:
        in_specs.append(pl.BlockSpec((None, 2, HW, HW), lambda b, c: (b, 0, 0, 0)))
        args.append(s0_bd)
    return pl.pallas_call(
        functools.partial(_hgrn_kernel, has_state=has_state),
        grid=(nb, nc),
        in_specs=in_specs,
        out_specs=[pl.BlockSpec((CHUNK, HW), lambda b, c: (b * nc + c, 0)),
                   pl.BlockSpec((CHUNK, HW), lambda b, c: (b * nc + nc - 1 - c, 0)),
                   pl.BlockSpec((None, 2, HW, HW), lambda b, c: (b, 0, 0, 0))],
        out_shape=[jax.ShapeDtypeStruct((nb * t, HW), F32), jax.ShapeDtypeStruct((nb * t, HW), F32),
                   jax.ShapeDtypeStruct((nb, 2, HW, HW), F32)],
        scratch_shapes=[pltpu.VMEM((2, HW, HW), F32)],
        compiler_params=_cparams(("parallel", "arbitrary")),
        name="hgrn_scan",
    )(*args)


def _gate_norm_kernel(of_ref, ob_ref, g_ref, nw_ref, p_ref, o_ref):
    o = of_ref[...] + ob_ref[...]
    y = o * lax.rsqrt(_head_mean_sq(o, p_ref[...]) + EPS) * nw_ref[...]
    o_ref[...] = (y * _silu(g_ref[...])).astype(o_ref.dtype)


def _gate_norm_call(o_f, o_b, z, gcol, row0, nw):
    n = o_f.shape[0]
    row = lambda i: (i, 0)
    return pl.pallas_call(
        _gate_norm_kernel,
        grid=(n // TM,),
        in_specs=[pl.BlockSpec((TM, HW), row), pl.BlockSpec((TM, HW), row),
                  pl.BlockSpec((TM, HW), lambda i: (row0 // TM + i, gcol // HW)),
                  pl.BlockSpec((1, HW), lambda i: (0, 0)),
                  pl.BlockSpec((HW, HW), lambda i: (0, 0))],
        out_specs=pl.BlockSpec((TM, HW), row),
        out_shape=jax.ShapeDtypeStruct((n, HW), BF16),
        compiler_params=_cparams(("parallel",)),
        name="gate_norm",
    )(o_f, o_b, z, jnp.tile(nw, HW // HEAD_DIM).reshape(1, HW), _seg_matrix(HW))


DQKV_W = 3 * HW
HALO = 8


def _softplus(x):
    return jnp.maximum(x, 0.0) + jnp.log1p(jnp.exp(-jnp.abs(x)))


def _delta_prep_kernel(x_ref, xp_ref, xn_ref, cw_ref, dab_ref, ex_ref, al_ref, dtb_ref, p_ref, qkv_ref, gb_ref,
                       *, n_single, tiles_per_seq):
    i = pl.program_id(0)
    j = jnp.maximum(i - n_single, 0) % tiles_per_seq
    first = (i < n_single) | (j == 0)
    last = (i < n_single) | (j == tiles_per_seq - 1)
    x = x_ref[...]
    tm = x.shape[0]
    prev = jnp.where(first, 0.0, xp_ref[...])
    nxt = jnp.where(last, 0.0, xn_ref[...])
    xe = jnp.concatenate([prev, x, nxt], axis=0)
    y = None
    for tap in range(CONV_K):
        lo = HALO + tap - CONV_K // 2
        term = cw_ref[tap:tap + 1, :] * xe[lo:lo + tm]
        y = term if y is None else y + term
    y = _silu(y)
    qk = y[:, :2 * HW]
    ssq = jnp.dot(qk * qk, p_ref[...], preferred_element_type=F32, precision=HIGHEST)
    qkn = qk * lax.rsqrt(ssq + EPS)
    qkv_ref[...] = jnp.concatenate([qkn[:, :HW] * (HEAD_DIM ** -0.5), qkn[:, HW:], y[:, 2 * HW:]], axis=1)
    e = jnp.dot(dab_ref[...], ex_ref[...], preferred_element_type=F32, precision=HIGHEST)
    g = -jnp.exp(al_ref[...]) * _softplus(e[:, :2 * HW] + dtb_ref[...])
    gb_ref[...] = jnp.concatenate([g, jax.nn.sigmoid(e[:, 2 * HW:])], axis=1)


def _delta_prep_call(z, conv_w, a_log, dt_bias, n_single, tiles_per_seq):
    n = z.shape[0]
    cb = COL['dq'] // DQKV_W
    hb = TM // HALO
    nhb = n // HALO
    ex = np.zeros((LANES, 4 * HW), np.float32)
    for r in range(4 * D_HEADS):
        ex[r, r * HEAD_DIM:(r + 1) * HEAD_DIM] = 1.0
    cw = jnp.zeros((8, DQKV_W), F32).at[:CONV_K].set(conv_w.astype(F32))
    const = lambda i: (0, 0)
    return pl.pallas_call(
        functools.partial(_delta_prep_kernel, n_single=n_single, tiles_per_seq=tiles_per_seq),
        grid=(n // TM,),
        in_specs=[pl.BlockSpec((TM, DQKV_W), lambda i: (i, cb)),
                  pl.BlockSpec((HALO, DQKV_W), lambda i: (jnp.maximum(i * hb - 1, 0), cb)),
                  pl.BlockSpec((HALO, DQKV_W), lambda i: (jnp.minimum((i + 1) * hb, nhb - 1), cb)),
                  pl.BlockSpec((8, DQKV_W), const),
                  pl.BlockSpec((TM, LANES), lambda i: (i, COL['dab'] // LANES)),
                  pl.BlockSpec((LANES, 4 * HW), const),
                  pl.BlockSpec((1, 2 * HW), const), pl.BlockSpec((1, 2 * HW), const),
                  pl.BlockSpec((2 * HW, 2 * HW), const)],
        out_specs=[pl.BlockSpec((TM, DQKV_W), lambda i: (i, 0)), pl.BlockSpec((TM, 4 * HW), lambda i: (i, 0))],
        out_shape=[jax.ShapeDtypeStruct((n, DQKV_W), F32), jax.ShapeDtypeStruct((n, 4 * HW), F32)],
        compiler_params=_cparams(("parallel",)),
        name="delta_prep",
    )(z, z, z, cw, z, jnp.asarray(ex),
      jnp.repeat(a_log.astype(F32).reshape(-1), HEAD_DIM).reshape(1, 2 * HW),
      jnp.repeat(dt_bias.astype(F32).reshape(-1), HEAD_DIM).reshape(1, 2 * HW), _seg_matrix(2 * HW))


def _delta_consts():
    r = np.arange(HW)
    h = r // HEAD_DIM
    t = r % HEAD_DIM
    same_h = h[:, None] == h[None, :]
    s16 = same_h & (t[:, None] // 16 == t[None, :] // 16)
    s32 = same_h & (t[:, None] // 32 == t[None, :] // 32)
    mats = [same_h, r[:, None] == r[None, :], s16, s32 & ~s16, same_h & ~s32]
    return jnp.asarray(np.stack(mats).astype(np.float32))


def _tri_inv(abd, eye, m16, m32, m64):
    bf = lambda x: x.astype(BF16)
    a16 = bf(abd * m16)
    tin = eye - abd * m16
    p = _dot(a16, a16)
    tin = tin + _dot(bf(tin), bf(p))
    p = _dot(bf(p), bf(p))
    tin = tin + _dot(bf(tin), bf(p))
    p = _dot(bf(p), bf(p))
    tin = tin + _dot(bf(tin), bf(p))
    for m in (m32, m64):
        tb = bf(tin)
        tin = tin - _dot(tb, bf(_dot(bf(abd * m), tb)))
    return tin


def _delta_chunk(qkv, g, beta, st_ref, consts_ref, rev):
    mask = consts_ref[0]
    mask_bf = mask.astype(BF16)
    pt = _scan_rows(rev)
    ps = _scan_lanes(rev)
    q, k, v = qkv[:, :HW], qkv[:, HW:2 * HW], qkv[:, 2 * HW:]
    gc = _cumsum_scan(g, pt, rev)
    t_idx = lax.broadcasted_iota(jnp.int32, (CHUNK, HW), 0)
    s_idx = lax.broadcasted_iota(jnp.int32, (CHUNK, HW), 1) % CHUNK
    gcs = jnp.sum(jnp.where(t_idx == s_idx, gc, 0.0), axis=0, keepdims=True)
    gam = jnp.exp(jnp.minimum(gc - gcs, 0.0))
    g_last = _row(gc, CHUNK - 1, rev)
    egc = jnp.exp(gc)
    kb = k * beta
    r = _dot_t(jnp.concatenate([kb, q], axis=0).astype(BF16), _bd(k, mask_bf))
    a = jnp.where(ps < pt, r[:CHUNK] * gam, 0.0)
    aq = jnp.where(ps <= pt, r[CHUNK:] * gam, 0.0)
    abd = jnp.concatenate([a] * (HW // CHUNK), axis=0) * mask
    tin = _tri_inv(abd, consts_ref[1], consts_ref[2], consts_ref[3], consts_ref[4])
    rhs = jnp.concatenate([_bd(v * beta, mask_bf), _bd(kb * egc, mask_bf)], axis=1)
    uw_bd = _dot(tin.astype(BF16), rhs)
    uw = uw_bd[0:CHUNK] + uw_bd[CHUNK:2 * CHUNK] + uw_bd[2 * CHUNK:3 * CHUNK] + uw_bd[3 * CHUNK:]
    u, w = uw[:, :HW], uw[:, HW:]
    st = st_ref[...]
    ws_qs = _dot_t(jnp.concatenate([w, q * egc], axis=0).astype(BF16), st.astype(BF16))
    v_new = u - ws_qs[:CHUNK]
    o = ws_qs[CHUNK:] + _dot(aq.astype(BF16), _bd(v_new, mask_bf))
    ke = (k * jnp.exp(g_last - gc)).astype(BF16)
    upd = lax.dot_general(v_new.astype(BF16), ke, (((0,), (0,)), ((), ())), preferred_element_type=F32)
    st_ref[...] = jnp.exp(g_last) * st + mask * upd
    return o


def _delta_kernel(xf_ref, gf_ref, bf_ref, xb_ref, gb_ref, bb_ref, consts_ref, *rest, has_state):
    if has_state:
        s0_ref, of_ref, ob_ref, sout_ref, st_s = rest
    else:
        of_ref, ob_ref, sout_ref, st_s = rest
    c = pl.program_id(1)

    @pl.when(c == 0)
    def _():
        st_s[...] = s0_ref[...] if has_state else jnp.zeros(st_s.shape, F32)

    of_ref[...] = _delta_chunk(xf_ref[...], gf_ref[...], bf_ref[...], st_s.at[0], consts_ref, False)
    ob_ref[...] = _delta_chunk(xb_ref[...], gb_ref[...], bb_ref[...], st_s.at[1], consts_ref, True)

    @pl.when(c == pl.num_programs(1) - 1)
    def _():
        sout_ref[...] = st_s[...]


def _delta_call(qkv, gb, row0, nb, t, s0_bd):
    nc = t // CHUNK
    has_state = s0_bd is not None
    r0 = row0 // CHUNK
    fwd = lambda b, c: r0 + b * nc + c
    bwd = lambda b, c: r0 + b * nc + nc - 1 - c
    in_specs = [pl.BlockSpec((CHUNK, DQKV_W), lambda b, c: (fwd(b, c), 0)),
                pl.BlockSpec((CHUNK, HW), lambda b, c: (fwd(b, c), 0)),
                pl.BlockSpec((CHUNK, HW), lambda b, c: (fwd(b, c), 2)),
                pl.BlockSpec((CHUNK, DQKV_W), lambda b, c: (bwd(b, c), 0)),
                pl.BlockSpec((CHUNK, HW), lambda b, c: (bwd(b, c), 1)),
                pl.BlockSpec((CHUNK, HW), lambda b, c: (bwd(b, c), 3)),
                pl.BlockSpec((5, HW, HW), lambda b, c: (0, 0, 0))]
    args = [qkv, gb, gb, qkv, gb, gb, _delta_consts()]
    if has_state:
        in_specs.append(pl.BlockSpec((None, 2, HW, HW), lambda b, c: (b, 0, 0, 0)))
        args.append(s0_bd)
    return pl.pallas_call(
        functools.partial(_delta_kernel, has_state=has_state),
        grid=(nb, nc),
        in_specs=in_specs,
        out_specs=[pl.BlockSpec((CHUNK, HW), lambda b, c: (b * nc + c, 0)),
                   pl.BlockSpec((CHUNK, HW), lambda b, c: (b * nc + nc - 1 - c, 0)),
                   pl.BlockSpec((None, 2, HW, HW), lambda b, c: (b, 0, 0, 0))],
        out_shape=[jax.ShapeDtypeStruct((nb * t, HW), F32), jax.ShapeDtypeStruct((nb * t, HW), F32),
                   jax.ShapeDtypeStruct((nb, 2, HW, HW), F32)],
        scratch_shapes=[pltpu.VMEM((2, HW, HW), F32)],
        compiler_params=_cparams(("parallel", "arbitrary")),
        name="delta_scan",
    )(*args)


def kernel(x_prompt, x_sample, cache_a_k, cache_a_v, cache_b_k, cache_b_v, state_hgrn, state_delta, c, c_ctx, w_mod, b_mod, norm1, norm2, w_in, w_gate, a_qn, a_kn, b_qn, b_kn, b_rpb, c_lb, c_norm, d_conv, d_alog, d_dtbias, d_norm, w_branch, w_out, w_router, b_router, w_gu, b_gu, w_down, b_down):
    nbp, tp, d = x_prompt.shape
    nbs, ts, _ = x_sample.shape
    depth = w_mod.shape[0]
    past = cache_a_k.shape[2]
    n_p = nbp * tp
    n = n_p + nbs * ts
    assert tp == TM and ts % TM == 0 and n_p % ts == 0 and 1 + nbs <= 8 and n % TM_MM == 0
    npt, tps = n_p // TM, ts // TM

    def cond_of_tile(i):
        return jnp.where(i < npt, 0, 1 + (i - npt) // tps)

    x = jnp.concatenate([x_prompt.reshape(n_p, d), x_sample.reshape(nbs * ts, d)], axis=0)
    conds = jnp.zeros((8, d), F32).at[0].set(c_ctx).at[1:1 + nbs].set(c)
    mod = _mod_call(conds, w_mod, b_mod)

    pl_ = jax.nn.softmax(c_lb.astype(F32), axis=0)
    lbs = jnp.cumsum(pl_, axis=0) - pl_[0:1]
    dqkv0 = COL['cg'] + HW
    da0 = dqkv0 + DQKV_W
    dg0 = da0 + 4 * D_HEADS
    w_in_p = jnp.concatenate([w_in[:, :, :dqkv0], w_in[:, :, dg0:dg0 + HW], w_in[:, :, dqkv0:da0],
                              w_in[:, :, da0:dg0],
                              jnp.zeros((depth, d, IN_W_PAD - dg0 - HW), w_in.dtype)], axis=2).astype(BF16)
    w_gate_b = w_gate.astype(BF16)
    w_branch_b = w_branch.astype(BF16)
    w_out_b = w_out.astype(BF16)
    w_router_p = jnp.zeros((depth, d, LANES), F32).at[:, :, :N_EXPERTS].set(w_router)
    b_router_p = jnp.full((depth, 1, LANES), NEG, F32).at[:, 0, :N_EXPERTS].set(b_router)

    tt = jnp.arange(ts, dtype=jnp.int32)
    half = HEAD_DIM // 2
    inv_freq = 1.0 / (ROPE_THETA ** (jnp.arange(0, half, 2, dtype=F32) / half))
    ang_r = (tt // GRID_W).astype(F32)[:, None] * inv_freq[None, :]
    ang_c = (tt % GRID_W).astype(F32)[:, None] * inv_freq[None, :]
    ang = jnp.concatenate([ang_r, ang_r, ang_c, ang_c], axis=-1)
    sign = jnp.asarray(np.where((np.arange(HEAD_DIM) % 32) < 16, -1.0, 1.0).astype(np.float32))
    cos_all = jnp.concatenate([jnp.ones((n_p, HEAD_DIM), F32), jnp.tile(jnp.cos(ang), (nbs, 1))], axis=0)
    sin_all = jnp.concatenate([jnp.zeros((n_p, HEAD_DIM), F32), jnp.tile(jnp.sin(ang) * sign, (nbs, 1))], axis=0)
    rope_tabs = (jnp.tile(cos_all, (1, 2)), jnp.tile(sin_all, (1, 2)))
    scale = HEAD_DIM ** -0.5
    wa, wb = A_KV_HEADS * HEAD_DIM, B_HEADS * HEAD_DIM

    ak_l, av_l, bk_l, bv_l, sc_l, sd_l = [], [], [], [], [], []
    for l in range(depth):
        mod3 = mod[l].reshape(8, 1, N_MOD * d)
        h = _norm_call(x, norm1[l].reshape(1, d), mod3, cond_of_tile)
        z = _mm_call(h, w_in_p[l])

        (qa,) = _prep_call(z, COL['aq'], A_HEADS * HEAD_DIM, a_qn[l], scale, rope_tabs, False)
        ka_n, ka = _prep_call(z, COL['ak'], wa, a_kn[l], 1.0, rope_tabs, True)
        oa_p = _attn_call(qa, ka, z, COL['av'], 0, nbp, tp, A_HEADS, A_KV_HEADS)
        oa_s = _attn_call(qa, ka, z, COL['av'], n_p, nbs, ts, A_HEADS, A_KV_HEADS,
                          ctx=(cache_a_k[:, l].reshape(nbs, past, wa).astype(BF16),
                               cache_a_v[:, l].reshape(nbs, past, wa).astype(BF16)))
        oa = jnp.concatenate([oa_p, oa_s], axis=0)

        (qb,) = _prep_call(z, COL['bq'], wb, b_qn[l], scale, None, False)
        kb_n, kb = _prep_call(z, COL['bk'], wb, b_kn[l], 1.0, None, True)
        ob_p = _attn_call(qb, kb, z, COL['bv'], 0, nbp, tp, B_HEADS, B_HEADS)
        ob_s = _nbr_call(qb, kb, z, COL['bv'], n_p, nbs, ts,
                         cache_b_k[:, l].reshape(nbs, past, wb).astype(BF16),
                         cache_b_v[:, l].reshape(nbs, past, wb).astype(BF16), _nbr_bias_table(b_rpb[l]))
        ob = jnp.concatenate([ob_p, ob_s], axis=0)

        cf_p, cb_p, sc_p = _hgrn_call(z, 0, nbp, tp, lbs[l], None)
        cf_s, cb_s, _ = _hgrn_call(z, n_p, nbs, ts, lbs[l], _state_to_bd(state_hgrn[:, l]))
        oc = _gate_norm_call(jnp.concatenate([cf_p, cf_s], axis=0), jnp.concatenate([cb_p, cb_s], axis=0),
                             z, COL['cg'], 0, c_norm[l])

        qkv, gb = _delta_prep_call(z, d_conv[l], d_alog[l], d_dtbias[l], npt, tps)
        df_p, db_p, sd_p = _delta_call(qkv, gb, 0, nbp, tp, None)
        df_s, db_s, _ = _delta_call(qkv, gb, n_p, nbs, ts, _state_to_bd(state_delta[:, l]))
        od = _gate_norm_call(jnp.concatenate([df_p, df_s], axis=0), jnp.concatenate([db_p, db_s], axis=0),
                             z, COL['dg'], 0, d_norm[l])

        x = _merge_call(x, h, oa, ob, oc, od, w_gate_b[l], w_branch_b[l], w_out_b[l], mod3, cond_of_tile)

        h2, gates, idx = _router_call(x, norm2[l].reshape(1, d), mod3, cond_of_tile, w_router_p[l], b_router_p[l])
        dest, slot_tok, block_e, n_used = _route(idx[:, :TOP_K], n)
        y = _gmm_call(block_e, n_used, h2[slot_tok], w_gu[l], b_gu[l], w_down[l], b_down[l])
        x = _combine_call(x, y[dest.reshape(n, TOP_K).T], gates, mod3, cond_of_tile)

        ak_l.append(ka_n[:n_p].reshape(nbp, tp, A_KV_HEADS, HEAD_DIM))
        av_l.append(z[:n_p, COL['av']:COL['av'] + wa].reshape(nbp, tp, A_KV_HEADS, HEAD_DIM))
        bk_l.append(kb_n[:n_p].reshape(nbp, tp, B_HEADS, HEAD_DIM))
        bv_l.append(z[:n_p, COL['bv']:COL['bv'] + wb].reshape(nbp, tp, B_HEADS, HEAD_DIM))
        sc_l.append(_state_from_bd(sc_p))
        sd_l.append(_state_from_bd(sd_p))

    return (x[:n_p].reshape(nbp, tp, d), x[n_p:].reshape(nbs, ts, d),
            jnp.stack(ak_l, axis=1), jnp.stack(av_l, axis=1), jnp.stack(bk_l, axis=1), jnp.stack(bv_l, axis=1),
            jnp.stack(sc_l, axis=1), jnp.stack(sd_l, axis=1))
```

```python
import functools

import numpy as np
import jax
import jax.numpy as jnp
from jax import lax
from jax.experimental import pallas as pl
from jax.experimental.pallas import tpu as pltpu

F32 = jnp.float32
BF16 = jnp.bfloat16
HIGHEST = lax.Precision.HIGHEST

GRID_W = 64
HEAD_DIM = 64
A_HEADS = 8
A_KV_HEADS = 2
B_HEADS = 4
NB_ROWS = 8
NB_COLS = 16
C_HEADS = 4
D_HEADS = 4
CONV_K = 5
CHUNK = 64
N_EXPERTS = 32
TOP_K = 4
SWIGLU_LIMIT = 7.0
SWIGLU_ALPHA = 1.702
ROPE_THETA = 10000.0
EPS = 1e-6
MAX_INPUT_GATE = 1.0 - 1e-6
N_MOD = 6
HW = 256

LANES = 128
VMEM_LIMIT = 56 * 1024 * 1024
NEG = -1e30

TM = 256
TM_MM = 512
TN_MM = 1024
TQ = 128
MOE_TM = 256

COL = dict(aq=0, ak=512, av=640, bq=768, bk=1024, bv=1280, cq=1536, ci=1792, cff=2048, cfb=2304,
           cg=2560, dg=2816, dq=3072, dk=3328, dv=3584, dab=3840)
IN_W_PAD = 4096


def _cparams(sem, vmem=VMEM_LIMIT):
    return pltpu.CompilerParams(dimension_semantics=sem, vmem_limit_bytes=vmem)


def _silu(x):
    return x * jax.nn.sigmoid(x)


def _dot(a, b):
    return jnp.dot(a, b, preferred_element_type=F32)


def _dot_t(a, b):
    return lax.dot_general(a, b, (((1,), (1,)), ((), ())), preferred_element_type=F32)


def _mod_kernel(c_ref, w_ref, b_ref, o_ref):
    a = _silu(c_ref[...]).astype(BF16)
    o_ref[...] = _dot(a, w_ref[...].astype(BF16)) + b_ref[...]


def _mod_call(conds, w_mod, b_mod):
    depth, d, n = w_mod.shape
    tn = 1024
    return pl.pallas_call(
        _mod_kernel,
        grid=(depth, n // tn),
        in_specs=[pl.BlockSpec((8, d), lambda l, j: (0, 0)),
                  pl.BlockSpec((None, d, tn), lambda l, j: (l, 0, j)),
                  pl.BlockSpec((None, 1, tn), lambda l, j: (l, 0, j))],
        out_specs=pl.BlockSpec((None, 8, tn), lambda l, j: (l, 0, j)),
        out_shape=jax.ShapeDtypeStruct((depth, 8, n), F32),
        compiler_params=_cparams(("parallel", "parallel")),
        name="mod",
    )(conds, w_mod, b_mod.reshape(depth, 1, n))


def _mod_norm(x, g, m, k_shift, k_scale, d):
    y = x * lax.rsqrt(jnp.mean(x * x, axis=-1, keepdims=True) + EPS) * g
    return y * (1.0 + m[:, k_scale * d:(k_scale + 1) * d]) + m[:, k_shift * d:(k_shift + 1) * d]


def _norm_kernel(x_ref, g_ref, mod_ref, h_ref, *, d):
    h_ref[...] = _mod_norm(x_ref[...], g_ref[...], mod_ref[...], 0, 1, d).astype(BF16)


def _norm_call(x, g, mod3, cond_of_tile):
    n, d = x.shape
    return pl.pallas_call(
        functools.partial(_norm_kernel, d=d),
        grid=(n // TM,),
        in_specs=[pl.BlockSpec((TM, d), lambda i: (i, 0)),
                  pl.BlockSpec((1, d), lambda i: (0, 0)),
                  pl.BlockSpec((None, 1, N_MOD * d), lambda i: (cond_of_tile(i), 0, 0))],
        out_specs=pl.BlockSpec((TM, d), lambda i: (i, 0)),
        out_shape=jax.ShapeDtypeStruct((n, d), BF16),
        compiler_params=_cparams(("parallel",)),
        name="norm1",
    )(x, g, mod3)


def _mm_kernel(a_ref, w_ref, o_ref):
    o_ref[...] = _dot(a_ref[...], w_ref[...]).astype(o_ref.dtype)


def _mm_call(a, w, out_dtype=F32):
    m, k = a.shape
    n = w.shape[1]
    return pl.pallas_call(
        _mm_kernel,
        grid=(n // TN_MM, m // TM_MM),
        in_specs=[pl.BlockSpec((TM_MM, k), lambda j, i: (i, 0)),
                  pl.BlockSpec((k, TN_MM), lambda j, i: (0, j))],
        out_specs=pl.BlockSpec((TM_MM, TN_MM), lambda j, i: (i, j)),
        out_shape=jax.ShapeDtypeStruct((m, n), out_dtype),
        compiler_params=_cparams(("parallel", "parallel")),
        name="in_proj",
    )(a, w)


def _merge_kernel(x_ref, h_ref, oa_ref, ob_ref, oc_ref, od_ref, wg_ref, wb_ref, wo_ref, mod_ref, o_ref, *, d):
    h = h_ref[...]
    outs = (oa_ref, ob_ref, oc_ref, od_ref)
    merged = None
    off = 0
    for i in range(4):
        wdt = outs[i].shape[1]
        gate = jax.nn.sigmoid(_dot(h, wg_ref[:, i * d:(i + 1) * d]))
        br = gate * _dot(outs[i][...], wb_ref[off:off + wdt, :])
        merged = br if merged is None else merged + br
        off += wdt
    y = _dot(merged.astype(BF16), wo_ref[...])
    o_ref[...] = x_ref[...] + mod_ref[...][:, 2 * d:3 * d] * y


def _merge_call(x, h, oa, ob, oc, od, w_gate, w_branch, w_out, mod3, cond_of_tile):
    n, d = x.shape
    row = lambda i: (i, 0)
    const = lambda i: (0, 0)
    once = pl.Buffered(1)
    return pl.pallas_call(
        functools.partial(_merge_kernel, d=d),
        grid=(n // TM,),
        in_specs=[pl.BlockSpec((TM, d), row), pl.BlockSpec((TM, d), row),
                  pl.BlockSpec((TM, oa.shape[1]), row), pl.BlockSpec((TM, ob.shape[1]), row),
                  pl.BlockSpec((TM, oc.shape[1]), row), pl.BlockSpec((TM, od.shape[1]), row),
                  pl.BlockSpec(w_gate.shape, const, pipeline_mode=once),
                  pl.BlockSpec(w_branch.shape, const, pipeline_mode=once),
                  pl.BlockSpec(w_out.shape, const, pipeline_mode=once),
                  pl.BlockSpec((None, 1, N_MOD * d), lambda i: (cond_of_tile(i), 0, 0))],
        out_specs=pl.BlockSpec((TM, d), row),
        out_shape=jax.ShapeDtypeStruct((n, d), F32),
        compiler_params=_cparams(("parallel",)),
        name="merge",
    )(x, h, oa, ob, oc, od, w_gate, w_branch, w_out, mod3)


def _router_kernel(x_ref, g_ref, mod_ref, wr_ref, br_ref, ltri_ref, h_ref, gate_ref, sel_ref, cnt_ref, run_s, *, d):
    i = pl.program_id(0)

    @pl.when(i == 0)
    def _():
        run_s[...] = jnp.zeros(run_s.shape, F32)

    h2 = _mod_norm(x_ref[...], g_ref[...], mod_ref[...], 3, 4, d)
    for sub in range(d // LANES):
        h_ref[:, sub, :] = h2[:, sub * LANES:(sub + 1) * LANES]
    logits = jnp.dot(h2, wr_ref[...], preferred_element_type=F32, precision=HIGHEST) + br_ref[...]
    lane = lax.broadcasted_iota(jnp.int32, logits.shape, 1)
    vals, idxs = [], []
    cur = logits
    for _ in range(TOP_K):
        m = jnp.max(cur, axis=-1, keepdims=True)
        ix = jnp.min(jnp.where(cur == m, lane, LANES), axis=-1, keepdims=True)
        vals.append(m)
        idxs.append(ix)
        cur = jnp.where(lane == ix, -jnp.inf, cur)
    es = [jnp.exp(v - vals[0]) for v in vals]
    tot = es[0] + es[1] + es[2] + es[3]
    gates = jnp.zeros(logits.shape, F32)
    sel = jnp.zeros(logits.shape, jnp.int32)
    base = run_s[...]
    ltri = ltri_ref[...]
    for k in range(TOP_K):
        onehot = jnp.where(lane == idxs[k], 1.0, 0.0)
        before = base + _dot(ltri, onehot.astype(BF16))
        rank = jnp.sum(onehot * before, axis=-1, keepdims=True).astype(jnp.int32)
        base = base + jnp.sum(onehot, axis=0, keepdims=True)
        gates = jnp.where(lane == k, es[k] / tot, gates)
        sel = jnp.where(lane == k, idxs[k], sel)
        sel = jnp.where(lane == TOP_K + k, rank, sel)
    run_s[...] = base
    gate_ref[...] = gates
    sel_ref[...] = sel
    cnt_ref[...] = base


def _router_call(x, g, mod3, cond_of_tile, w_router_p, b_router_p):
    n, d = x.shape
    row = lambda i: (i, 0)
    const = lambda i: (0, 0)
    ltri = jnp.asarray(np.tril(np.ones((TM, TM), np.float32), -1)).astype(BF16)
    return pl.pallas_call(
        functools.partial(_router_kernel, d=d),
        grid=(n // TM,),
        in_specs=[pl.BlockSpec((TM, d), row),
                  pl.BlockSpec((1, d), const),
                  pl.BlockSpec((None, 1, N_MOD * d), lambda i: (cond_of_tile(i), 0, 0)),
                  pl.BlockSpec((d, LANES), const),
                  pl.BlockSpec((1, LANES), const),
                  pl.BlockSpec((TM, TM), const)],
        out_specs=[pl.BlockSpec((TM, d // LANES, LANES), lambda i: (i, 0, 0)), pl.BlockSpec((TM, LANES), row),
                   pl.BlockSpec((TM, LANES), row), pl.BlockSpec((1, LANES), const)],
        out_shape=[jax.ShapeDtypeStruct((n, d // LANES, LANES), F32), jax.ShapeDtypeStruct((n, LANES), F32),
                   jax.ShapeDtypeStruct((n, LANES), jnp.int32), jax.ShapeDtypeStruct((1, LANES), F32)],
        scratch_shapes=[pltpu.VMEM((1, LANES), F32)],
        compiler_params=_cparams(("arbitrary",)),
        name="router",
    )(x, g, mod3, w_router_p, b_router_p, ltri)


def _slot_plan(sel, counts, n_tok):
    cnt = counts[0, :N_EXPERTS].astype(jnp.int32)
    padded = (cnt + MOE_TM - 1) // MOE_TM * MOE_TM
    pad_end = jnp.cumsum(padded)
    pad_start = pad_end - padded
    dest = (pad_start[sel[:, :TOP_K]] + sel[:, TOP_K:2 * TOP_K]).reshape(n_tok * TOP_K)
    n_blocks = n_tok * TOP_K // MOE_TM + N_EXPERTS
    starts = jnp.arange(n_blocks, dtype=jnp.int32) * MOE_TM
    block_e = jnp.minimum(jnp.sum((pad_end[None, :] <= starts[:, None]).astype(jnp.int32), axis=1), N_EXPERTS - 1)
    n_used = (pad_end[-1:] // MOE_TM).astype(jnp.int32)
    return dest, block_e, n_used, n_blocks


def _row_copy_wait(src_ref, dst_ref, sem, count):
    def body(_, carry):
        pltpu.make_async_copy(src_ref.at[0], dst_ref.at[0], sem).wait()
        return carry
    lax.fori_loop(0, count, body, 0)


def _dispatch_kernel(dest_ref, h_ref, zero_ref, o_ref, sem):
    del zero_ref
    tm = h_ref.shape[0]
    base = pl.program_id(0) * (tm * TOP_K)

    def body(t, carry):
        for k in range(TOP_K):
            pltpu.make_async_copy(h_ref.at[t], o_ref.at[dest_ref[base + t * TOP_K + k]], sem).start()
        return carry

    lax.fori_loop(0, tm, body, 0)
    _row_copy_wait(h_ref, o_ref, sem, tm * TOP_K)


def _dispatch_call(dest, h3, n_slots):
    n, s, _ = h3.shape
    grid_spec = pltpu.PrefetchScalarGridSpec(
        num_scalar_prefetch=1,
        grid=(n // TM,),
        in_specs=[pl.BlockSpec((TM, s, LANES), lambda i, dst: (i, 0, 0)),
                  pl.BlockSpec(memory_space=pl.ANY)],
        out_specs=pl.BlockSpec(memory_space=pl.ANY),
        scratch_shapes=[pltpu.SemaphoreType.DMA(())],
    )
    zeros = jnp.zeros((n_slots, s, LANES), F32)
    return pl.pallas_call(
        _dispatch_kernel,
        grid_spec=grid_spec,
        out_shape=jax.ShapeDtypeStruct((n_slots, s, LANES), F32),
        input_output_aliases={2: 0},
        compiler_params=_cparams(("arbitrary",)),
        name="dispatch",
    )(dest, h3, zeros)


def _gmm_kernel(be_ref, nu_ref, x_ref, wgu_ref, bgu_ref, wd_ref, bd_ref, y_ref, wgu_s, wd_s, x_s, *, f):
    i = pl.program_id(0)
    used = i < nu_ref[0]
    changed = (i == 0) | (be_ref[i] != be_ref[jnp.maximum(i - 1, 0)])
    nsub = x_ref.shape[1]

    @pl.when(used & changed)
    def _():
        wgu_s[...] = wgu_ref[...].astype(BF16)
        wd_s[...] = wd_ref[...].astype(BF16)

    @pl.when(used)
    def _():
        for sub in range(nsub):
            x_s[:, sub * LANES:(sub + 1) * LANES] = x_ref[:, sub, :].astype(BF16)
        gu = _dot(x_s[...], wgu_s[...]) + bgu_ref[...]
        gate = jnp.minimum(gu[:, :f], SWIGLU_LIMIT)
        up = jnp.clip(gu[:, f:], -SWIGLU_LIMIT, SWIGLU_LIMIT)
        act = gate * jax.nn.sigmoid(SWIGLU_ALPHA * gate) * (up + 1.0)
        y = _dot(act.astype(BF16), wd_s[...]) + bd_ref[...]
        for sub in range(nsub):
            y_ref[:, sub, :] = y[:, sub * LANES:(sub + 1) * LANES]

    @pl.when(jnp.logical_not(used))
    def _():
        y_ref[...] = jnp.zeros(y_ref.shape, y_ref.dtype)


def _gmm_call(block_e, n_used, x_slots, w_gu, b_gu, w_down, b_down):
    n_slots, s, _ = x_slots.shape
    e, d, f2 = w_gu.shape
    f = f2 // 2
    n_blocks = n_slots // MOE_TM
    grid_spec = pltpu.PrefetchScalarGridSpec(
        num_scalar_prefetch=2,
        grid=(n_blocks,),
        in_specs=[pl.BlockSpec((MOE_TM, s, LANES), lambda i, be, nu: (i, 0, 0)),
                  pl.BlockSpec((None, d, f2), lambda i, be, nu: (be[i], 0, 0)),
                  pl.BlockSpec((None, 1, f2), lambda i, be, nu: (be[i], 0, 0)),
                  pl.BlockSpec((None, f, d), lambda i, be, nu: (be[i], 0, 0)),
                  pl.BlockSpec((None, 1, d), lambda i, be, nu: (be[i], 0, 0))],
        out_specs=pl.BlockSpec((MOE_TM, s, LANES), lambda i, be, nu: (i, 0, 0)),
        scratch_shapes=[pltpu.VMEM((d, f2), BF16), pltpu.VMEM((f, d), BF16), pltpu.VMEM((MOE_TM, d), BF16)],
    )
    return pl.pallas_call(
        functools.partial(_gmm_kernel, f=f),
        grid_spec=grid_spec,
        out_shape=jax.ShapeDtypeStruct((n_slots, s, LANES), F32),
        compiler_params=_cparams(("arbitrary",)),
        name="experts",
    )(block_e, n_used, x_slots, w_gu, b_gu.reshape(e, 1, f2), w_down, b_down.reshape(e, 1, d))


def _combine_kernel(dest_ref, x_ref, gate_ref, mod_ref, y_ref, o_ref, *rest, d):
    bufs, sem = rest[:TOP_K], rest[TOP_K]
    tm = x_ref.shape[0]
    base = pl.program_id(0) * (tm * TOP_K)

    def body(t, carry):
        for k in range(TOP_K):
            pltpu.make_async_copy(y_ref.at[dest_ref[base + t * TOP_K + k]], bufs[k].at[t], sem).start()
        return carry

    lax.fori_loop(0, tm, body, 0)
    _row_copy_wait(y_ref, bufs[0], sem, tm * TOP_K)
    g = gate_ref[...]
    g2 = mod_ref[...][:, 5 * d:6 * d]
    for sub in range(d // LANES):
        sl = slice(sub * LANES, (sub + 1) * LANES)
        acc = g[:, 0:1] * bufs[0][:, sub, :]
        for k in range(1, TOP_K):
            acc = acc + g[:, k:k + 1] * bufs[k][:, sub, :]
        o_ref[:, sl] = x_ref[:, sl] + g2[:, sl] * acc


def _combine_call(dest, x, gates, mod3, cond_of_tile, y_slots):
    n, d = x.shape
    s = d // LANES
    row = lambda i, dst: (i, 0)
    grid_spec = pltpu.PrefetchScalarGridSpec(
        num_scalar_prefetch=1,
        grid=(n // TM,),
        in_specs=[pl.BlockSpec((TM, d), row),
                  pl.BlockSpec((TM, LANES), row),
                  pl.BlockSpec((None, 1, N_MOD * d), lambda i, dst: (cond_of_tile(i), 0, 0)),
                  pl.BlockSpec(memory_space=pl.ANY)],
        out_specs=pl.BlockSpec((TM, d), row),
        scratch_shapes=[pltpu.VMEM((TM, s, LANES), F32)] * TOP_K + [pltpu.SemaphoreType.DMA(())],
    )
    return pl.pallas_call(
        functools.partial(_combine_kernel, d=d),
        grid_spec=grid_spec,
        out_shape=jax.ShapeDtypeStruct((n, d), F32),
        compiler_params=_cparams(("arbitrary",)),
        name="combine",
    )(dest, x, gates, mod3, y_slots)


def _seg_matrix(w):
    i = np.arange(w) // HEAD_DIM
    return jnp.asarray((i[:, None] == i[None, :]).astype(np.float32))


def _head_mean_sq(x, p):
    return jnp.dot(x * x, p, preferred_element_type=F32, precision=HIGHEST) * (1.0 / HEAD_DIM)


def _prep_kernel(z_ref, nw_ref, p_ref, *rest, scale, rope, want_norm):
    if rope:
        cos_ref, sin_ref = rest[0], rest[1]
        rest = rest[2:]
    x = z_ref[...]
    w = x.shape[1]
    y = x * lax.rsqrt(_head_mean_sq(x, p_ref[...]) + EPS) * nw_ref[...]
    if want_norm:
        rest[0][...] = y
    out_ref = rest[-1]
    if rope:
        reps = w // LANES
        cos = jnp.concatenate([cos_ref[...]] * reps, axis=1) if reps > 1 else cos_ref[...]
        sin = jnp.concatenate([sin_ref[...]] * reps, axis=1) if reps > 1 else sin_ref[...]
        lane = lax.broadcasted_iota(jnp.int32, y.shape, 1)
        first = (lane % 32) < 16
        rot = jnp.where(first, pltpu.roll(y, w - 16, axis=1), pltpu.roll(y, 16, axis=1))
        y = y * cos + rot * sin
    out_ref[...] = (y * scale).astype(BF16)


def _prep_call(z, col, w, nw, scale, rope_tabs, want_norm):
    n = z.shape[0]
    cb = col // w
    row = lambda i: (i, 0)
    in_specs = [pl.BlockSpec((TM, w), lambda i: (i, cb)),
                pl.BlockSpec((1, w), lambda i: (0, 0)),
                pl.BlockSpec((w, w), lambda i: (0, 0))]
    args = [z, jnp.tile(nw, w // HEAD_DIM).reshape(1, w), _seg_matrix(w)]
    if rope_tabs is not None:
        in_specs += [pl.BlockSpec((TM, LANES), row), pl.BlockSpec((TM, LANES), row)]
        args += list(rope_tabs)
    out_specs, out_shape = [], []
    if want_norm:
        out_specs.append(pl.BlockSpec((TM, w), row))
        out_shape.append(jax.ShapeDtypeStruct((n, w), F32))
    out_specs.append(pl.BlockSpec((TM, w), row))
    out_shape.append(jax.ShapeDtypeStruct((n, w), BF16))
    return pl.pallas_call(
        functools.partial(_prep_kernel, scale=scale, rope=rope_tabs is not None, want_norm=want_norm),
        grid=(n // TM,),
        in_specs=in_specs, out_specs=out_specs, out_shape=out_shape,
        compiler_params=_cparams(("parallel",)),
        name="qk_prep",
    )(*args)


def _attn_kernel(q_ref, k_ref, v_ref, *rest, hq, hk, has_ctx):
    if has_ctx:
        ck_ref, cv_ref, o_ref = rest
    else:
        (o_ref,) = rest
    g = hq // hk
    tq = q_ref.shape[0]
    dh = HEAD_DIM
    outs = []
    for j in range(hk):
        q4 = jnp.concatenate([q_ref[:, (j * g + gi) * dh:(j * g + gi + 1) * dh] for gi in range(g)], axis=0)
        s = _dot_t(q4, k_ref[:, j * dh:(j + 1) * dh])
        m = jnp.max(s, axis=-1, keepdims=True)
        if has_ctx:
            sc = _dot_t(q4, ck_ref[:, j * dh:(j + 1) * dh])
            m = jnp.maximum(m, jnp.max(sc, axis=-1, keepdims=True))
        p = jnp.exp(s - m)
        l = jnp.sum(p, axis=-1, keepdims=True)
        o = _dot(p.astype(BF16), v_ref[:, j * dh:(j + 1) * dh].astype(BF16))
        if has_ctx:
            pc = jnp.exp(sc - m)
            l = l + jnp.sum(pc, axis=-1, keepdims=True)
            o = o + _dot(pc.astype(BF16), cv_ref[:, j * dh:(j + 1) * dh])
        o = o / l
        outs += [o[gi * tq:(gi + 1) * tq] for gi in range(g)]
    o_ref[...] = jnp.concatenate(outs, axis=1).astype(o_ref.dtype)


def _attn_call(q, k, v, vcol, row0, nb, t, hq, hk, ctx=None):
    tq = min(TQ, t)
    wq, wk = hq * HEAD_DIM, hk * HEAD_DIM
    qb0, kb0, vb = row0 // tq, row0 // t, vcol // wk
    nq = t // tq
    in_specs = [pl.BlockSpec((tq, wq), lambda b, i: (qb0 + b * nq + i, 0)),
                pl.BlockSpec((t, wk), lambda b, i: (kb0 + b, 0)),
                pl.BlockSpec((t, wk), lambda b, i: (kb0 + b, vb))]
    args = [q, k, v]
    if ctx is not None:
        p = ctx[0].shape[1]
        in_specs += [pl.BlockSpec((None, p, wk), lambda b, i: (b, 0, 0))] * 2
        args += list(ctx)
    return pl.pallas_call(
        functools.partial(_attn_kernel, hq=hq, hk=hk, has_ctx=ctx is not None),
        grid=(nb, nq),
        in_specs=in_specs,
        out_specs=pl.BlockSpec((tq, wq), lambda b, i: (b * nq + i, 0)),
        out_shape=jax.ShapeDtypeStruct((nb * t, wq), BF16),
        compiler_params=_cparams(("parallel", "parallel")),
        name="attn",
    )(*args)


def _nbr_bias_table(rpb):
    o = np.arange(NB_ROWS)[:, None, None, None]
    cc = np.arange(GRID_W)[None, :, None, None]
    i = np.arange(NB_ROWS)[None, None, :, None]
    j = np.arange(GRID_W)[None, None, None, :]
    col_start = np.clip(cc - NB_COLS // 2, 0, GRID_W - NB_COLS)
    valid = (j >= col_start) & (j < col_start + NB_COLS)
    full = (NB_ROWS, GRID_W, NB_ROWS, GRID_W)
    dr = np.broadcast_to(i - o + NB_ROWS - 1, full)
    dc = np.clip(np.broadcast_to(j - cc + NB_COLS - 1, full), 0, 2 * NB_COLS - 2)
    tab = rpb.astype(F32)[:, dr, dc]
    tab = jnp.where(jnp.asarray(np.broadcast_to(valid, full))[None], tab, NEG)
    return tab.reshape(rpb.shape[0], NB_ROWS, GRID_W, NB_ROWS * GRID_W)


def _nbr_kernel(q_ref, k_ref, v_ref, ck_ref, cv_ref, bias_ref, o_ref, *, rows):
    r = pl.program_id(1)
    rs = jnp.clip(r - NB_ROWS // 2, 0, rows - NB_ROWS)
    start = pl.multiple_of(rs * GRID_W, GRID_W)
    kw = k_ref[pl.ds(start, NB_ROWS * GRID_W), :]
    vw = v_ref[pl.ds(start, NB_ROWS * GRID_W), :].astype(BF16)
    dh = HEAD_DIM
    outs = []
    for h in range(B_HEADS):
        sl = slice(h * dh, (h + 1) * dh)
        qh = q_ref[:, sl]
        s_loc = _dot_t(qh, kw[:, sl]) + bias_ref[h]
        s_ctx = _dot_t(qh, ck_ref[:, sl])
        m = jnp.maximum(jnp.max(s_loc, axis=-1, keepdims=True), jnp.max(s_ctx, axis=-1, keepdims=True))
        p_loc = jnp.exp(s_loc - m)
        p_ctx = jnp.exp(s_ctx - m)
        l = jnp.sum(p_loc, axis=-1, keepdims=True) + jnp.sum(p_ctx, axis=-1, keepdims=True)
        o = _dot(p_loc.astype(BF16), vw[:, sl]) + _dot(p_ctx.astype(BF16), cv_ref[:, sl])
        outs.append(o / l)
    o_ref[...] = jnp.concatenate(outs, axis=1).astype(o_ref.dtype)


def _nbr_call(q, k, z, vcol, row0, nb, t, ck, cv, bias):
    rows = t // GRID_W
    assert rows >= NB_ROWS
    w = B_HEADS * HEAD_DIM
    p = ck.shape[1]
    nwin = NB_ROWS * GRID_W

    def bias_map(b, r):
        return (0, r - jnp.clip(r - NB_ROWS // 2, 0, rows - NB_ROWS), 0, 0)

    return pl.pallas_call(
        functools.partial(_nbr_kernel, rows=rows),
        grid=(nb, rows),
        in_specs=[pl.BlockSpec((GRID_W, w), lambda b, r: (row0 // GRID_W + b * rows + r, 0)),
                  pl.BlockSpec((t, w), lambda b, r: (row0 // t + b, 0)),
                  pl.BlockSpec((t, w), lambda b, r: (row0 // t + b, vcol // w)),
                  pl.BlockSpec((None, p, w), lambda b, r: (b, 0, 0)),
                  pl.BlockSpec((None, p, w), lambda b, r: (b, 0, 0)),
                  pl.BlockSpec((B_HEADS, None, GRID_W, nwin), bias_map)],
        out_specs=pl.BlockSpec((GRID_W, w), lambda b, r: (b * rows + r, 0)),
        out_shape=jax.ShapeDtypeStruct((nb * t, w), BF16),
        compiler_params=_cparams(("parallel", "arbitrary")),
        name="nbr_attn",
    )(q, k, z, ck, cv, bias)


def _head_mask():
    i = np.arange(HW) // HEAD_DIM
    return jnp.asarray((i[:, None] == i[None, :]).astype(np.float32))


def _scan_rows(rev):
    t = lax.broadcasted_iota(jnp.int32, (CHUNK, HW), 0)
    return (CHUNK - 1 - t) if rev else t


def _scan_lanes(rev):
    s = lax.broadcasted_iota(jnp.int32, (CHUNK, HW), 1) % CHUNK
    return (CHUNK - 1 - s) if rev else s


def _sh(x, d, rev):
    s = (-d if rev else d) % CHUNK
    return x if s == 0 else pltpu.roll(x, s, axis=0)


def _cumsum_scan(x, pt, rev):
    d = 1
    while d < CHUNK:
        x = x + jnp.where(pt >= d, _sh(x, d, rev), 0.0)
        d *= 2
    return x


def _block_end(x, pt, n, rev):
    r = (n - 1) - pt % n
    bit = 1
    while bit < n:
        x = jnp.where((r & bit) != 0, _sh(x, -bit, rev), x)
        bit *= 2
    return x


def _pick16(x, j, rev):
    jj = (15 - j) if rev else j
    x3 = x.reshape(CHUNK // 16, 16, HW)
    return jnp.broadcast_to(x3[:, jj:jj + 1, :], x3.shape).reshape(CHUNK, HW)


def _row(x, p, rev):
    t = (CHUNK - 1 - p) if rev else p
    return x[t:t + 1, :]


def _bd(x, mask_bf):
    xb = x.astype(BF16)
    return jnp.concatenate([xb] * (HW // CHUNK), axis=0) * mask_bf


def _gla_chunk(q_raw, v, f_raw, lb, st_ref, mask, rev):
    mask_bf = mask.astype(BF16)
    pt = _scan_rows(rev)
    ps = _scan_lanes(rev)
    q = _silu(q_raw)
    kk = jnp.minimum((1.0 - lb) * jax.nn.sigmoid(-f_raw), MAX_INPUT_GATE)
    b = _cumsum_scan(jnp.log1p(-kk), pt, rev)
    b_last = _row(b, CHUNK - 1, rev)
    e4 = _block_end(b, pt, 4, rev)
    r4 = jnp.where(pt >= 4, _sh(e4, 4, rev), 0.0)
    r16 = jnp.where(pt >= 16, _sh(_pick16(b, 15, rev), 16, rev), 0.0)
    q0 = (q * jnp.exp(b - r4)).astype(BF16)
    q2 = (q * jnp.exp(b - r16)).astype(BF16)
    qe = (q * jnp.exp(b)).astype(BF16)
    ks01 = [_bd(kk * jnp.exp(r4 - b), mask_bf)]
    ks2 = []
    for i in range(1, 4):
        ks01.append(_bd(kk * jnp.exp(jnp.minimum(_pick16(r4, 4 * i, rev) - b, 0.0)), mask_bf))
        ks2.append(_bd(kk * jnp.exp(jnp.minimum(_row(b, 16 * i - 1, rev) - b, 0.0)), mask_bf))
    r01 = _dot_t(q0, jnp.concatenate(ks01, axis=0))
    r2 = _dot_t(q2, jnp.concatenate(ks2, axis=0))
    pt4, ps4, pt16, ps16 = pt // 4, ps // 4, pt // 16, ps // 16
    attn = jnp.where((pt4 == ps4) & (ps <= pt), r01[:, :HW], 0.0)
    for i in range(1, 4):
        attn = jnp.where((pt16 == ps16) & (ps4 < pt4) & (pt4 % 4 == i), r01[:, i * HW:(i + 1) * HW], attn)
        attn = jnp.where((ps16 < pt16) & (pt16 == i), r2[:, (i - 1) * HW:i * HW], attn)
    st = st_ref[...]
    o = _dot_t(qe, st.astype(BF16)) + _dot(attn.astype(BF16), _bd(v, mask_bf))
    ke = (kk * jnp.exp(b_last - b)).astype(BF16)
    upd = lax.dot_general(v.astype(BF16), ke, (((0,), (0,)), ((), ())), preferred_element_type=F32)
    st_ref[...] = jnp.exp(b_last) * st + mask * upd
    return o


def _hgrn_kernel(qf_ref, vf_ref, ff_ref, qb_ref, vb_ref, fb_ref, lb_ref, mask_ref, *rest, has_state):
    if has_state:
        s0_ref, of_ref, ob_ref, sout_ref, st_s = rest
    else:
        of_ref, ob_ref, sout_ref, st_s = rest
    c = pl.program_id(1)

    @pl.when(c == 0)
    def _():
        st_s[...] = s0_ref[...] if has_state else jnp.zeros(st_s.shape, F32)

    mask = mask_ref[...]
    of_ref[...] = _gla_chunk(qf_ref[...], vf_ref[...], ff_ref[...], lb_ref[0:1, :], st_s.at[0], mask, False)
    ob_ref[...] = _gla_chunk(qb_ref[...], vb_ref[...], fb_ref[...], lb_ref[1:2, :], st_s.at[1], mask, True)

    @pl.when(c == pl.num_programs(1) - 1)
    def _():
        sout_ref[...] = st_s[...]


def _state_to_bd(s):
    nb = s.shape[0]
    eye = jnp.eye(C_HEADS, dtype=F32)
    return jnp.einsum('bdhkv,hg->bdhvgk', s.astype(F32), eye).reshape(nb, 2, HW, HW)


def _state_from_bd(sbd):
    nb = sbd.shape[0]
    s6 = sbd.reshape(nb, 2, C_HEADS, HEAD_DIM, C_HEADS, HEAD_DIM)
    return jnp.stack([s6[:, :, h, :, h, :] for h in range(C_HEADS)], axis=2).transpose(0, 1, 2, 4, 3)


def _scan_specs(row0, nc, cols_f, cols_b):
    fwd = [pl.BlockSpec((CHUNK, HW), functools.partial(lambda b, c, cb: (row0 // CHUNK + b * nc + c, cb), cb=col // HW))
           for col in cols_f]
    bwd = [pl.BlockSpec((CHUNK, HW), functools.partial(lambda b, c, cb: (row0 // CHUNK + b * nc + nc - 1 - c, cb),
                                                      cb=col // HW))
           for col in cols_b]
    return fwd + bwd


def _hgrn_call(z, row0, nb, t, lb, s0_bd):
    nc = t // CHUNK
    has_state = s0_bd is not None
    const2 = lambda b, c: (0, 0)
    in_specs = _scan_specs(row0, nc, (COL['cq'], COL['ci'], COL['cff']), (COL['cq'], COL['ci'], COL['cfb']))
    in_specs += [pl.BlockSpec((2, HW), const2), pl.BlockSpec((HW, HW), const2)]
    args = [z] * 6 + [lb, _head_mask()]
    if has_state:
        in_specs.append(pl.BlockSpec((None, 2, HW, HW), lambda b, c: (b, 0, 0, 0)))
        args.append(s0_bd)
    return pl.pallas_call(
        functools.partial(_hgrn_kernel, has_state=has_state),
        grid=(nb, nc),
        in_specs=in_specs,
        out_specs=[pl.BlockSpec((CHUNK, HW), lambda b, c: (b * nc + c, 0)),
                   pl.BlockSpec((CHUNK, HW), lambda b, c: (b * nc + nc - 1 - c, 0)),
                   pl.BlockSpec((None, 2, HW, HW), lambda b, c: (b, 0, 0, 0))],
        out_shape=[jax.ShapeDtypeStruct((nb * t, HW), F32), jax.ShapeDtypeStruct((nb * t, HW), F32),
                   jax.ShapeDtypeStruct((nb, 2, HW, HW), F32)],
        scratch_shapes=[pltpu.VMEM((2, HW, HW), F32)],
        compiler_params=_cparams(("parallel", "arbitrary")),
        name="hgrn_scan",
    )(*args)


def _gate_norm_kernel(of_ref, ob_ref, g_ref, nw_ref, p_ref, o_ref):
    o = of_ref[...] + ob_ref[...]
    y = o * lax.rsqrt(_head_mean_sq(o, p_ref[...]) + EPS) * nw_ref[...]
    o_ref[...] = (y * _silu(g_ref[...])).astype(o_ref.dtype)


def _gate_norm_call(o_f, o_b, z, gcol, row0, nw):
    n = o_f.shape[0]
    row = lambda i: (i, 0)
    return pl.pallas_call(
        _gate_norm_kernel,
        grid=(n // TM,),
        in_specs=[pl.BlockSpec((TM, HW), row), pl.BlockSpec((TM, HW), row),
                  pl.BlockSpec((TM, HW), lambda i: (row0 // TM + i, gcol // HW)),
                  pl.BlockSpec((1, HW), lambda i: (0, 0)),
                  pl.BlockSpec((HW, HW), lambda i: (0, 0))],
        out_specs=pl.BlockSpec((TM, HW), row),
        out_shape=jax.ShapeDtypeStruct((n, HW), BF16),
        compiler_params=_cparams(("parallel",)),
        name="gate_norm",
    )(o_f, o_b, z, jnp.tile(nw, HW // HEAD_DIM).reshape(1, HW), _seg_matrix(HW))


DQKV_W = 3 * HW
HALO = 8


def _softplus(x):
    return jnp.maximum(x, 0.0) + jnp.log1p(jnp.exp(-jnp.abs(x)))


def _delta_prep_kernel(x_ref, xp_ref, xn_ref, cw_ref, dab_ref, ex_ref, al_ref, dtb_ref, p_ref, qkv_ref, gb_ref,
                       *, n_single, tiles_per_seq):
    i = pl.program_id(0)
    j = jnp.maximum(i - n_single, 0) % tiles_per_seq
    first = (i < n_single) | (j == 0)
    last = (i < n_single) | (j == tiles_per_seq - 1)
    x = x_ref[...]
    tm = x.shape[0]
    prev = jnp.where(first, 0.0, xp_ref[...])
    nxt = jnp.where(last, 0.0, xn_ref[...])
    xe = jnp.concatenate([prev, x, nxt], axis=0)
    y = None
    for tap in range(CONV_K):
        lo = HALO + tap - CONV_K // 2
        term = cw_ref[tap:tap + 1, :] * xe[lo:lo + tm]
        y = term if y is None else y + term
    y = _silu(y)
    qk = y[:, :2 * HW]
    ssq = jnp.dot(qk * qk, p_ref[...], preferred_element_type=F32, precision=HIGHEST)
    qkn = qk * lax.rsqrt(ssq + EPS)
    qkv_ref[...] = jnp.concatenate([qkn[:, :HW] * (HEAD_DIM ** -0.5), qkn[:, HW:], y[:, 2 * HW:]], axis=1)
    e = jnp.dot(dab_ref[...], ex_ref[...], preferred_element_type=F32, precision=HIGHEST)
    g = -jnp.exp(al_ref[...]) * _softplus(e[:, :2 * HW] + dtb_ref[...])
    gb_ref[...] = jnp.concatenate([g, jax.nn.sigmoid(e[:, 2 * HW:])], axis=1)


def _delta_prep_call(z, conv_w, a_log, dt_bias, n_single, tiles_per_seq):
    n = z.shape[0]
    cb = COL['dq'] // DQKV_W
    hb = TM // HALO
    nhb = n // HALO
    ex = np.zeros((LANES, 4 * HW), np.float32)
    for r in range(4 * D_HEADS):
        ex[r, r * HEAD_DIM:(r + 1) * HEAD_DIM] = 1.0
    cw = jnp.zeros((8, DQKV_W), F32).at[:CONV_K].set(conv_w.astype(F32))
    const = lambda i: (0, 0)
    return pl.pallas_call(
        functools.partial(_delta_prep_kernel, n_single=n_single, tiles_per_seq=tiles_per_seq),
        grid=(n // TM,),
        in_specs=[pl.BlockSpec((TM, DQKV_W), lambda i: (i, cb)),
                  pl.BlockSpec((HALO, DQKV_W), lambda i: (jnp.maximum(i * hb - 1, 0), cb)),
                  pl.BlockSpec((HALO, DQKV_W), lambda i: (jnp.minimum((i + 1) * hb, nhb - 1), cb)),
                  pl.BlockSpec((8, DQKV_W), const),
                  pl.BlockSpec((TM, LANES), lambda i: (i, COL['dab'] // LANES)),
                  pl.BlockSpec((LANES, 4 * HW), const),
                  pl.BlockSpec((1, 2 * HW), const), pl.BlockSpec((1, 2 * HW), const),
                  pl.BlockSpec((2 * HW, 2 * HW), const)],
        out_specs=[pl.BlockSpec((TM, DQKV_W), lambda i: (i, 0)), pl.BlockSpec((TM, 4 * HW), lambda i: (i, 0))],
        out_shape=[jax.ShapeDtypeStruct((n, DQKV_W), F32), jax.ShapeDtypeStruct((n, 4 * HW), F32)],
        compiler_params=_cparams(("parallel",)),
        name="delta_prep",
    )(z, z, z, cw, z, jnp.asarray(ex),
      jnp.repeat(a_log.astype(F32).reshape(-1), HEAD_DIM).reshape(1, 2 * HW),
      jnp.repeat(dt_bias.astype(F32).reshape(-1), HEAD_DIM).reshape(1, 2 * HW), _seg_matrix(2 * HW))


def _delta_consts():
    r = np.arange(HW)
    h = r // HEAD_DIM
    t = r % HEAD_DIM
    same_h = h[:, None] == h[None, :]
    s16 = same_h & (t[:, None] // 16 == t[None, :] // 16)
    s32 = same_h & (t[:, None] // 32 == t[None, :] // 32)
    mats = [same_h, r[:, None] == r[None, :], s16, s32 & ~s16, same_h & ~s32]
    return jnp.asarray(np.stack(mats).astype(np.float32))


def _tri_inv(abd, eye, m16, m32, m64):
    bf = lambda x: x.astype(BF16)
    a16 = bf(abd * m16)
    tin = eye - abd * m16
    p = _dot(a16, a16)
    tin = tin + _dot(bf(tin), bf(p))
    p = _dot(bf(p), bf(p))
    tin = tin + _dot(bf(tin), bf(p))
    p = _dot(bf(p), bf(p))
    tin = tin + _dot(bf(tin), bf(p))
    for m in (m32, m64):
        tb = bf(tin)
        tin = tin - _dot(tb, bf(_dot(bf(abd * m), tb)))
    return tin


def _delta_chunk(qkv, g, beta, st_ref, consts_ref, rev):
    mask = consts_ref[0]
    mask_bf = mask.astype(BF16)
    pt = _scan_rows(rev)
    ps = _scan_lanes(rev)
    q, k, v = qkv[:, :HW], qkv[:, HW:2 * HW], qkv[:, 2 * HW:]
    gc = _cumsum_scan(g, pt, rev)
    t_idx = lax.broadcasted_iota(jnp.int32, (CHUNK, HW), 0)
    s_idx = lax.broadcasted_iota(jnp.int32, (CHUNK, HW), 1) % CHUNK
    gcs = jnp.sum(jnp.where(t_idx == s_idx, gc, 0.0), axis=0, keepdims=True)
    gam = jnp.exp(jnp.minimum(gc - gcs, 0.0))
    g_last = _row(gc, CHUNK - 1, rev)
    egc = jnp.exp(gc)
    kb = k * beta
    r = _dot_t(jnp.concatenate([kb, q], axis=0).astype(BF16), _bd(k, mask_bf))
    a = jnp.where(ps < pt, r[:CHUNK] * gam, 0.0)
    aq = jnp.where(ps <= pt, r[CHUNK:] * gam, 0.0)
    abd = jnp.concatenate([a] * (HW // CHUNK), axis=0) * mask
    tin = _tri_inv(abd, consts_ref[1], consts_ref[2], consts_ref[3], consts_ref[4])
    rhs = jnp.concatenate([_bd(v * beta, mask_bf), _bd(kb * egc, mask_bf)], axis=1)
    uw_bd = _dot(tin.astype(BF16), rhs)
    uw = uw_bd[0:CHUNK] + uw_bd[CHUNK:2 * CHUNK] + uw_bd[2 * CHUNK:3 * CHUNK] + uw_bd[3 * CHUNK:]
    u, w = uw[:, :HW], uw[:, HW:]
    st = st_ref[...]
    ws_qs = _dot_t(jnp.concatenate([w, q * egc], axis=0).astype(BF16), st.astype(BF16))
    v_new = u - ws_qs[:CHUNK]
    o = ws_qs[CHUNK:] + _dot(aq.astype(BF16), _bd(v_new, mask_bf))
    ke = (k * jnp.exp(g_last - gc)).astype(BF16)
    upd = lax.dot_general(v_new.astype(BF16), ke, (((0,), (0,)), ((), ())), preferred_element_type=F32)
    st_ref[...] = jnp.exp(g_last) * st + mask * upd
    return o


def _delta_kernel(xf_ref, gf_ref, bf_ref, xb_ref, gb_ref, bb_ref, consts_ref, *rest, has_state):
    if has_state:
        s0_ref, of_ref, ob_ref, sout_ref, st_s = rest
    else:
        of_ref, ob_ref, sout_ref, st_s = rest
    c = pl.program_id(1)

    @pl.when(c == 0)
    def _():
        st_s[...] = s0_ref[...] if has_state else jnp.zeros(st_s.shape, F32)

    of_ref[...] = _delta_chunk(xf_ref[...], gf_ref[...], bf_ref[...], st_s.at[0], consts_ref, False)
    ob_ref[...] = _delta_chunk(xb_ref[...], gb_ref[...], bb_ref[...], st_s.at[1], consts_ref, True)

    @pl.when(c == pl.num_programs(1) - 1)
    def _():
        sout_ref[...] = st_s[...]


def _delta_call(qkv, gb, row0, nb, t, s0_bd):
    nc = t // CHUNK
    has_state = s0_bd is not None
    r0 = row0 // CHUNK
    fwd = lambda b, c: r0 + b * nc + c
    bwd = lambda b, c: r0 + b * nc + nc - 1 - c
    in_specs = [pl.BlockSpec((CHUNK, DQKV_W), lambda b, c: (fwd(b, c), 0)),
                pl.BlockSpec((CHUNK, HW), lambda b, c: (fwd(b, c), 0)),
                pl.BlockSpec((CHUNK, HW), lambda b, c: (fwd(b, c), 2)),
                pl.BlockSpec((CHUNK, DQKV_W), lambda b, c: (bwd(b, c), 0)),
                pl.BlockSpec((CHUNK, HW), lambda b, c: (bwd(b, c), 1)),
                pl.BlockSpec((CHUNK, HW), lambda b, c: (bwd(b, c), 3)),
                pl.BlockSpec((5, HW, HW), lambda b, c: (0, 0, 0))]
    args = [qkv, gb, gb, qkv, gb, gb, _delta_consts()]
    if has_state:
        in_specs.append(pl.BlockSpec((None, 2, HW, HW), lambda b, c: (b, 0, 0, 0)))
        args.append(s0_bd)
    return pl.pallas_call(
        functools.partial(_delta_kernel, has_state=has_state),
        grid=(nb, nc),
        in_specs=in_specs,
        out_specs=[pl.BlockSpec((CHUNK, HW), lambda b, c: (b * nc + c, 0)),
                   pl.BlockSpec((CHUNK, HW), lambda b, c: (b * nc + nc - 1 - c, 0)),
                   pl.BlockSpec((None, 2, HW, HW), lambda b, c: (b, 0, 0, 0))],
        out_shape=[jax.ShapeDtypeStruct((nb * t, HW), F32), jax.ShapeDtypeStruct((nb * t, HW), F32),
                   jax.ShapeDtypeStruct((nb, 2, HW, HW), F32)],
        scratch_shapes=[pltpu.VMEM((2, HW, HW), F32)],
        compiler_params=_cparams(("parallel", "arbitrary")),
        name="delta_scan",
    )(*args)


def kernel(x_prompt, x_sample, cache_a_k, cache_a_v, cache_b_k, cache_b_v, state_hgrn, state_delta, c, c_ctx, w_mod, b_mod, norm1, norm2, w_in, w_gate, a_qn, a_kn, b_qn, b_kn, b_rpb, c_lb, c_norm, d_conv, d_alog, d_dtbias, d_norm, w_branch, w_out, w_router, b_router, w_gu, b_gu, w_down, b_down):
    nbp, tp, d = x_prompt.shape
    nbs, ts, _ = x_sample.shape
    depth = w_mod.shape[0]
    past = cache_a_k.shape[2]
    n_p = nbp * tp
    n = n_p + nbs * ts
    assert tp == TM and ts % TM == 0 and n_p % ts == 0 and 1 + nbs <= 8 and n % TM_MM == 0
    npt, tps = n_p // TM, ts // TM

    def cond_of_tile(i):
        return jnp.where(i < npt, 0, 1 + (i - npt) // tps)

    x = jnp.concatenate([x_prompt.reshape(n_p, d), x_sample.reshape(nbs * ts, d)], axis=0)
    conds = jnp.zeros((8, d), F32).at[0].set(c_ctx).at[1:1 + nbs].set(c)
    mod = _mod_call(conds, w_mod, b_mod)

    pl_ = jax.nn.softmax(c_lb.astype(F32), axis=0)
    lbs = jnp.cumsum(pl_, axis=0) - pl_[0:1]
    dqkv0 = COL['cg'] + HW
    da0 = dqkv0 + DQKV_W
    dg0 = da0 + 4 * D_HEADS
    w_in_p = jnp.concatenate([w_in[:, :, :dqkv0], w_in[:, :, dg0:dg0 + HW], w_in[:, :, dqkv0:da0],
                              w_in[:, :, da0:dg0],
                              jnp.zeros((depth, d, IN_W_PAD - dg0 - HW), w_in.dtype)], axis=2).astype(BF16)
    w_gate_b = w_gate.astype(BF16)
    w_branch_b = w_branch.astype(BF16)
    w_out_b = w_out.astype(BF16)
    w_router_p = jnp.zeros((depth, d, LANES), F32).at[:, :, :N_EXPERTS].set(w_router)
    b_router_p = jnp.full((depth, 1, LANES), NEG, F32).at[:, 0, :N_EXPERTS].set(b_router)

    tt = jnp.arange(ts, dtype=jnp.int32)
    half = HEAD_DIM // 2
    inv_freq = 1.0 / (ROPE_THETA ** (jnp.arange(0, half, 2, dtype=F32) / half))
    ang_r = (tt // GRID_W).astype(F32)[:, None] * inv_freq[None, :]
    ang_c = (tt % GRID_W).astype(F32)[:, None] * inv_freq[None, :]
    ang = jnp.concatenate([ang_r, ang_r, ang_c, ang_c], axis=-1)
    sign = jnp.asarray(np.where((np.arange(HEAD_DIM) % 32) < 16, -1.0, 1.0).astype(np.float32))
    cos_all = jnp.concatenate([jnp.ones((n_p, HEAD_DIM), F32), jnp.tile(jnp.cos(ang), (nbs, 1))], axis=0)
    sin_all = jnp.concatenate([jnp.zeros((n_p, HEAD_DIM), F32), jnp.tile(jnp.sin(ang) * sign, (nbs, 1))], axis=0)
    rope_tabs = (jnp.tile(cos_all, (1, 2)), jnp.tile(sin_all, (1, 2)))
    scale = HEAD_DIM ** -0.5
    wa, wb = A_KV_HEADS * HEAD_DIM, B_HEADS * HEAD_DIM

    ak_l, av_l, bk_l, bv_l, sc_l, sd_l = [], [], [], [], [], []
    for l in range(depth):
        mod3 = mod[l].reshape(8, 1, N_MOD * d)
        h = _norm_call(x, norm1[l].reshape(1, d), mod3, cond_of_tile)
        z = _mm_call(h, w_in_p[l])

        (qa,) = _prep_call(z, COL['aq'], A_HEADS * HEAD_DIM, a_qn[l], scale, rope_tabs, False)
        ka_n, ka = _prep_call(z, COL['ak'], wa, a_kn[l], 1.0, rope_tabs, True)
        oa_p = _attn_call(qa, ka, z, COL['av'], 0, nbp, tp, A_HEADS, A_KV_HEADS)
        oa_s = _attn_call(qa, ka, z, COL['av'], n_p, nbs, ts, A_HEADS, A_KV_HEADS,
                          ctx=(cache_a_k[:, l].reshape(nbs, past, wa).astype(BF16),
                               cache_a_v[:, l].reshape(nbs, past, wa).astype(BF16)))
        oa = jnp.concatenate([oa_p, oa_s], axis=0)

        (qb,) = _prep_call(z, COL['bq'], wb, b_qn[l], scale, None, False)
        kb_n, kb = _prep_call(z, COL['bk'], wb, b_kn[l], 1.0, None, True)
        ob_p = _attn_call(qb, kb, z, COL['bv'], 0, nbp, tp, B_HEADS, B_HEADS)
        ob_s = _nbr_call(qb, kb, z, COL['bv'], n_p, nbs, ts,
                         cache_b_k[:, l].reshape(nbs, past, wb).astype(BF16),
                         cache_b_v[:, l].reshape(nbs, past, wb).astype(BF16), _nbr_bias_table(b_rpb[l]))
        ob = jnp.concatenate([ob_p, ob_s], axis=0)

        cf_p, cb_p, sc_p = _hgrn_call(z, 0, nbp, tp, lbs[l], None)
        cf_s, cb_s, _ = _hgrn_call(z, n_p, nbs, ts, lbs[l], _state_to_bd(state_hgrn[:, l]))
        oc = _gate_norm_call(jnp.concatenate([cf_p, cf_s], axis=0), jnp.concatenate([cb_p, cb_s], axis=0),
                             z, COL['cg'], 0, c_norm[l])

        qkv, gb = _delta_prep_call(z, d_conv[l], d_alog[l], d_dtbias[l], npt, tps)
        df_p, db_p, sd_p = _delta_call(qkv, gb, 0, nbp, tp, None)
        df_s, db_s, _ = _delta_call(qkv, gb, n_p, nbs, ts, _state_to_bd(state_delta[:, l]))
        od = _gate_norm_call(jnp.concatenate([df_p, df_s], axis=0), jnp.concatenate([db_p, db_s], axis=0),
                             z, COL['dg'], 0, d_norm[l])

        x = _merge_call(x, h, oa, ob, oc, od, w_gate_b[l], w_branch_b[l], w_out_b[l], mod3, cond_of_tile)

        h3, gates, sel, counts = _router_call(x, norm2[l].reshape(1, d), mod3, cond_of_tile,
                                              w_router_p[l], b_router_p[l])
        dest, block_e, n_used, n_blocks = _slot_plan(sel, counts, n)
        x_slots = _dispatch_call(dest, h3, n_blocks * MOE_TM)
        y_slots = _gmm_call(block_e, n_used, x_slots, w_gu[l], b_gu[l], w_down[l], b_down[l])
        x = _combine_call(dest, x, gates, mod3, cond_of_tile, y_slots)

        ak_l.append(ka_n[:n_p].reshape(nbp, tp, A_KV_HEADS, HEAD_DIM))
        av_l.append(z[:n_p, COL['av']:COL['av'] + wa].reshape(nbp, tp, A_KV_HEADS, HEAD_DIM))
        bk_l.append(kb_n[:n_p].reshape(nbp, tp, B_HEADS, HEAD_DIM))
        bv_l.append(z[:n_p, COL['bv']:COL['bv'] + wb].reshape(nbp, tp, B_HEADS, HEAD_DIM))
        sc_l.append(_state_from_bd(sc_p))
        sd_l.append(_state_from_bd(sd_p))

    return (x[:n_p].reshape(nbp, tp, d), x[n_p:].reshape(nbs, ts, d),
            jnp.stack(ak_l, axis=1), jnp.stack(av_l, axis=1), jnp.stack(bk_l, axis=1), jnp.stack(bv_l, axis=1),
            jnp.stack(sc_l, axis=1), jnp.stack(sd_l, axis=1))
```

```python
import functools

import numpy as np
import jax
import jax.numpy as jnp
from jax import lax
from jax.experimental import pallas as pl
from jax.experimental.pallas import tpu as pltpu

F32 = jnp.float32
BF16 = jnp.bfloat16
HIGHEST = lax.Precision.HIGHEST

GRID_W = 64
HEAD_DIM = 64
A_HEADS = 8
A_KV_HEADS = 2
B_HEADS = 4
NB_ROWS = 8
NB_COLS = 16
C_HEADS = 4
D_HEADS = 4
CONV_K = 5
CHUNK = 64
N_EXPERTS = 32
TOP_K = 4
SWIGLU_LIMIT = 7.0
SWIGLU_ALPHA = 1.702
ROPE_THETA = 10000.0
EPS = 1e-6
MAX_INPUT_GATE = 1.0 - 1e-6
N_MOD = 6
HW = 256

LANES = 128
VMEM_LIMIT = 56 * 1024 * 1024
NEG = -1e30

TM = 256
TM_MM = 512
TN_MM = 1024
TQ = 128
MOE_TM = 256

COL = dict(aq=0, ak=512, av=640, bq=768, bk=1024, bv=1280, cq=1536, ci=1792, cff=2048, cfb=2304,
           cg=2560, dg=2816, dq=3072, dk=3328, dv=3584, dab=3840)
IN_W_PAD = 4096


def _cparams(sem, vmem=VMEM_LIMIT):
    return pltpu.CompilerParams(dimension_semantics=sem, vmem_limit_bytes=vmem)


def _silu(x):
    return x * jax.nn.sigmoid(x)


def _dot(a, b):
    return jnp.dot(a, b, preferred_element_type=F32)


def _dot_t(a, b):
    return lax.dot_general(a, b, (((1,), (1,)), ((), ())), preferred_element_type=F32)


def _mod_kernel(c_ref, w_ref, b_ref, o_ref):
    a = _silu(c_ref[...]).astype(BF16)
    o_ref[...] = _dot(a, w_ref[...].astype(BF16)) + b_ref[...]


def _mod_call(conds, w_mod, b_mod):
    depth, d, n = w_mod.shape
    tn = 1024
    return pl.pallas_call(
        _mod_kernel,
        grid=(depth, n // tn),
        in_specs=[pl.BlockSpec((8, d), lambda l, j: (0, 0)),
                  pl.BlockSpec((None, d, tn), lambda l, j: (l, 0, j)),
                  pl.BlockSpec((None, 1, tn), lambda l, j: (l, 0, j))],
        out_specs=pl.BlockSpec((None, 8, tn), lambda l, j: (l, 0, j)),
        out_shape=jax.ShapeDtypeStruct((depth, 8, n), F32),
        compiler_params=_cparams(("parallel", "parallel")),
        name="mod",
    )(conds, w_mod, b_mod.reshape(depth, 1, n))


def _mod_norm(x, g, m, k_shift, k_scale, d):
    y = x * lax.rsqrt(jnp.mean(x * x, axis=-1, keepdims=True) + EPS) * g
    return y * (1.0 + m[:, k_scale * d:(k_scale + 1) * d]) + m[:, k_shift * d:(k_shift + 1) * d]


def _norm_kernel(x_ref, g_ref, mod_ref, h_ref, *, d):
    h_ref[...] = _mod_norm(x_ref[...], g_ref[...], mod_ref[...], 0, 1, d).astype(BF16)


def _norm_call(x, g, mod3, cond_of_tile):
    n, d = x.shape
    return pl.pallas_call(
        functools.partial(_norm_kernel, d=d),
        grid=(n // TM,),
        in_specs=[pl.BlockSpec((TM, d), lambda i: (i, 0)),
                  pl.BlockSpec((1, d), lambda i: (0, 0)),
                  pl.BlockSpec((None, 1, N_MOD * d), lambda i: (cond_of_tile(i), 0, 0))],
        out_specs=pl.BlockSpec((TM, d), lambda i: (i, 0)),
        out_shape=jax.ShapeDtypeStruct((n, d), BF16),
        compiler_params=_cparams(("parallel",)),
        name="norm1",
    )(x, g, mod3)


def _mm_kernel(a_ref, w_ref, o_ref):
    o_ref[...] = _dot(a_ref[...], w_ref[...]).astype(o_ref.dtype)


def _mm_call(a, w, l, out_dtype=F32):
    m, k = a.shape
    n = w.shape[2]
    return pl.pallas_call(
        _mm_kernel,
        grid=(n // TN_MM, m // TM_MM),
        in_specs=[pl.BlockSpec((TM_MM, k), lambda j, i: (i, 0)),
                  pl.BlockSpec((None, k, TN_MM), lambda j, i: (l, 0, j))],
        out_specs=pl.BlockSpec((TM_MM, TN_MM), lambda j, i: (i, j)),
        out_shape=jax.ShapeDtypeStruct((m, n), out_dtype),
        compiler_params=_cparams(("parallel", "parallel")),
        name="in_proj",
    )(a, w)


def _merge_kernel(x_ref, h_ref, oa_ref, ob_ref, oc_ref, od_ref, wg_ref, wb_ref, wo_ref, mod_ref, o_ref, *, d):
    h = h_ref[...]
    outs = (oa_ref, ob_ref, oc_ref, od_ref)
    merged = None
    off = 0
    for i in range(4):
        wdt = outs[i].shape[1]
        gate = jax.nn.sigmoid(_dot(h, wg_ref[:, i * d:(i + 1) * d]))
        br = gate * _dot(outs[i][...], wb_ref[off:off + wdt, :])
        merged = br if merged is None else merged + br
        off += wdt
    y = _dot(merged.astype(BF16), wo_ref[...])
    o_ref[...] = x_ref[...] + mod_ref[...][:, 2 * d:3 * d] * y


def _merge_call(x, h, oa, ob, oc, od, w_gate, w_branch, w_out, l, mod3, cond_of_tile):
    n, d = x.shape
    row = lambda i: (i, 0)
    const = lambda i: (l, 0, 0)
    once = pl.Buffered(1)
    return pl.pallas_call(
        functools.partial(_merge_kernel, d=d),
        grid=(n // TM,),
        in_specs=[pl.BlockSpec((TM, d), row), pl.BlockSpec((TM, d), row),
                  pl.BlockSpec((TM, oa.shape[1]), row), pl.BlockSpec((TM, ob.shape[1]), row),
                  pl.BlockSpec((TM, oc.shape[1]), row), pl.BlockSpec((TM, od.shape[1]), row),
                  pl.BlockSpec((None,) + w_gate.shape[1:], const, pipeline_mode=once),
                  pl.BlockSpec((None,) + w_branch.shape[1:], const, pipeline_mode=once),
                  pl.BlockSpec((None,) + w_out.shape[1:], const, pipeline_mode=once),
                  pl.BlockSpec((None, 1, N_MOD * d), lambda i: (cond_of_tile(i), 0, 0))],
        out_specs=pl.BlockSpec((TM, d), row),
        out_shape=jax.ShapeDtypeStruct((n, d), F32),
        compiler_params=_cparams(("parallel",)),
        name="merge",
    )(x, h, oa, ob, oc, od, w_gate, w_branch, w_out, mod3)


def _router_kernel(x_ref, g_ref, mod_ref, wr_ref, br_ref, ltri_ref, h_ref, gate_ref, sel_ref, cnt_ref, run_s, *, d):
    i = pl.program_id(0)

    @pl.when(i == 0)
    def _():
        run_s[...] = jnp.zeros(run_s.shape, F32)

    h2 = _mod_norm(x_ref[...], g_ref[...], mod_ref[...], 3, 4, d)
    nsub = d // LANES
    for sub in range(nsub):
        h_ref[pl.ds(sub, x_ref.shape[0], stride=nsub), :] = h2[:, sub * LANES:(sub + 1) * LANES]
    logits = jnp.dot(h2, wr_ref[...], preferred_element_type=F32, precision=HIGHEST) + br_ref[...]
    lane = lax.broadcasted_iota(jnp.int32, logits.shape, 1)
    vals, idxs = [], []
    cur = logits
    for _ in range(TOP_K):
        m = jnp.max(cur, axis=-1, keepdims=True)
        ix = jnp.min(jnp.where(cur == m, lane, LANES), axis=-1, keepdims=True)
        vals.append(m)
        idxs.append(ix)
        cur = jnp.where(lane == ix, -jnp.inf, cur)
    es = [jnp.exp(v - vals[0]) for v in vals]
    tot = es[0] + es[1] + es[2] + es[3]
    gates = jnp.zeros(logits.shape, F32)
    sel = jnp.zeros(logits.shape, jnp.int32)
    base = run_s[...]
    ltri = ltri_ref[...]
    for k in range(TOP_K):
        onehot = jnp.where(lane == idxs[k], 1.0, 0.0)
        before = base + _dot(ltri, onehot.astype(BF16))
        rank = jnp.sum(onehot * before, axis=-1, keepdims=True).astype(jnp.int32)
        base = base + jnp.sum(onehot, axis=0, keepdims=True)
        gates = jnp.where(lane == k, es[k] / tot, gates)
        sel = jnp.where(lane == k, idxs[k], sel)
        sel = jnp.where(lane == TOP_K + k, rank, sel)
    run_s[...] = base
    gate_ref[...] = gates
    sel_ref[...] = sel
    cnt_ref[...] = base


def _router_call(x, g, mod3, cond_of_tile, w_router_p, b_router_p, l):
    n, d = x.shape
    row = lambda i: (i, 0)
    const = lambda i: (0, 0)
    layer = lambda i: (l, 0, 0)
    ltri = jnp.asarray(np.tril(np.ones((TM, TM), np.float32), -1)).astype(BF16)
    return pl.pallas_call(
        functools.partial(_router_kernel, d=d),
        grid=(n // TM,),
        in_specs=[pl.BlockSpec((TM, d), row),
                  pl.BlockSpec((1, d), const),
                  pl.BlockSpec((None, 1, N_MOD * d), lambda i: (cond_of_tile(i), 0, 0)),
                  pl.BlockSpec((None, d, LANES), layer),
                  pl.BlockSpec((None, 1, LANES), layer),
                  pl.BlockSpec((TM, TM), const)],
        out_specs=[pl.BlockSpec((TM * (d // LANES), LANES), row), pl.BlockSpec((TM, LANES), row),
                   pl.BlockSpec((TM, LANES), row), pl.BlockSpec((1, LANES), const)],
        out_shape=[jax.ShapeDtypeStruct((n * (d // LANES), LANES), F32), jax.ShapeDtypeStruct((n, LANES), F32),
                   jax.ShapeDtypeStruct((n, LANES), jnp.int32), jax.ShapeDtypeStruct((1, LANES), F32)],
        scratch_shapes=[pltpu.VMEM((1, LANES), F32)],
        compiler_params=_cparams(("arbitrary",)),
        name="router",
    )(x, g, mod3, w_router_p, b_router_p, ltri)


def _slot_plan(sel, counts, n_tok):
    cnt = counts[0, :N_EXPERTS].astype(jnp.int32)
    padded = (cnt + MOE_TM - 1) // MOE_TM * MOE_TM
    pad_end = jnp.cumsum(padded)
    pad_start = pad_end - padded
    dest = (pad_start[sel[:, :TOP_K]] + sel[:, TOP_K:2 * TOP_K]).reshape(n_tok * TOP_K)
    n_blocks = n_tok * TOP_K // MOE_TM + N_EXPERTS
    starts = jnp.arange(n_blocks, dtype=jnp.int32) * MOE_TM
    block_e = jnp.minimum(jnp.sum((pad_end[None, :] <= starts[:, None]).astype(jnp.int32), axis=1), N_EXPERTS - 1)
    n_used = (pad_end[-1:] // MOE_TM).astype(jnp.int32)
    return dest, block_e, n_used, n_blocks


ROW_TILE = 8


def _row_slice(ref, r):
    return ref.at[pl.ds(pl.multiple_of(r * ROW_TILE, ROW_TILE), ROW_TILE)]


def _dispatch_kernel(dest_ref, h_ref, zero_ref, o_ref, sem):
    del zero_ref
    rows = h_ref.shape[0]
    tm = rows // ROW_TILE
    base = pl.program_id(0) * (tm * TOP_K)

    def body(t, carry):
        for k in range(TOP_K):
            pltpu.make_async_copy(_row_slice(h_ref, t), _row_slice(o_ref, dest_ref[base + t * TOP_K + k]),
                                  sem).start()
        return carry

    lax.fori_loop(0, tm, body, 0)
    for _ in range(TOP_K):
        pltpu.make_async_copy(h_ref, o_ref.at[pl.ds(0, rows)], sem).wait()


def _dispatch_call(dest, h_rows, n_slots):
    n = h_rows.shape[0] // ROW_TILE
    grid_spec = pltpu.PrefetchScalarGridSpec(
        num_scalar_prefetch=1,
        grid=(n // TM,),
        in_specs=[pl.BlockSpec((TM * ROW_TILE, LANES), lambda i, dst: (i, 0)),
                  pl.BlockSpec(memory_space=pl.ANY)],
        out_specs=pl.BlockSpec(memory_space=pl.ANY),
        scratch_shapes=[pltpu.SemaphoreType.DMA(())],
    )
    zeros = jnp.zeros((n_slots * ROW_TILE, LANES), F32)
    return pl.pallas_call(
        _dispatch_kernel,
        grid_spec=grid_spec,
        out_shape=jax.ShapeDtypeStruct((n_slots * ROW_TILE, LANES), F32),
        input_output_aliases={2: 0},
        compiler_params=_cparams(("arbitrary",)),
        name="dispatch",
    )(dest, h_rows, zeros)


def _gmm_kernel(be_ref, nu_ref, x_ref, wgu_ref, bgu_ref, wd_ref, bd_ref, y_ref, wgu_s, wd_s, *, f):
    i = pl.program_id(0)
    used = i < nu_ref[0]
    changed = (i == 0) | (be_ref[i] != be_ref[jnp.maximum(i - 1, 0)])
    tm = x_ref.shape[0] // ROW_TILE

    @pl.when(used & changed)
    def _():
        wgu_s[...] = wgu_ref[...].astype(BF16)
        wd_s[...] = wd_ref[...].astype(BF16)

    @pl.when(used)
    def _():
        x = jnp.concatenate([x_ref[pl.ds(sub, tm, stride=ROW_TILE), :] for sub in range(ROW_TILE)], axis=1)
        gu = _dot(x.astype(BF16), wgu_s[...]) + bgu_ref[...]
        gate = jnp.minimum(gu[:, :f], SWIGLU_LIMIT)
        up = jnp.clip(gu[:, f:], -SWIGLU_LIMIT, SWIGLU_LIMIT)
        act = gate * jax.nn.sigmoid(SWIGLU_ALPHA * gate) * (up + 1.0)
        y = _dot(act.astype(BF16), wd_s[...]) + bd_ref[...]
        for sub in range(ROW_TILE):
            y_ref[pl.ds(sub, tm, stride=ROW_TILE), :] = y[:, sub * LANES:(sub + 1) * LANES]

    @pl.when(jnp.logical_not(used))
    def _():
        y_ref[...] = jnp.zeros(y_ref.shape, y_ref.dtype)


def _gmm_call(block_e, n_used, x_slots, w_gu, b_gu, w_down, b_down, l):
    n_slots = x_slots.shape[0] // ROW_TILE
    _, e, d, f2 = w_gu.shape
    f = f2 // 2
    n_blocks = n_slots // MOE_TM
    xrow = lambda i, be, nu: (i, 0)
    wsel = lambda i, be, nu: (l, be[i], 0, 0)
    grid_spec = pltpu.PrefetchScalarGridSpec(
        num_scalar_prefetch=2,
        grid=(n_blocks,),
        in_specs=[pl.BlockSpec((MOE_TM * ROW_TILE, LANES), xrow),
                  pl.BlockSpec((None, None, d, f2), wsel),
                  pl.BlockSpec((None, None, 1, f2), wsel),
                  pl.BlockSpec((None, None, f, d), wsel),
                  pl.BlockSpec((None, None, 1, d), wsel)],
        out_specs=pl.BlockSpec((MOE_TM * ROW_TILE, LANES), xrow),
        scratch_shapes=[pltpu.VMEM((d, f2), BF16), pltpu.VMEM((f, d), BF16)],
    )
    depth = w_gu.shape[0]
    return pl.pallas_call(
        functools.partial(_gmm_kernel, f=f),
        grid_spec=grid_spec,
        out_shape=jax.ShapeDtypeStruct((n_slots * ROW_TILE, LANES), F32),
        compiler_params=_cparams(("arbitrary",)),
        name="experts",
    )(block_e, n_used, x_slots, w_gu, b_gu.reshape(depth, e, 1, f2), w_down, b_down.reshape(depth, e, 1, d))


def _combine_kernel(dest_ref, x_ref, gate_ref, mod_ref, y_ref, o_ref, buf, sems, *, d):
    tm = x_ref.shape[0]
    rows = tm * ROW_TILE
    i = pl.program_id(0)
    n_steps = pl.num_programs(0)

    def issue(step, slot):
        base = step * (tm * TOP_K)

        def body(t, carry):
            for k in range(TOP_K):
                dst = buf.at[pl.ds(pl.multiple_of((slot * TOP_K + k) * rows + t * ROW_TILE, ROW_TILE), ROW_TILE)]
                pltpu.make_async_copy(_row_slice(y_ref, dest_ref[base + t * TOP_K + k]), dst, sems.at[slot]).start()
            return carry

        lax.fori_loop(0, tm, body, 0)

    @pl.when(i == 0)
    def _():
        issue(0, 0)

    @pl.when(i + 1 < n_steps)
    def _():
        issue(i + 1, (i + 1) % 2)

    slot = i % 2
    for k in range(TOP_K):
        off = pl.multiple_of((slot * TOP_K + k) * rows, ROW_TILE)
        pltpu.make_async_copy(y_ref.at[pl.ds(0, rows)], buf.at[pl.ds(off, rows)], sems.at[slot]).wait()
    g = gate_ref[...]
    gk = [jnp.broadcast_to(g[:, k:k + 1], (tm, LANES)) for k in range(TOP_K)]
    g2 = mod_ref[...][:, 5 * d:6 * d]
    for sub in range(ROW_TILE):
        sl = slice(sub * LANES, (sub + 1) * LANES)
        acc = None
        for k in range(TOP_K):
            off = pl.multiple_of((slot * TOP_K + k) * rows, ROW_TILE)
            term = gk[k] * buf[pl.ds(off + sub, tm, stride=ROW_TILE), :]
            acc = term if acc is None else acc + term
        o_ref[:, sl] = x_ref[:, sl] + g2[:, sl] * acc


def _combine_call(dest, x, gates, mod3, cond_of_tile, y_slots):
    n, d = x.shape
    row = lambda i, dst: (i, 0)
    grid_spec = pltpu.PrefetchScalarGridSpec(
        num_scalar_prefetch=1,
        grid=(n // TM,),
        in_specs=[pl.BlockSpec((TM, d), row),
                  pl.BlockSpec((TM, LANES), row),
                  pl.BlockSpec((None, 1, N_MOD * d), lambda i, dst: (cond_of_tile(i), 0, 0)),
                  pl.BlockSpec(memory_space=pl.ANY)],
        out_specs=pl.BlockSpec((TM, d), row),
        scratch_shapes=[pltpu.VMEM((2 * TOP_K * TM * ROW_TILE, LANES), F32), pltpu.SemaphoreType.DMA((2,))],
    )
    return pl.pallas_call(
        functools.partial(_combine_kernel, d=d),
        grid_spec=grid_spec,
        out_shape=jax.ShapeDtypeStruct((n, d), F32),
        compiler_params=_cparams(("arbitrary",)),
        name="combine",
    )(dest, x, gates, mod3, y_slots)


def _seg_matrix(w):
    i = np.arange(w) // HEAD_DIM
    return jnp.asarray((i[:, None] == i[None, :]).astype(np.float32))


def _head_mean_sq(x, p):
    return jnp.dot(x * x, p, preferred_element_type=F32, precision=HIGHEST) * (1.0 / HEAD_DIM)


def _prep_kernel(z_ref, nw_ref, p_ref, *rest, scale, rope, want_norm):
    if rope:
        cos_ref, sin_ref = rest[0], rest[1]
        rest = rest[2:]
    x = z_ref[...]
    w = x.shape[1]
    y = x * lax.rsqrt(_head_mean_sq(x, p_ref[...]) + EPS) * nw_ref[...]
    if want_norm:
        rest[0][...] = y
    out_ref = rest[-1]
    if rope:
        reps = w // LANES
        cos = jnp.concatenate([cos_ref[...]] * reps, axis=1) if reps > 1 else cos_ref[...]
        sin = jnp.concatenate([sin_ref[...]] * reps, axis=1) if reps > 1 else sin_ref[...]
        lane = lax.broadcasted_iota(jnp.int32, y.shape, 1)
        first = (lane % 32) < 16
        rot = jnp.where(first, pltpu.roll(y, w - 16, axis=1), pltpu.roll(y, 16, axis=1))
        y = y * cos + rot * sin
    out_ref[...] = (y * scale).astype(BF16)


def _prep_call(z, col, w, nw, scale, rope_tabs, want_norm):
    n = z.shape[0]
    cb = col // w
    row = lambda i: (i, 0)
    in_specs = [pl.BlockSpec((TM, w), lambda i: (i, cb)),
                pl.BlockSpec((1, w), lambda i: (0, 0)),
                pl.BlockSpec((w, w), lambda i: (0, 0))]
    args = [z, jnp.tile(nw, w // HEAD_DIM).reshape(1, w), _seg_matrix(w)]
    if rope_tabs is not None:
        in_specs += [pl.BlockSpec((TM, LANES), row), pl.BlockSpec((TM, LANES), row)]
        args += list(rope_tabs)
    out_specs, out_shape = [], []
    if want_norm:
        out_specs.append(pl.BlockSpec((TM, w), row))
        out_shape.append(jax.ShapeDtypeStruct((n, w), F32))
    out_specs.append(pl.BlockSpec((TM, w), row))
    out_shape.append(jax.ShapeDtypeStruct((n, w), BF16))
    return pl.pallas_call(
        functools.partial(_prep_kernel, scale=scale, rope=rope_tabs is not None, want_norm=want_norm),
        grid=(n // TM,),
        in_specs=in_specs, out_specs=out_specs, out_shape=out_shape,
        compiler_params=_cparams(("parallel",)),
        name="qk_prep",
    )(*args)


def _attn_kernel(q_ref, k_ref, v_ref, *rest, hq, hk, has_ctx):
    if has_ctx:
        ck_ref, cv_ref, o_ref = rest
    else:
        (o_ref,) = rest
    g = hq // hk
    tq = q_ref.shape[0]
    dh = HEAD_DIM
    outs = []
    for j in range(hk):
        q4 = jnp.concatenate([q_ref[:, (j * g + gi) * dh:(j * g + gi + 1) * dh] for gi in range(g)], axis=0)
        s = _dot_t(q4, k_ref[:, j * dh:(j + 1) * dh])
        m = jnp.max(s, axis=-1, keepdims=True)
        if has_ctx:
            sc = _dot_t(q4, ck_ref[:, j * dh:(j + 1) * dh])
            m = jnp.maximum(m, jnp.max(sc, axis=-1, keepdims=True))
        p = jnp.exp(s - m)
        l = jnp.sum(p, axis=-1, keepdims=True)
        o = _dot(p.astype(BF16), v_ref[:, j * dh:(j + 1) * dh].astype(BF16))
        if has_ctx:
            pc = jnp.exp(sc - m)
            l = l + jnp.sum(pc, axis=-1, keepdims=True)
            o = o + _dot(pc.astype(BF16), cv_ref[:, j * dh:(j + 1) * dh])
        o = o / l
        outs += [o[gi * tq:(gi + 1) * tq] for gi in range(g)]
    o_ref[...] = jnp.concatenate(outs, axis=1).astype(o_ref.dtype)


def _attn_call(q, k, v, vcol, row0, nb, t, hq, hk, ctx=None):
    tq = min(TQ, t)
    wq, wk = hq * HEAD_DIM, hk * HEAD_DIM
    qb0, kb0, vb = row0 // tq, row0 // t, vcol // wk
    nq = t // tq
    in_specs = [pl.BlockSpec((tq, wq), lambda b, i: (qb0 + b * nq + i, 0)),
                pl.BlockSpec((t, wk), lambda b, i: (kb0 + b, 0)),
                pl.BlockSpec((t, wk), lambda b, i: (kb0 + b, vb))]
    args = [q, k, v]
    if ctx is not None:
        p = ctx[0].shape[1]
        in_specs += [pl.BlockSpec((None, p, wk), lambda b, i: (b, 0, 0))] * 2
        args += list(ctx)
    return pl.pallas_call(
        functools.partial(_attn_kernel, hq=hq, hk=hk, has_ctx=ctx is not None),
        grid=(nb, nq),
        in_specs=in_specs,
        out_specs=pl.BlockSpec((tq, wq), lambda b, i: (b * nq + i, 0)),
        out_shape=jax.ShapeDtypeStruct((nb * t, wq), BF16),
        compiler_params=_cparams(("parallel", "parallel")),
        name="attn",
    )(*args)


def _nbr_bias_table(rpb):
    o = np.arange(NB_ROWS)[:, None, None, None]
    cc = np.arange(GRID_W)[None, :, None, None]
    i = np.arange(NB_ROWS)[None, None, :, None]
    j = np.arange(GRID_W)[None, None, None, :]
    col_start = np.clip(cc - NB_COLS // 2, 0, GRID_W - NB_COLS)
    valid = (j >= col_start) & (j < col_start + NB_COLS)
    sel_r = ((i - o + NB_ROWS - 1)[..., None] == np.arange(2 * NB_ROWS - 1)).astype(np.float32)[:, 0, :, 0]
    sel_c = (((j - cc + NB_COLS - 1)[..., None] == np.arange(2 * NB_COLS - 1)) & valid[..., None])
    sel_c = sel_c.astype(np.float32)[0, :, 0]
    tab = jnp.einsum('hrd,oir,cjd->hocij', rpb.astype(F32), jnp.asarray(sel_r), jnp.asarray(sel_c),
                     precision=HIGHEST)
    tab = tab + jnp.asarray(np.where(valid, 0.0, NEG).astype(np.float32))[None, :, :, :, :]
    return tab.reshape(rpb.shape[0], NB_ROWS, GRID_W, NB_ROWS * GRID_W)


def _nbr_kernel(q_ref, k_ref, v_ref, ck_ref, cv_ref, bias_ref, o_ref, *, rows):
    r = pl.program_id(1)
    rs = jnp.clip(r - NB_ROWS // 2, 0, rows - NB_ROWS)
    start = pl.multiple_of(rs * GRID_W, GRID_W)
    kw = k_ref[pl.ds(start, NB_ROWS * GRID_W), :]
    vw = v_ref[pl.ds(start, NB_ROWS * GRID_W), :].astype(BF16)
    dh = HEAD_DIM
    outs = []
    for h in range(B_HEADS):
        sl = slice(h * dh, (h + 1) * dh)
        qh = q_ref[:, sl]
        s_loc = _dot_t(qh, kw[:, sl]) + bias_ref[h]
        s_ctx = _dot_t(qh, ck_ref[:, sl])
        m = jnp.maximum(jnp.max(s_loc, axis=-1, keepdims=True), jnp.max(s_ctx, axis=-1, keepdims=True))
        p_loc = jnp.exp(s_loc - m)
        p_ctx = jnp.exp(s_ctx - m)
        l = jnp.sum(p_loc, axis=-1, keepdims=True) + jnp.sum(p_ctx, axis=-1, keepdims=True)
        o = _dot(p_loc.astype(BF16), vw[:, sl]) + _dot(p_ctx.astype(BF16), cv_ref[:, sl])
        outs.append(o / l)
    o_ref[...] = jnp.concatenate(outs, axis=1).astype(o_ref.dtype)


def _nbr_call(q, k, z, vcol, row0, nb, t, ck, cv, bias):
    rows = t // GRID_W
    assert rows >= NB_ROWS
    w = B_HEADS * HEAD_DIM
    p = ck.shape[1]
    nwin = NB_ROWS * GRID_W

    def bias_map(b, r):
        return (0, r - jnp.clip(r - NB_ROWS // 2, 0, rows - NB_ROWS), 0, 0)

    return pl.pallas_call(
        functools.partial(_nbr_kernel, rows=rows),
        grid=(nb, rows),
        in_specs=[pl.BlockSpec((GRID_W, w), lambda b, r: (row0 // GRID_W + b * rows + r, 0)),
                  pl.BlockSpec((t, w), lambda b, r: (row0 // t + b, 0)),
                  pl.BlockSpec((t, w), lambda b, r: (row0 // t + b, vcol // w)),
                  pl.BlockSpec((None, p, w), lambda b, r: (b, 0, 0)),
                  pl.BlockSpec((None, p, w), lambda b, r: (b, 0, 0)),
                  pl.BlockSpec((B_HEADS, None, GRID_W, nwin), bias_map)],
        out_specs=pl.BlockSpec((GRID_W, w), lambda b, r: (b * rows + r, 0)),
        out_shape=jax.ShapeDtypeStruct((nb * t, w), BF16),
        compiler_params=_cparams(("parallel", "arbitrary")),
        name="nbr_attn",
    )(q, k, z, ck, cv, bias)


def _head_mask():
    i = np.arange(HW) // HEAD_DIM
    return jnp.asarray((i[:, None] == i[None, :]).astype(np.float32))


def _scan_rows(rev):
    t = lax.broadcasted_iota(jnp.int32, (CHUNK, HW), 0)
    return (CHUNK - 1 - t) if rev else t


def _scan_lanes(rev):
    s = lax.broadcasted_iota(jnp.int32, (CHUNK, HW), 1) % CHUNK
    return (CHUNK - 1 - s) if rev else s


def _sh(x, d, rev):
    s = (-d if rev else d) % CHUNK
    return x if s == 0 else pltpu.roll(x, s, axis=0)


def _cumsum_scan(x, pt, rev):
    d = 1
    while d < CHUNK:
        x = x + jnp.where(pt >= d, _sh(x, d, rev), 0.0)
        d *= 2
    return x


def _block_end(x, pt, n, rev):
    r = (n - 1) - pt % n
    bit = 1
    while bit < n:
        x = jnp.where((r & bit) != 0, _sh(x, -bit, rev), x)
        bit *= 2
    return x


def _pick16(x, j, rev):
    jj = (15 - j) if rev else j
    x3 = x.reshape(CHUNK // 16, 16, HW)
    return jnp.broadcast_to(x3[:, jj:jj + 1, :], x3.shape).reshape(CHUNK, HW)


def _row(x, p, rev):
    t = (CHUNK - 1 - p) if rev else p
    return x[t:t + 1, :]


def _bd(x, mask_bf):
    xb = x.astype(BF16)
    return jnp.concatenate([xb] * (HW // CHUNK), axis=0) * mask_bf


def _each(f, *lists):
    return [f(*a) for a in zip(*lists)]


def _gla_chunks(ops_list, lbs, st_refs, mask, revs):
    mask_bf = mask.astype(BF16)
    bf = lambda x: x.astype(BF16)

    def prepare(ops, lb, rev):
        q_raw, v, f_raw = ops
        pt = _scan_rows(rev)
        q = _silu(q_raw)
        kk = jnp.minimum((1.0 - lb) * jax.nn.sigmoid(-f_raw), MAX_INPUT_GATE)
        b = _cumsum_scan(jnp.log1p(-kk), pt, rev)
        e4 = _block_end(b, pt, 4, rev)
        r4 = jnp.where(pt >= 4, _sh(e4, 4, rev), 0.0)
        r16 = jnp.where(pt >= 16, _sh(_pick16(b, 15, rev), 16, rev), 0.0)
        ks01 = [_bd(kk * jnp.exp(r4 - b), mask_bf)]
        ks2 = []
        for i in range(1, 4):
            ks01.append(_bd(kk * jnp.exp(jnp.minimum(_pick16(r4, 4 * i, rev) - b, 0.0)), mask_bf))
            ks2.append(_bd(kk * jnp.exp(jnp.minimum(_row(b, 16 * i - 1, rev) - b, 0.0)), mask_bf))
        return dict(pt=pt, ps=_scan_lanes(rev), v=v, kk=kk, b=b, b_last=_row(b, CHUNK - 1, rev),
                    q0=bf(q * jnp.exp(b - r4)), q2=bf(q * jnp.exp(b - r16)), qe=bf(q * jnp.exp(b)),
                    k01=jnp.concatenate(ks01, axis=0), k2=jnp.concatenate(ks2, axis=0))

    def select(c, r01, r2):
        pt, ps = c['pt'], c['ps']
        pt4, ps4, pt16, ps16 = pt // 4, ps // 4, pt // 16, ps // 16
        attn = jnp.where((pt4 == ps4) & (ps <= pt), r01[:, :HW], 0.0)
        for i in range(1, 4):
            attn = jnp.where((pt16 == ps16) & (ps4 < pt4) & (pt4 % 4 == i), r01[:, i * HW:(i + 1) * HW], attn)
            attn = jnp.where((ps16 < pt16) & (pt16 == i), r2[:, (i - 1) * HW:i * HW], attn)
        return bf(attn)

    cs = _each(prepare, ops_list, lbs, revs)
    r01 = [_dot_t(c['q0'], c['k01']) for c in cs]
    r2 = [_dot_t(c['q2'], c['k2']) for c in cs]
    attn = _each(select, cs, r01, r2)
    st = [s[...] for s in st_refs]
    o = _each(lambda c, s, aa: _dot_t(c['qe'], bf(s)) + _dot(aa, _bd(c['v'], mask_bf)), cs, st, attn)
    upd = [lax.dot_general(bf(c['v']), bf(c['kk'] * jnp.exp(c['b_last'] - c['b'])), (((0,), (0,)), ((), ())),
                           preferred_element_type=F32) for c in cs]
    for s_ref, c, s, u in zip(st_refs, cs, st, upd):
        s_ref[...] = jnp.exp(c['b_last']) * s + mask * u
    return o


def _scan_kernel(*refs, chunk_fn, n_in, n_const, group, has_state):
    n_op = group * 2 * n_in
    ins, consts, rest = refs[:n_op], refs[n_op:n_op + n_const], refs[n_op + n_const:]
    if has_state:
        s0_ref, of_ref, ob_ref, sout_ref, st_s = rest
    else:
        of_ref, ob_ref, sout_ref, st_s = rest
    c = pl.program_id(1)

    @pl.when(c == 0)
    def _():
        st_s[...] = s0_ref[...] if has_state else jnp.zeros(st_s.shape, F32)

    chains = [(g, direction) for g in range(group) for direction in (0, 1)]
    ops_list = [[r[...] for r in ins[(g * 2 + dr) * n_in:(g * 2 + dr + 1) * n_in]] for g, dr in chains]
    outs = chunk_fn(ops_list, consts, [st_s.at[g, dr] for g, dr in chains], [dr for _, dr in chains])
    for (g, dr), o in zip(chains, outs):
        (ob_ref if dr else of_ref)[g] = o

    @pl.when(c == pl.num_programs(1) - 1)
    def _():
        sout_ref[...] = st_s[...]


def _scan_call(name, chunk_fn, ops_f, ops_b, consts, row0, nb, t, s0_bd, group):
    assert nb % group == 0 and len(ops_f) == len(ops_b)
    nc = t // CHUNK
    r0 = row0 // CHUNK
    has_state = s0_bd is not None
    in_specs, args = [], []
    for g in range(group):
        for direction, ops in ((0, ops_f), (1, ops_b)):
            for arr, w, cb in ops:
                if direction == 0:
                    imap = functools.partial(lambda b, c, g, cb: (r0 + (b * group + g) * nc + c, cb), g=g, cb=cb)
                else:
                    imap = functools.partial(lambda b, c, g, cb: (r0 + (b * group + g) * nc + nc - 1 - c, cb),
                                             g=g, cb=cb)
                in_specs.append(pl.BlockSpec((CHUNK, w), imap))
                args.append(arr)
    for cst in consts:
        in_specs.append(pl.BlockSpec(cst.shape, functools.partial(lambda b, c, nd: (0,) * nd, nd=cst.ndim)))
        args.append(cst)
    state_spec = pl.BlockSpec((group, 2, HW, HW), lambda b, c: (b, 0, 0, 0))
    if has_state:
        in_specs.append(state_spec)
        args.append(s0_bd)
    o_f, o_b, s_out = pl.pallas_call(
        functools.partial(_scan_kernel, chunk_fn=chunk_fn, n_in=len(ops_f), n_const=len(consts), group=group,
                          has_state=has_state),
        grid=(nb // group, nc),
        in_specs=in_specs,
        out_specs=[pl.BlockSpec((group, CHUNK, HW), lambda b, c: (b, c, 0)),
                   pl.BlockSpec((group, CHUNK, HW), lambda b, c: (b, nc - 1 - c, 0)),
                   state_spec],
        out_shape=[jax.ShapeDtypeStruct((nb, t, HW), F32), jax.ShapeDtypeStruct((nb, t, HW), F32),
                   jax.ShapeDtypeStruct((nb, 2, HW, HW), F32)],
        scratch_shapes=[pltpu.VMEM((group, 2, HW, HW), F32)],
        compiler_params=_cparams(("parallel", "arbitrary")),
        name=name,
    )(*args)
    return o_f.reshape(nb * t, HW), o_b.reshape(nb * t, HW), s_out


def _hgrn_chunks_fn(ops_list, consts, st_refs, directions):
    lb_ref, mask_ref = consts
    return _gla_chunks(ops_list, [lb_ref[dr:dr + 1, :] for dr in directions], st_refs, mask_ref[...],
                       [dr == 1 for dr in directions])


def _state_to_bd(s):
    nb = s.shape[0]
    eye = jnp.eye(C_HEADS, dtype=F32)
    return jnp.einsum('bdhkv,hg->bdhvgk', s.astype(F32), eye).reshape(nb, 2, HW, HW)


def _state_from_bd(sbd):
    nb = sbd.shape[0]
    s6 = sbd.reshape(nb, 2, C_HEADS, HEAD_DIM, C_HEADS, HEAD_DIM)
    return jnp.stack([s6[:, :, h, :, h, :] for h in range(C_HEADS)], axis=2).transpose(0, 1, 2, 4, 3)


def _hgrn_call(z, row0, nb, t, lb, s0_bd, group):
    ops = lambda fcol: [(z, HW, COL['cq'] // HW), (z, HW, COL['ci'] // HW), (z, HW, fcol // HW)]
    return _scan_call("hgrn_scan", _hgrn_chunks_fn, ops(COL['cff']), ops(COL['cfb']), [lb, _head_mask()],
                      row0, nb, t, s0_bd, group)


def _gate_norm_kernel(of_ref, ob_ref, g_ref, nw_ref, p_ref, o_ref):
    o = of_ref[...] + ob_ref[...]
    y = o * lax.rsqrt(_head_mean_sq(o, p_ref[...]) + EPS) * nw_ref[...]
    o_ref[...] = (y * _silu(g_ref[...])).astype(o_ref.dtype)


def _gate_norm_call(o_f, o_b, z, gcol, row0, nw):
    n = o_f.shape[0]
    row = lambda i: (i, 0)
    return pl.pallas_call(
        _gate_norm_kernel,
        grid=(n // TM,),
        in_specs=[pl.BlockSpec((TM, HW), row), pl.BlockSpec((TM, HW), row),
                  pl.BlockSpec((TM, HW), lambda i: (row0 // TM + i, gcol // HW)),
                  pl.BlockSpec((1, HW), lambda i: (0, 0)),
                  pl.BlockSpec((HW, HW), lambda i: (0, 0))],
        out_specs=pl.BlockSpec((TM, HW), row),
        out_shape=jax.ShapeDtypeStruct((n, HW), BF16),
        compiler_params=_cparams(("parallel",)),
        name="gate_norm",
    )(o_f, o_b, z, jnp.tile(nw, HW // HEAD_DIM).reshape(1, HW), _seg_matrix(HW))


DQKV_W = 3 * HW
HALO = 8


def _softplus(x):
    return jnp.maximum(x, 0.0) + jnp.log1p(jnp.exp(-jnp.abs(x)))


def _delta_prep_kernel(x_ref, xp_ref, xn_ref, cw_ref, dab_ref, ex_ref, al_ref, dtb_ref, p_ref, qkv_ref, gb_ref,
                       *, n_single, tiles_per_seq):
    i = pl.program_id(0)
    j = jnp.maximum(i - n_single, 0) % tiles_per_seq
    first = (i < n_single) | (j == 0)
    last = (i < n_single) | (j == tiles_per_seq - 1)
    x = x_ref[...]
    tm = x.shape[0]
    prev = jnp.where(first, 0.0, xp_ref[...])
    nxt = jnp.where(last, 0.0, xn_ref[...])
    xe = jnp.concatenate([prev, x, nxt], axis=0)
    y = None
    for tap in range(CONV_K):
        lo = HALO + tap - CONV_K // 2
        term = cw_ref[tap:tap + 1, :] * xe[lo:lo + tm]
        y = term if y is None else y + term
    y = _silu(y)
    qk = y[:, :2 * HW]
    ssq = jnp.dot(qk * qk, p_ref[...], preferred_element_type=F32, precision=HIGHEST)
    qkn = qk * lax.rsqrt(ssq + EPS)
    qkv_ref[...] = jnp.concatenate([qkn[:, :HW] * (HEAD_DIM ** -0.5), qkn[:, HW:], y[:, 2 * HW:]], axis=1)
    e = jnp.dot(dab_ref[...], ex_ref[...], preferred_element_type=F32, precision=HIGHEST)
    g = -jnp.exp(al_ref[...]) * _softplus(e[:, :2 * HW] + dtb_ref[...])
    gb_ref[...] = jnp.concatenate([g, jax.nn.sigmoid(e[:, 2 * HW:])], axis=1)


def _delta_prep_call(z, conv_w, a_log, dt_bias, n_single, tiles_per_seq):
    n = z.shape[0]
    cb = COL['dq'] // DQKV_W
    hb = TM // HALO
    nhb = n // HALO
    ex = np.zeros((LANES, 4 * HW), np.float32)
    for r in range(4 * D_HEADS):
        ex[r, r * HEAD_DIM:(r + 1) * HEAD_DIM] = 1.0
    cw = jnp.zeros((8, DQKV_W), F32).at[:CONV_K].set(conv_w.astype(F32))
    const = lambda i: (0, 0)
    return pl.pallas_call(
        functools.partial(_delta_prep_kernel, n_single=n_single, tiles_per_seq=tiles_per_seq),
        grid=(n // TM,),
        in_specs=[pl.BlockSpec((TM, DQKV_W), lambda i: (i, cb)),
                  pl.BlockSpec((HALO, DQKV_W), lambda i: (jnp.maximum(i * hb - 1, 0), cb)),
                  pl.BlockSpec((HALO, DQKV_W), lambda i: (jnp.minimum((i + 1) * hb, nhb - 1), cb)),
                  pl.BlockSpec((8, DQKV_W), const),
                  pl.BlockSpec((TM, LANES), lambda i: (i, COL['dab'] // LANES)),
                  pl.BlockSpec((LANES, 4 * HW), const),
                  pl.BlockSpec((1, 2 * HW), const), pl.BlockSpec((1, 2 * HW), const),
                  pl.BlockSpec((2 * HW, 2 * HW), const)],
        out_specs=[pl.BlockSpec((TM, DQKV_W), lambda i: (i, 0)), pl.BlockSpec((TM, 4 * HW), lambda i: (i, 0))],
        out_shape=[jax.ShapeDtypeStruct((n, DQKV_W), F32), jax.ShapeDtypeStruct((n, 4 * HW), F32)],
        compiler_params=_cparams(("parallel",)),
        name="delta_prep",
    )(z, z, z, cw, z, jnp.asarray(ex),
      jnp.repeat(a_log.astype(F32).reshape(-1), HEAD_DIM).reshape(1, 2 * HW),
      jnp.repeat(dt_bias.astype(F32).reshape(-1), HEAD_DIM).reshape(1, 2 * HW), _seg_matrix(2 * HW))


def _delta_consts():
    r = np.arange(HW)
    h = r // HEAD_DIM
    t = r % HEAD_DIM
    same_h = h[:, None] == h[None, :]
    s16 = same_h & (t[:, None] // 16 == t[None, :] // 16)
    s32 = same_h & (t[:, None] // 32 == t[None, :] // 32)
    mats = [same_h, r[:, None] == r[None, :], s16, s32 & ~s16, same_h & ~s32]
    return jnp.asarray(np.stack(mats).astype(np.float32))


def _tri_inv(abds, eye, m16, m32, m64):
    bf = lambda x: x.astype(BF16)
    a16 = [bf(a * m16) for a in abds]
    tin = [eye - a * m16 for a in abds]
    p = _each(_dot, a16, a16)
    for step in range(3):
        tin = _each(lambda t, pp: t + _dot(bf(t), bf(pp)), tin, p)
        if step < 2:
            p = _each(lambda pp: _dot(bf(pp), bf(pp)), p)
    for m in (m32, m64):
        tb = _each(bf, tin)
        inner = _each(lambda a, t: bf(_dot(bf(a * m), t)), abds, tb)
        tin = _each(lambda t, t_b, i_n: t - _dot(t_b, i_n), tin, tb, inner)
    return tin


def _delta_chunks(ops_list, st_refs, consts_ref, revs):
    mask = consts_ref[0]
    mask_bf = mask.astype(BF16)
    bf = lambda x: x.astype(BF16)
    pts = [_scan_rows(r) for r in revs]
    pss = [_scan_lanes(r) for r in revs]
    qs = [o[0][:, :HW] for o in ops_list]
    ks = [o[0][:, HW:2 * HW] for o in ops_list]
    vs = [o[0][:, 2 * HW:] for o in ops_list]
    betas = [o[2] for o in ops_list]
    gcs = _each(_cumsum_scan, [o[1] for o in ops_list], pts, revs)
    t_idx = lax.broadcasted_iota(jnp.int32, (CHUNK, HW), 0)
    s_idx = lax.broadcasted_iota(jnp.int32, (CHUNK, HW), 1) % CHUNK
    diag = t_idx == s_idx
    gcl = [jnp.sum(jnp.where(diag, gc, 0.0), axis=0, keepdims=True) for gc in gcs]
    gam = _each(lambda gc, gl: jnp.exp(jnp.minimum(gc - gl, 0.0)), gcs, gcl)
    g_last = _each(lambda gc, r: _row(gc, CHUNK - 1, r), gcs, revs)
    egc = [jnp.exp(gc) for gc in gcs]
    kb = _each(lambda k, b: k * b, ks, betas)
    r = _each(lambda k_b, q, k: _dot_t(bf(jnp.concatenate([k_b, q], axis=0)), _bd(k, mask_bf)), kb, qs, ks)
    a = _each(lambda rr, gm, ps, pt: jnp.where(ps < pt, rr[:CHUNK] * gm, 0.0), r, gam, pss, pts)
    aq = _each(lambda rr, gm, ps, pt: jnp.where(ps <= pt, rr[CHUNK:] * gm, 0.0), r, gam, pss, pts)
    abd = [jnp.concatenate([x] * (HW // CHUNK), axis=0) * mask for x in a]
    tin = _tri_inv(abd, consts_ref[1], consts_ref[2], consts_ref[3], consts_ref[4])
    rhs = _each(lambda v, b, k_b, e: jnp.concatenate([_bd(v * b, mask_bf), _bd(k_b * e, mask_bf)], axis=1),
                vs, betas, kb, egc)
    uw_bd = _each(lambda t, rh: _dot(bf(t), rh), tin, rhs)
    uw = [x[0:CHUNK] + x[CHUNK:2 * CHUNK] + x[2 * CHUNK:3 * CHUNK] + x[3 * CHUNK:] for x in uw_bd]
    st = [s[...] for s in st_refs]
    ws_qs = _each(lambda x, q, e, s: _dot_t(bf(jnp.concatenate([x[:, HW:], q * e], axis=0)), bf(s)),
                  uw, qs, egc, st)
    v_new = _each(lambda x, y: x[:, :HW] - y[:CHUNK], uw, ws_qs)
    o = _each(lambda y, aa, vn: y[CHUNK:] + _dot(bf(aa), _bd(vn, mask_bf)), ws_qs, aq, v_new)
    ke = _each(lambda k, gl, gc: bf(k * jnp.exp(gl - gc)), ks, g_last, gcs)
    upd = _each(lambda vn, kk: lax.dot_general(bf(vn), kk, (((0,), (0,)), ((), ())), preferred_element_type=F32),
                v_new, ke)
    for s_ref, gl, s, u in zip(st_refs, g_last, st, upd):
        s_ref[...] = jnp.exp(gl) * s + mask * u
    return o


def _delta_chunks_fn(ops_list, consts, st_refs, directions):
    return _delta_chunks(ops_list, st_refs, consts[0], [d == 1 for d in directions])


def _delta_call(qkv, gb, row0, nb, t, s0_bd, group):
    ops_f = [(qkv, DQKV_W, 0), (gb, HW, 0), (gb, HW, 2)]
    ops_b = [(qkv, DQKV_W, 0), (gb, HW, 1), (gb, HW, 3)]
    return _scan_call("delta_scan", _delta_chunks_fn, ops_f, ops_b, [_delta_consts()], row0, nb, t, s0_bd, group)


def kernel(x_prompt, x_sample, cache_a_k, cache_a_v, cache_b_k, cache_b_v, state_hgrn, state_delta, c, c_ctx, w_mod, b_mod, norm1, norm2, w_in, w_gate, a_qn, a_kn, b_qn, b_kn, b_rpb, c_lb, c_norm, d_conv, d_alog, d_dtbias, d_norm, w_branch, w_out, w_router, b_router, w_gu, b_gu, w_down, b_down):
    nbp, tp, d = x_prompt.shape
    nbs, ts, _ = x_sample.shape
    depth = w_mod.shape[0]
    past = cache_a_k.shape[2]
    n_p = nbp * tp
    n = n_p + nbs * ts
    assert tp == TM and ts % TM == 0 and n_p % ts == 0 and 1 + nbs <= 8 and n % TM_MM == 0
    assert d == ROW_TILE * LANES
    npt, tps = n_p // TM, ts // TM
    grp_p = next(g for g in (4, 2, 1) if nbp % g == 0)
    grp_s = next(g for g in (2, 1) if nbs % g == 0)

    def cond_of_tile(i):
        return jnp.where(i < npt, 0, 1 + (i - npt) // tps)

    x = jnp.concatenate([x_prompt.reshape(n_p, d), x_sample.reshape(nbs * ts, d)], axis=0)
    conds = jnp.zeros((8, d), F32).at[0].set(c_ctx).at[1:1 + nbs].set(c)
    mod = _mod_call(conds, w_mod, b_mod)

    pl_ = jax.nn.softmax(c_lb.astype(F32), axis=0)
    lbs = jnp.cumsum(pl_, axis=0) - pl_[0:1]
    dqkv0 = COL['cg'] + HW
    da0 = dqkv0 + DQKV_W
    dg0 = da0 + 4 * D_HEADS
    w_in_p = jnp.concatenate([w_in[:, :, :dqkv0], w_in[:, :, dg0:dg0 + HW], w_in[:, :, dqkv0:da0],
                              w_in[:, :, da0:dg0],
                              jnp.zeros((depth, d, IN_W_PAD - dg0 - HW), w_in.dtype)], axis=2).astype(BF16)
    w_gate_b = w_gate.astype(BF16)
    w_branch_b = w_branch.astype(BF16)
    w_out_b = w_out.astype(BF16)
    w_router_p = jnp.zeros((depth, d, LANES), F32).at[:, :, :N_EXPERTS].set(w_router)
    b_router_p = jnp.full((depth, 1, LANES), NEG, F32).at[:, 0, :N_EXPERTS].set(b_router)

    tt = jnp.arange(ts, dtype=jnp.int32)
    half = HEAD_DIM // 2
    inv_freq = 1.0 / (ROPE_THETA ** (jnp.arange(0, half, 2, dtype=F32) / half))
    ang_r = (tt // GRID_W).astype(F32)[:, None] * inv_freq[None, :]
    ang_c = (tt % GRID_W).astype(F32)[:, None] * inv_freq[None, :]
    ang = jnp.concatenate([ang_r, ang_r, ang_c, ang_c], axis=-1)
    sign = jnp.asarray(np.where((np.arange(HEAD_DIM) % 32) < 16, -1.0, 1.0).astype(np.float32))
    cos_all = jnp.concatenate([jnp.ones((n_p, HEAD_DIM), F32), jnp.tile(jnp.cos(ang), (nbs, 1))], axis=0)
    sin_all = jnp.concatenate([jnp.zeros((n_p, HEAD_DIM), F32), jnp.tile(jnp.sin(ang) * sign, (nbs, 1))], axis=0)
    rope_tabs = (jnp.tile(cos_all, (1, 2)), jnp.tile(sin_all, (1, 2)))
    scale = HEAD_DIM ** -0.5
    wa, wb = A_KV_HEADS * HEAD_DIM, B_HEADS * HEAD_DIM

    ak_l, av_l, bk_l, bv_l, sc_l, sd_l = [], [], [], [], [], []
    for l in range(depth):
        mod3 = mod[l].reshape(8, 1, N_MOD * d)
        h = _norm_call(x, norm1[l].reshape(1, d), mod3, cond_of_tile)
        z = _mm_call(h, w_in_p, l)

        (qa,) = _prep_call(z, COL['aq'], A_HEADS * HEAD_DIM, a_qn[l], scale, rope_tabs, False)
        ka_n, ka = _prep_call(z, COL['ak'], wa, a_kn[l], 1.0, rope_tabs, True)
        oa_p = _attn_call(qa, ka, z, COL['av'], 0, nbp, tp, A_HEADS, A_KV_HEADS)
        oa_s = _attn_call(qa, ka, z, COL['av'], n_p, nbs, ts, A_HEADS, A_KV_HEADS,
                          ctx=(cache_a_k[:, l].reshape(nbs, past, wa).astype(BF16),
                               cache_a_v[:, l].reshape(nbs, past, wa).astype(BF16)))
        oa = jnp.concatenate([oa_p, oa_s], axis=0)

        (qb,) = _prep_call(z, COL['bq'], wb, b_qn[l], scale, None, False)
        kb_n, kb = _prep_call(z, COL['bk'], wb, b_kn[l], 1.0, None, True)
        ob_p = _attn_call(qb, kb, z, COL['bv'], 0, nbp, tp, B_HEADS, B_HEADS)
        ob_s = _nbr_call(qb, kb, z, COL['bv'], n_p, nbs, ts,
                         cache_b_k[:, l].reshape(nbs, past, wb).astype(BF16),
                         cache_b_v[:, l].reshape(nbs, past, wb).astype(BF16), _nbr_bias_table(b_rpb[l]))
        ob = jnp.concatenate([ob_p, ob_s], axis=0)

        cf_p, cb_p, sc_p = _hgrn_call(z, 0, nbp, tp, lbs[l], None, grp_p)
        cf_s, cb_s, _ = _hgrn_call(z, n_p, nbs, ts, lbs[l], _state_to_bd(state_hgrn[:, l]), grp_s)
        oc = _gate_norm_call(jnp.concatenate([cf_p, cf_s], axis=0), jnp.concatenate([cb_p, cb_s], axis=0),
                             z, COL['cg'], 0, c_norm[l])

        qkv, gb = _delta_prep_call(z, d_conv[l], d_alog[l], d_dtbias[l], npt, tps)
        df_p, db_p, sd_p = _delta_call(qkv, gb, 0, nbp, tp, None, grp_p)
        df_s, db_s, _ = _delta_call(qkv, gb, n_p, nbs, ts, _state_to_bd(state_delta[:, l]), grp_s)
        od = _gate_norm_call(jnp.concatenate([df_p, df_s], axis=0), jnp.concatenate([db_p, db_s], axis=0),
                             z, COL['dg'], 0, d_norm[l])

        x = _merge_call(x, h, oa, ob, oc, od, w_gate_b, w_branch_b, w_out_b, l, mod3, cond_of_tile)

        h_rows, gates, sel, counts = _router_call(x, norm2[l].reshape(1, d), mod3, cond_of_tile,
                                                  w_router_p, b_router_p, l)
        dest, block_e, n_used, n_blocks = _slot_plan(sel, counts, n)
        x_slots = _dispatch_call(dest, h_rows, n_blocks * MOE_TM)
        y_slots = _gmm_call(block_e, n_used, x_slots, w_gu, b_gu, w_down, b_down, l)
        x = _combine_call(dest, x, gates, mod3, cond_of_tile, y_slots)

        ak_l.append(ka_n[:n_p].reshape(nbp, tp, A_KV_HEADS, HEAD_DIM))
        av_l.append(z[:n_p, COL['av']:COL['av'] + wa].reshape(nbp, tp, A_KV_HEADS, HEAD_DIM))
        bk_l.append(kb_n[:n_p].reshape(nbp, tp, B_HEADS, HEAD_DIM))
        bv_l.append(z[:n_p, COL['bv']:COL['bv'] + wb].reshape(nbp, tp, B_HEADS, HEAD_DIM))
        sc_l.append(_state_from_bd(sc_p))
        sd_l.append(_state_from_bd(sd_p))

    return (x[:n_p].reshape(nbp, tp, d), x[n_p:].reshape(nbs, ts, d),
            jnp.stack(ak_l, axis=1), jnp.stack(av_l, axis=1), jnp.stack(bk_l, axis=1), jnp.stack(bv_l, axis=1),
            jnp.stack(sc_l, axis=1), jnp.stack(sd_l, axis=1))
```

```python
import functools

import numpy as np
import jax
import jax.numpy as jnp
from jax import lax
from jax.experimental import pallas as pl
from jax.experimental.pallas import tpu as pltpu

F32 = jnp.float32
BF16 = jnp.bfloat16
HIGHEST = lax.Precision.HIGHEST

GRID_W = 64
HEAD_DIM = 64
A_HEADS = 8
A_KV_HEADS = 2
B_HEADS = 4
NB_ROWS = 8
NB_COLS = 16
C_HEADS = 4
D_HEADS = 4
CONV_K = 5
CHUNK = 64
N_EXPERTS = 32
TOP_K = 4
SWIGLU_LIMIT = 7.0
SWIGLU_ALPHA = 1.702
ROPE_THETA = 10000.0
EPS = 1e-6
MAX_INPUT_GATE = 1.0 - 1e-6
N_MOD = 6
HW = 256

LANES = 128
VMEM_LIMIT = 56 * 1024 * 1024
NEG = -1e30

TM = 256
TM_MM = 512
TN_MM = 1024
TQ = 128
MOE_TM = 256

COL = dict(aq=0, ak=512, av=640, bq=768, bk=1024, bv=1280, cq=1536, ci=1792, cff=2048, cfb=2304,
           cg=2560, dg=2816, dq=3072, dk=3328, dv=3584, dab=3840)
IN_W_PAD = 4096


def _cparams(sem, vmem=VMEM_LIMIT):
    return pltpu.CompilerParams(dimension_semantics=sem, vmem_limit_bytes=vmem)


def _silu(x):
    return x * jax.nn.sigmoid(x)


def _dot(a, b):
    return jnp.dot(a, b, preferred_element_type=F32)


def _dot_t(a, b):
    return lax.dot_general(a, b, (((1,), (1,)), ((), ())), preferred_element_type=F32)


def _each(f, *lists):
    return [f(*a) for a in zip(*lists)]


def _mod_kernel(c_ref, w_ref, b_ref, o_ref):
    a = _silu(c_ref[...]).astype(BF16)
    o_ref[...] = _dot(a, w_ref[...].astype(BF16)) + b_ref[...]


def _mod_call(conds, w_mod, b_mod):
    depth, d, n = w_mod.shape
    tn = 1024
    return pl.pallas_call(
        _mod_kernel,
        grid=(depth, n // tn),
        in_specs=[pl.BlockSpec((8, d), lambda l, j: (0, 0)),
                  pl.BlockSpec((None, d, tn), lambda l, j: (l, 0, j)),
                  pl.BlockSpec((None, 1, tn), lambda l, j: (l, 0, j))],
        out_specs=pl.BlockSpec((None, 8, tn), lambda l, j: (l, 0, j)),
        out_shape=jax.ShapeDtypeStruct((depth, 8, n), F32),
        compiler_params=_cparams(("parallel", "parallel")),
        name="mod",
    )(conds, w_mod, b_mod.reshape(depth, 1, n))


def _mod_norm(x, g, m, k_shift, k_scale, d):
    y = x * lax.rsqrt(jnp.mean(x * x, axis=-1, keepdims=True) + EPS) * g
    return y * (1.0 + m[:, k_scale * d:(k_scale + 1) * d]) + m[:, k_shift * d:(k_shift + 1) * d]


def _norm_kernel(x_ref, g_ref, mod_ref, h_ref, *, d):
    h_ref[...] = _mod_norm(x_ref[...], g_ref[...], mod_ref[...], 0, 1, d).astype(BF16)


def _norm_call(x, g, mod3, cond_of_tile):
    n, d = x.shape
    return pl.pallas_call(
        functools.partial(_norm_kernel, d=d),
        grid=(n // TM,),
        in_specs=[pl.BlockSpec((TM, d), lambda i: (i, 0)),
                  pl.BlockSpec((1, d), lambda i: (0, 0)),
                  pl.BlockSpec((None, 1, N_MOD * d), lambda i: (cond_of_tile(i), 0, 0))],
        out_specs=pl.BlockSpec((TM, d), lambda i: (i, 0)),
        out_shape=jax.ShapeDtypeStruct((n, d), BF16),
        compiler_params=_cparams(("parallel",)),
        name="norm1",
    )(x, g, mod3)


def _mm_kernel(a_ref, w_ref, o_ref):
    o_ref[...] = _dot(a_ref[...], w_ref[...]).astype(o_ref.dtype)


def _mm_call(a, w, l, out_dtype=F32):
    m, k = a.shape
    n = w.shape[2]
    return pl.pallas_call(
        _mm_kernel,
        grid=(n // TN_MM, m // TM_MM),
        in_specs=[pl.BlockSpec((TM_MM, k), lambda j, i: (i, 0)),
                  pl.BlockSpec((None, k, TN_MM), lambda j, i: (l, 0, j))],
        out_specs=pl.BlockSpec((TM_MM, TN_MM), lambda j, i: (i, j)),
        out_shape=jax.ShapeDtypeStruct((m, n), out_dtype),
        compiler_params=_cparams(("parallel", "parallel")),
        name="in_proj",
    )(a, w)


def _two_path(i, n_first, first_ref, second_ref):
    return jnp.where(i < n_first, first_ref[...], second_ref[...])


def _two_path_specs(width, n_first):
    return [pl.BlockSpec((TM, width), lambda i: (jnp.minimum(i, n_first - 1), 0)),
            pl.BlockSpec((TM, width), lambda i: (jnp.maximum(i - n_first, 0), 0))]


def _merge_kernel(x_ref, h_ref, oap_ref, oas_ref, obp_ref, obs_ref, oc_ref, od_ref, wg_ref, wb_ref, wo_ref, mod_ref,
                  o_ref, *, d, n_first):
    i = pl.program_id(0)
    h = h_ref[...]
    outs = (_two_path(i, n_first, oap_ref, oas_ref), _two_path(i, n_first, obp_ref, obs_ref), oc_ref[...], od_ref[...])
    merged = None
    off = 0
    for j in range(4):
        wdt = outs[j].shape[1]
        gate = jax.nn.sigmoid(_dot(h, wg_ref[:, j * d:(j + 1) * d]))
        br = gate * _dot(outs[j], wb_ref[off:off + wdt, :])
        merged = br if merged is None else merged + br
        off += wdt
    y = _dot(merged.astype(BF16), wo_ref[...])
    o_ref[...] = x_ref[...] + mod_ref[...][:, 2 * d:3 * d] * y


def _merge_call(x, h, oa_p, oa_s, ob_p, ob_s, oc, od, w_gate, w_branch, w_out, l, mod3, cond_of_tile, n_first):
    n, d = x.shape
    row = lambda i: (i, 0)
    const = lambda i: (l, 0, 0)
    once = pl.Buffered(1)
    return pl.pallas_call(
        functools.partial(_merge_kernel, d=d, n_first=n_first),
        grid=(n // TM,),
        in_specs=[pl.BlockSpec((TM, d), row), pl.BlockSpec((TM, d), row)]
        + _two_path_specs(oa_p.shape[1], n_first) + _two_path_specs(ob_p.shape[1], n_first)
        + [pl.BlockSpec((TM, oc.shape[1]), row), pl.BlockSpec((TM, od.shape[1]), row),
           pl.BlockSpec((None,) + w_gate.shape[1:], const, pipeline_mode=once),
           pl.BlockSpec((None,) + w_branch.shape[1:], const, pipeline_mode=once),
           pl.BlockSpec((None,) + w_out.shape[1:], const, pipeline_mode=once),
           pl.BlockSpec((None, 1, N_MOD * d), lambda i: (cond_of_tile(i), 0, 0))],
        out_specs=pl.BlockSpec((TM, d), row),
        out_shape=jax.ShapeDtypeStruct((n, d), F32),
        compiler_params=_cparams(("parallel",)),
        name="merge",
    )(x, h, oa_p, oa_s, ob_p, ob_s, oc, od, w_gate, w_branch, w_out, mod3)


def _router_kernel(x_ref, g_ref, mod_ref, wr_ref, br_ref, ltri_ref, h_ref, gate_ref, sel_ref, cnt_ref, run_s, *, d):
    i = pl.program_id(0)

    @pl.when(i == 0)
    def _():
        run_s[...] = jnp.zeros(run_s.shape, F32)

    h2 = _mod_norm(x_ref[...], g_ref[...], mod_ref[...], 3, 4, d)
    nsub = d // LANES
    for sub in range(nsub):
        h_ref[pl.ds(sub, x_ref.shape[0], stride=nsub), :] = h2[:, sub * LANES:(sub + 1) * LANES]
    logits = jnp.dot(h2, wr_ref[...], preferred_element_type=F32, precision=HIGHEST) + br_ref[...]
    lane = lax.broadcasted_iota(jnp.int32, logits.shape, 1)
    vals, idxs = [], []
    cur = logits
    for _ in range(TOP_K):
        m = jnp.max(cur, axis=-1, keepdims=True)
        ix = jnp.min(jnp.where(cur == m, lane, LANES), axis=-1, keepdims=True)
        vals.append(m)
        idxs.append(ix)
        cur = jnp.where(lane == ix, -jnp.inf, cur)
    es = [jnp.exp(v - vals[0]) for v in vals]
    tot = es[0] + es[1] + es[2] + es[3]
    gates = jnp.zeros(logits.shape, F32)
    sel = jnp.zeros(logits.shape, jnp.int32)
    base = run_s[...]
    ltri = ltri_ref[...]
    for k in range(TOP_K):
        onehot = jnp.where(lane == idxs[k], 1.0, 0.0)
        before = base + _dot(ltri, onehot.astype(BF16))
        rank = jnp.sum(onehot * before, axis=-1, keepdims=True).astype(jnp.int32)
        base = base + jnp.sum(onehot, axis=0, keepdims=True)
        gates = jnp.where(lane == k, es[k] / tot, gates)
        sel = jnp.where(lane == k, idxs[k], sel)
        sel = jnp.where(lane == TOP_K + k, rank, sel)
    run_s[...] = base
    gate_ref[...] = gates
    sel_ref[...] = sel
    cnt_ref[...] = base


def _router_call(x, g, mod3, cond_of_tile, w_router_p, b_router_p, l):
    n, d = x.shape
    row = lambda i: (i, 0)
    const = lambda i: (0, 0)
    layer = lambda i: (l, 0, 0)
    ltri = jnp.asarray(np.tril(np.ones((TM, TM), np.float32), -1)).astype(BF16)
    return pl.pallas_call(
        functools.partial(_router_kernel, d=d),
        grid=(n // TM,),
        in_specs=[pl.BlockSpec((TM, d), row),
                  pl.BlockSpec((1, d), const),
                  pl.BlockSpec((None, 1, N_MOD * d), lambda i: (cond_of_tile(i), 0, 0)),
                  pl.BlockSpec((None, d, LANES), layer),
                  pl.BlockSpec((None, 1, LANES), layer),
                  pl.BlockSpec((TM, TM), const)],
        out_specs=[pl.BlockSpec((TM * (d // LANES), LANES), row), pl.BlockSpec((TM, LANES), row),
                   pl.BlockSpec((TM, LANES), row), pl.BlockSpec((1, LANES), const)],
        out_shape=[jax.ShapeDtypeStruct((n * (d // LANES), LANES), F32), jax.ShapeDtypeStruct((n, LANES), F32),
                   jax.ShapeDtypeStruct((n, LANES), jnp.int32), jax.ShapeDtypeStruct((1, LANES), F32)],
        scratch_shapes=[pltpu.VMEM((1, LANES), F32)],
        compiler_params=_cparams(("arbitrary",)),
        name="router",
    )(x, g, mod3, w_router_p, b_router_p, ltri)


def _slot_plan(sel, counts, n_tok):
    cnt = counts[0, :N_EXPERTS].astype(jnp.int32)
    padded = (cnt + MOE_TM - 1) // MOE_TM * MOE_TM
    pad_end = jnp.cumsum(padded)
    pad_start = pad_end - padded
    dest = (pad_start[sel[:, :TOP_K]] + sel[:, TOP_K:2 * TOP_K]).reshape(n_tok * TOP_K)
    n_blocks = n_tok * TOP_K // MOE_TM + N_EXPERTS
    blk = jnp.arange(n_blocks, dtype=jnp.int32)
    block_e = jnp.minimum(jnp.sum((pad_end[None, :] <= (blk * MOE_TM)[:, None]).astype(jnp.int32), axis=1),
                          N_EXPERTS - 1)
    n_used = (pad_end[-1:] // MOE_TM).astype(jnp.int32)
    later = (blk[None, :] > blk[:, None]) & (block_e[None, :] != block_e[:, None]) & (blk[None, :] < n_used[0])
    first_later = jnp.argmax(later, axis=1)
    next_e = jnp.where(jnp.any(later, axis=1), block_e[first_later], -1).astype(jnp.int32)
    last_blk = jnp.where(padded > 0, pad_end // MOE_TM - 1, -1).astype(jnp.int32)
    return dest, block_e, n_used, next_e, last_blk, n_blocks


ROW_TILE = 8


def _row_slice(ref, r):
    return ref.at[pl.ds(pl.multiple_of(r * ROW_TILE, ROW_TILE), ROW_TILE)]


def _dispatch_kernel(dest_ref, last_ref, nu_ref, h_ref, o_ref, zero_s, sem, zsem):
    rows = h_ref.shape[0]
    tm = rows // ROW_TILE
    grp = zero_s.shape[0]
    n_groups = o_ref.shape[0] // grp
    base = pl.program_id(0) * (tm * TOP_K)

    @pl.when(pl.program_id(0) == 0)
    def _():
        zero_s[...] = jnp.zeros(zero_s.shape, zero_s.dtype)

        def fill(group):
            return pltpu.make_async_copy(zero_s, o_ref.at[pl.ds(pl.multiple_of(group * grp, grp), grp)], zsem)

        targets = [(last_ref[e] >= 0, last_ref[e]) for e in range(N_EXPERTS)]
        targets += [(nu_ref[0] + j < n_groups, nu_ref[0] + j) for j in range(N_EXPERTS)]
        for do, group in targets:
            @pl.when(do)
            def _():
                fill(group).start()
        for do, group in targets:
            @pl.when(do)
            def _():
                fill(group).wait()

    def body(t, carry):
        for k in range(TOP_K):
            pltpu.make_async_copy(_row_slice(h_ref, t), _row_slice(o_ref, dest_ref[base + t * TOP_K + k]),
                                  sem).start()
        return carry

    lax.fori_loop(0, tm, body, 0)
    for _ in range(TOP_K):
        pltpu.make_async_copy(h_ref, o_ref.at[pl.ds(0, rows)], sem).wait()


def _dispatch_call(dest, last_blk, n_used, h_rows, n_slots):
    n = h_rows.shape[0] // ROW_TILE
    grid_spec = pltpu.PrefetchScalarGridSpec(
        num_scalar_prefetch=3,
        grid=(n // TM,),
        in_specs=[pl.BlockSpec((TM * ROW_TILE, LANES), lambda i, dst, lst, nu: (i, 0))],
        out_specs=pl.BlockSpec(memory_space=pl.ANY),
        scratch_shapes=[pltpu.VMEM((MOE_TM * ROW_TILE, LANES), F32), pltpu.SemaphoreType.DMA(()),
                        pltpu.SemaphoreType.DMA(())],
    )
    return pl.pallas_call(
        _dispatch_kernel,
        grid_spec=grid_spec,
        out_shape=jax.ShapeDtypeStruct((n_slots * ROW_TILE, LANES), F32),
        compiler_params=_cparams(("arbitrary",)),
        name="dispatch",
    )(dest, last_blk, n_used, h_rows)


def _gmm_kernel(be_ref, nu_ref, nxt_ref, x_ref, wgu_hbm, bgu_ref, wd_hbm, bd_ref, y_ref, wgu_f, wd_f, wgu_s, wd_s,
                sems, *, f, layer):
    i = pl.program_id(0)
    used = i < nu_ref[0]
    changed = (i == 0) | (be_ref[i] != be_ref[jnp.maximum(i - 1, 0)])
    tm = x_ref.shape[0] // ROW_TILE

    def weight_copies(e):
        return (pltpu.make_async_copy(wgu_hbm.at[layer, e], wgu_f, sems.at[0]),
                pltpu.make_async_copy(wd_hbm.at[layer, e], wd_f, sems.at[1]))

    @pl.when(used & (i == 0))
    def _():
        for cp in weight_copies(be_ref[0]):
            cp.start()

    @pl.when(used & changed)
    def _():
        for cp in weight_copies(be_ref[i]):
            cp.wait()
        wgu_s[...] = wgu_f[...].astype(BF16)
        wd_s[...] = wd_f[...].astype(BF16)

        @pl.when(nxt_ref[i] >= 0)
        def _():
            for cp in weight_copies(nxt_ref[i]):
                cp.start()

    @pl.when(used)
    def _():
        x = jnp.concatenate([x_ref[pl.ds(sub, tm, stride=ROW_TILE), :] for sub in range(ROW_TILE)], axis=1)
        gu = _dot(x.astype(BF16), wgu_s[...]) + bgu_ref[...]
        gate = jnp.minimum(gu[:, :f], SWIGLU_LIMIT)
        up = jnp.clip(gu[:, f:], -SWIGLU_LIMIT, SWIGLU_LIMIT)
        act = gate * jax.nn.sigmoid(SWIGLU_ALPHA * gate) * (up + 1.0)
        y = _dot(act.astype(BF16), wd_s[...]) + bd_ref[...]
        for sub in range(ROW_TILE):
            y_ref[pl.ds(sub, tm, stride=ROW_TILE), :] = y[:, sub * LANES:(sub + 1) * LANES]

    @pl.when(jnp.logical_not(used))
    def _():
        y_ref[...] = jnp.zeros(y_ref.shape, y_ref.dtype)


def _gmm_call(block_e, n_used, next_e, x_slots, w_gu, b_gu, w_down, b_down, l):
    n_slots = x_slots.shape[0] // ROW_TILE
    depth, e, d, f2 = w_gu.shape
    f = f2 // 2
    n_blocks = n_slots // MOE_TM
    xrow = lambda i, be, nu, nx: (jnp.minimum(i, nu[0] - 1), 0)
    yrow = lambda i, be, nu, nx: (i, 0)
    bsel = lambda i, be, nu, nx: (l, be[i], 0, 0)
    grid_spec = pltpu.PrefetchScalarGridSpec(
        num_scalar_prefetch=3,
        grid=(n_blocks,),
        in_specs=[pl.BlockSpec((MOE_TM * ROW_TILE, LANES), xrow),
                  pl.BlockSpec(memory_space=pl.ANY),
                  pl.BlockSpec((None, None, 1, f2), bsel),
                  pl.BlockSpec(memory_space=pl.ANY),
                  pl.BlockSpec((None, None, 1, d), bsel)],
        out_specs=pl.BlockSpec((MOE_TM * ROW_TILE, LANES), yrow),
        scratch_shapes=[pltpu.VMEM((d, f2), F32), pltpu.VMEM((f, d), F32), pltpu.VMEM((d, f2), BF16),
                        pltpu.VMEM((f, d), BF16), pltpu.SemaphoreType.DMA((2,))],
    )
    return pl.pallas_call(
        functools.partial(_gmm_kernel, f=f, layer=l),
        grid_spec=grid_spec,
        out_shape=jax.ShapeDtypeStruct((n_slots * ROW_TILE, LANES), F32),
        compiler_params=_cparams(("arbitrary",)),
        name="experts",
    )(block_e, n_used, next_e, x_slots, w_gu, b_gu.reshape(depth, e, 1, f2), w_down, b_down.reshape(depth, e, 1, d))


def _combine_kernel(dest_ref, x_ref, gate_ref, mod_ref, y_ref, o_ref, buf, sems, *, d):
    tm = x_ref.shape[0]
    rows = tm * ROW_TILE
    i = pl.program_id(0)
    n_steps = pl.num_programs(0)

    def issue(step, slot):
        base = step * (tm * TOP_K)

        def body(t, carry):
            for k in range(TOP_K):
                dst = buf.at[pl.ds(pl.multiple_of((slot * TOP_K + k) * rows + t * ROW_TILE, ROW_TILE), ROW_TILE)]
                pltpu.make_async_copy(_row_slice(y_ref, dest_ref[base + t * TOP_K + k]), dst, sems.at[slot]).start()
            return carry

        lax.fori_loop(0, tm, body, 0)

    @pl.when(i == 0)
    def _():
        issue(0, 0)

    @pl.when(i + 1 < n_steps)
    def _():
        issue(i + 1, (i + 1) % 2)

    slot = i % 2
    for k in range(TOP_K):
        off = pl.multiple_of((slot * TOP_K + k) * rows, ROW_TILE)
        pltpu.make_async_copy(y_ref.at[pl.ds(0, rows)], buf.at[pl.ds(off, rows)], sems.at[slot]).wait()
    g = gate_ref[...]
    gk = [jnp.broadcast_to(g[:, k:k + 1], (tm, LANES)) for k in range(TOP_K)]
    g2 = mod_ref[...][:, 5 * d:6 * d]
    for sub in range(ROW_TILE):
        sl = slice(sub * LANES, (sub + 1) * LANES)
        acc = None
        for k in range(TOP_K):
            off = pl.multiple_of((slot * TOP_K + k) * rows, ROW_TILE)
            term = gk[k] * buf[pl.ds(off + sub, tm, stride=ROW_TILE), :]
            acc = term if acc is None else acc + term
        o_ref[:, sl] = x_ref[:, sl] + g2[:, sl] * acc


def _combine_call(dest, x, gates, mod3, cond_of_tile, y_slots):
    n, d = x.shape
    row = lambda i, dst: (i, 0)
    grid_spec = pltpu.PrefetchScalarGridSpec(
        num_scalar_prefetch=1,
        grid=(n // TM,),
        in_specs=[pl.BlockSpec((TM, d), row),
                  pl.BlockSpec((TM, LANES), row),
                  pl.BlockSpec((None, 1, N_MOD * d), lambda i, dst: (cond_of_tile(i), 0, 0)),
                  pl.BlockSpec(memory_space=pl.ANY)],
        out_specs=pl.BlockSpec((TM, d), row),
        scratch_shapes=[pltpu.VMEM((2 * TOP_K * TM * ROW_TILE, LANES), F32), pltpu.SemaphoreType.DMA((2,))],
    )
    return pl.pallas_call(
        functools.partial(_combine_kernel, d=d),
        grid_spec=grid_spec,
        out_shape=jax.ShapeDtypeStruct((n, d), F32),
        compiler_params=_cparams(("arbitrary",)),
        name="combine",
    )(dest, x, gates, mod3, y_slots)


def _seg_matrix(w):
    i = np.arange(w) // HEAD_DIM
    return jnp.asarray((i[:, None] == i[None, :]).astype(np.float32)).astype(BF16)


def _seg_sum(y, p):
    hi = y.astype(BF16)
    lo = (y - hi.astype(F32)).astype(BF16)
    return _dot(hi, p) + _dot(lo, p)


def _head_mean_sq(x, p):
    return _seg_sum(x * x, p) * (1.0 / HEAD_DIM)


def _prep_kernel(z_ref, nw_ref, p_ref, *rest, scale, rope, want_norm):
    if rope:
        cos_ref, sin_ref = rest[0], rest[1]
        rest = rest[2:]
    x = z_ref[...]
    w = x.shape[1]
    y = x * lax.rsqrt(_head_mean_sq(x, p_ref[...]) + EPS) * nw_ref[...]
    if want_norm:
        rest[0][...] = y
    out_ref = rest[-1]
    if rope:
        reps = w // LANES
        cos = jnp.concatenate([cos_ref[...]] * reps, axis=1) if reps > 1 else cos_ref[...]
        sin = jnp.concatenate([sin_ref[...]] * reps, axis=1) if reps > 1 else sin_ref[...]
        lane = lax.broadcasted_iota(jnp.int32, y.shape, 1)
        first = (lane % 32) < 16
        rot = jnp.where(first, pltpu.roll(y, w - 16, axis=1), pltpu.roll(y, 16, axis=1))
        y = y * cos + rot * sin
    out_ref[...] = (y * scale).astype(BF16)


def _prep_call(z, col, w, nw, scale, rope_tabs, want_norm):
    n = z.shape[0]
    cb = col // w
    row = lambda i: (i, 0)
    in_specs = [pl.BlockSpec((TM, w), lambda i: (i, cb)),
                pl.BlockSpec((1, w), lambda i: (0, 0)),
                pl.BlockSpec((w, w), lambda i: (0, 0))]
    args = [z, jnp.tile(nw, w // HEAD_DIM).reshape(1, w), _seg_matrix(w)]
    if rope_tabs is not None:
        in_specs += [pl.BlockSpec((TM, LANES), row), pl.BlockSpec((TM, LANES), row)]
        args += list(rope_tabs)
    out_specs, out_shape = [], []
    if want_norm:
        out_specs.append(pl.BlockSpec((TM, w), row))
        out_shape.append(jax.ShapeDtypeStruct((n, w), F32))
    out_specs.append(pl.BlockSpec((TM, w), row))
    out_shape.append(jax.ShapeDtypeStruct((n, w), BF16))
    return pl.pallas_call(
        functools.partial(_prep_kernel, scale=scale, rope=rope_tabs is not None, want_norm=want_norm),
        grid=(n // TM,),
        in_specs=in_specs, out_specs=out_specs, out_shape=out_shape,
        compiler_params=_cparams(("parallel",)),
        name="qk_prep",
    )(*args)


def _attn_kernel(q_ref, k_ref, v_ref, *rest, hq, hk, has_ctx):
    if has_ctx:
        ck_ref, cv_ref, o_ref = rest
    else:
        (o_ref,) = rest
    g = hq // hk
    tq = q_ref.shape[0]
    dh = HEAD_DIM
    outs = []
    for j in range(hk):
        q4 = jnp.concatenate([q_ref[:, (j * g + gi) * dh:(j * g + gi + 1) * dh] for gi in range(g)], axis=0)
        s = _dot_t(q4, k_ref[:, j * dh:(j + 1) * dh])
        m = jnp.max(s, axis=-1, keepdims=True)
        if has_ctx:
            sc = _dot_t(q4, ck_ref[:, j * dh:(j + 1) * dh])
            m = jnp.maximum(m, jnp.max(sc, axis=-1, keepdims=True))
        p = jnp.exp(s - m)
        l = jnp.sum(p, axis=-1, keepdims=True)
        o = _dot(p.astype(BF16), v_ref[:, j * dh:(j + 1) * dh].astype(BF16))
        if has_ctx:
            pc = jnp.exp(sc - m)
            l = l + jnp.sum(pc, axis=-1, keepdims=True)
            o = o + _dot(pc.astype(BF16), cv_ref[:, j * dh:(j + 1) * dh])
        o = o / l
        outs += [o[gi * tq:(gi + 1) * tq] for gi in range(g)]
    o_ref[...] = jnp.concatenate(outs, axis=1).astype(o_ref.dtype)


def _attn_call(q, k, v, vcol, row0, nb, t, hq, hk, ctx=None):
    tq = min(TQ, t)
    wq, wk = hq * HEAD_DIM, hk * HEAD_DIM
    qb0, kb0, vb = row0 // tq, row0 // t, vcol // wk
    nq = t // tq
    in_specs = [pl.BlockSpec((tq, wq), lambda b, i: (qb0 + b * nq + i, 0)),
                pl.BlockSpec((t, wk), lambda b, i: (kb0 + b, 0)),
                pl.BlockSpec((t, wk), lambda b, i: (kb0 + b, vb))]
    args = [q, k, v]
    if ctx is not None:
        p = ctx[0].shape[1]
        in_specs += [pl.BlockSpec((None, p, wk), lambda b, i: (b, 0, 0))] * 2
        args += list(ctx)
    return pl.pallas_call(
        functools.partial(_attn_kernel, hq=hq, hk=hk, has_ctx=ctx is not None),
        grid=(nb, nq),
        in_specs=in_specs,
        out_specs=pl.BlockSpec((tq, wq), lambda b, i: (b * nq + i, 0)),
        out_shape=jax.ShapeDtypeStruct((nb * t, wq), BF16),
        compiler_params=_cparams(("parallel", "parallel")),
        name="attn",
    )(*args)


def _nbr_bias_table(rpb):
    o = np.arange(NB_ROWS)[:, None, None, None]
    cc = np.arange(GRID_W)[None, :, None, None]
    i = np.arange(NB_ROWS)[None, None, :, None]
    j = np.arange(GRID_W)[None, None, None, :]
    col_start = np.clip(cc - NB_COLS // 2, 0, GRID_W - NB_COLS)
    valid = (j >= col_start) & (j < col_start + NB_COLS)
    sel_r = ((i - o + NB_ROWS - 1)[..., None] == np.arange(2 * NB_ROWS - 1)).astype(np.float32)[:, 0, :, 0]
    sel_c = (((j - cc + NB_COLS - 1)[..., None] == np.arange(2 * NB_COLS - 1)) & valid[..., None])
    sel_c = sel_c.astype(np.float32)[0, :, 0]
    tab = jnp.einsum('hrd,oir,cjd->hocij', rpb.astype(F32), jnp.asarray(sel_r), jnp.asarray(sel_c),
                     precision=HIGHEST)
    tab = tab + jnp.asarray(np.where(valid, 0.0, NEG).astype(np.float32))[None, :, :, :, :]
    return tab.reshape(rpb.shape[0], NB_ROWS, GRID_W, NB_ROWS * GRID_W)


def _nbr_kernel(q_ref, k_ref, v_ref, ck_ref, cv_ref, bias_ref, o_ref, *, rows):
    r = pl.program_id(1)
    rs = jnp.clip(r - NB_ROWS // 2, 0, rows - NB_ROWS)
    start = pl.multiple_of(rs * GRID_W, GRID_W)
    kw = k_ref[pl.ds(start, NB_ROWS * GRID_W), :]
    vw = v_ref[pl.ds(start, NB_ROWS * GRID_W), :].astype(BF16)
    dh = HEAD_DIM
    sls = [slice(h * dh, (h + 1) * dh) for h in range(B_HEADS)]
    rowmax = lambda x: jnp.max(x, axis=-1, keepdims=True)
    rowsum = lambda x: jnp.sum(x, axis=-1, keepdims=True)
    qs = [q_ref[:, sl] for sl in sls]
    s_loc = [_dot_t(q, kw[:, sl]) + bias_ref[h] for h, (q, sl) in enumerate(zip(qs, sls))]
    s_ctx = [_dot_t(q, ck_ref[:, sl]) for q, sl in zip(qs, sls)]
    m = _each(lambda a, b: jnp.maximum(rowmax(a), rowmax(b)), s_loc, s_ctx)
    p_loc = _each(lambda a, mm: jnp.exp(a - mm), s_loc, m)
    p_ctx = _each(lambda a, mm: jnp.exp(a - mm), s_ctx, m)
    l = _each(lambda a, b: rowsum(a) + rowsum(b), p_loc, p_ctx)
    o = [_dot(a.astype(BF16), vw[:, sl]) + _dot(b.astype(BF16), cv_ref[:, sl]) for a, b, sl in zip(p_loc, p_ctx, sls)]
    o_ref[...] = jnp.concatenate(_each(lambda a, b: a / b, o, l), axis=1).astype(o_ref.dtype)


def _nbr_call(q, k, z, vcol, row0, nb, t, ck, cv, bias):
    rows = t // GRID_W
    assert rows >= NB_ROWS
    w = B_HEADS * HEAD_DIM
    p = ck.shape[1]
    nwin = NB_ROWS * GRID_W

    def bias_map(b, r):
        return (0, r - jnp.clip(r - NB_ROWS // 2, 0, rows - NB_ROWS), 0, 0)

    return pl.pallas_call(
        functools.partial(_nbr_kernel, rows=rows),
        grid=(nb, rows),
        in_specs=[pl.BlockSpec((GRID_W, w), lambda b, r: (row0 // GRID_W + b * rows + r, 0)),
                  pl.BlockSpec((t, w), lambda b, r: (row0 // t + b, 0)),
                  pl.BlockSpec((t, w), lambda b, r: (row0 // t + b, vcol // w)),
                  pl.BlockSpec((None, p, w), lambda b, r: (b, 0, 0)),
                  pl.BlockSpec((None, p, w), lambda b, r: (b, 0, 0)),
                  pl.BlockSpec((B_HEADS, None, GRID_W, nwin), bias_map)],
        out_specs=pl.BlockSpec((GRID_W, w), lambda b, r: (b * rows + r, 0)),
        out_shape=jax.ShapeDtypeStruct((nb * t, w), BF16),
        compiler_params=_cparams(("parallel", "arbitrary")),
        name="nbr_attn",
    )(q, k, z, ck, cv, bias)


def _head_mask():
    i = np.arange(HW) // HEAD_DIM
    return jnp.asarray((i[:, None] == i[None, :]).astype(np.float32))


def _scan_rows(rev):
    t = lax.broadcasted_iota(jnp.int32, (CHUNK, HW), 0)
    return (CHUNK - 1 - t) if rev else t


def _scan_lanes(rev):
    s = lax.broadcasted_iota(jnp.int32, (CHUNK, HW), 1) % CHUNK
    return (CHUNK - 1 - s) if rev else s


def _sh(x, d, rev):
    s = (-d if rev else d) % CHUNK
    return x if s == 0 else pltpu.roll(x, s, axis=0)


def _cumsum_scan(x, pt, rev):
    d = 1
    while d < CHUNK:
        x = x + jnp.where(pt >= d, _sh(x, d, rev), 0.0)
        d *= 2
    return x


def _block_end(x, pt, n, rev):
    r = (n - 1) - pt % n
    bit = 1
    while bit < n:
        x = jnp.where((r & bit) != 0, _sh(x, -bit, rev), x)
        bit *= 2
    return x


def _pick16(x, j, rev):
    jj = (15 - j) if rev else j
    x3 = x.reshape(CHUNK // 16, 16, HW)
    return jnp.broadcast_to(x3[:, jj:jj + 1, :], x3.shape).reshape(CHUNK, HW)


def _row(x, p, rev):
    t = (CHUNK - 1 - p) if rev else p
    return x[t:t + 1, :]


def _bd(x, mask_bf):
    xb = x.astype(BF16)
    return jnp.concatenate([xb] * (HW // CHUNK), axis=0) * mask_bf


def _gla_chunks(ops_list, lbs, st_refs, mask, revs):
    mask_bf = mask.astype(BF16)
    bf = lambda x: x.astype(BF16)

    def prepare(ops, lb, rev):
        q_raw, v, f_raw = ops
        pt = _scan_rows(rev)
        q = _silu(q_raw)
        kk = jnp.minimum((1.0 - lb) * jax.nn.sigmoid(-f_raw), MAX_INPUT_GATE)
        b = _cumsum_scan(jnp.log1p(-kk), pt, rev)
        e4 = _block_end(b, pt, 4, rev)
        r4 = jnp.where(pt >= 4, _sh(e4, 4, rev), 0.0)
        r16 = jnp.where(pt >= 16, _sh(_pick16(b, 15, rev), 16, rev), 0.0)
        ks01 = [_bd(kk * jnp.exp(r4 - b), mask_bf)]
        ks2 = []
        for i in range(1, 4):
            ks01.append(_bd(kk * jnp.exp(jnp.minimum(_pick16(r4, 4 * i, rev) - b, 0.0)), mask_bf))
            ks2.append(_bd(kk * jnp.exp(jnp.minimum(_row(b, 16 * i - 1, rev) - b, 0.0)), mask_bf))
        return dict(pt=pt, ps=_scan_lanes(rev), v=v, kk=kk, b=b, b_last=_row(b, CHUNK - 1, rev),
                    q0=bf(q * jnp.exp(b - r4)), q2=bf(q * jnp.exp(b - r16)), qe=bf(q * jnp.exp(b)),
                    k01=jnp.concatenate(ks01, axis=0), k2=jnp.concatenate(ks2, axis=0))

    def select(c, r01, r2):
        pt, ps = c['pt'], c['ps']
        pt4, ps4, pt16, ps16 = pt // 4, ps // 4, pt // 16, ps // 16
        attn = jnp.where((pt4 == ps4) & (ps <= pt), r01[:, :HW], 0.0)
        for i in range(1, 4):
            attn = jnp.where((pt16 == ps16) & (ps4 < pt4) & (pt4 % 4 == i), r01[:, i * HW:(i + 1) * HW], attn)
            attn = jnp.where((ps16 < pt16) & (pt16 == i), r2[:, (i - 1) * HW:i * HW], attn)
        return bf(attn)

    cs = _each(prepare, ops_list, lbs, revs)
    r01 = [_dot_t(c['q0'], c['k01']) for c in cs]
    r2 = [_dot_t(c['q2'], c['k2']) for c in cs]
    attn = _each(select, cs, r01, r2)
    st = [s[...] for s in st_refs]
    o = _each(lambda c, s, aa: _dot_t(c['qe'], bf(s)) + _dot(aa, _bd(c['v'], mask_bf)), cs, st, attn)
    upd = [lax.dot_general(bf(c['v']), bf(c['kk'] * jnp.exp(c['b_last'] - c['b'])), (((0,), (0,)), ((), ())),
                           preferred_element_type=F32) for c in cs]
    for s_ref, c, s, u in zip(st_refs, cs, st, upd):
        s_ref[...] = jnp.exp(c['b_last']) * s + mask * u
    return o


def _scan_kernel(*refs, chunk_fn, n_in, n_const, group, has_state):
    n_op = group * 2 * n_in
    ins, consts, rest = refs[:n_op], refs[n_op:n_op + n_const], refs[n_op + n_const:]
    if has_state:
        s0_ref, of_ref, ob_ref, sout_ref, st_s = rest
    else:
        of_ref, ob_ref, sout_ref, st_s = rest
    c = pl.program_id(1)

    @pl.when(c == 0)
    def _():
        st_s[...] = s0_ref[...] if has_state else jnp.zeros(st_s.shape, F32)

    chains = [(g, direction) for g in range(group) for direction in (0, 1)]
    ops_list = [[r[...] for r in ins[(g * 2 + dr) * n_in:(g * 2 + dr + 1) * n_in]] for g, dr in chains]
    outs = chunk_fn(ops_list, consts, [st_s.at[g, dr] for g, dr in chains], [dr for _, dr in chains])
    for (g, dr), o in zip(chains, outs):
        (ob_ref if dr else of_ref)[g] = o

    @pl.when(c == pl.num_programs(1) - 1)
    def _():
        sout_ref[...] = st_s[...]


def _scan_call(name, chunk_fn, ops_f, ops_b, consts, row0, nb, t, s0_bd, group):
    assert nb % group == 0 and len(ops_f) == len(ops_b)
    nc = t // CHUNK
    r0 = row0 // CHUNK
    has_state = s0_bd is not None
    in_specs, args = [], []
    for g in range(group):
        for direction, ops in ((0, ops_f), (1, ops_b)):
            for arr, w, cb in ops:
                if direction == 0:
                    imap = functools.partial(lambda b, c, g, cb: (r0 + (b * group + g) * nc + c, cb), g=g, cb=cb)
                else:
                    imap = functools.partial(lambda b, c, g, cb: (r0 + (b * group + g) * nc + nc - 1 - c, cb),
                                             g=g, cb=cb)
                in_specs.append(pl.BlockSpec((CHUNK, w), imap))
                args.append(arr)
    for cst in consts:
        in_specs.append(pl.BlockSpec(cst.shape, functools.partial(lambda b, c, nd: (0,) * nd, nd=cst.ndim)))
        args.append(cst)
    state_spec = pl.BlockSpec((group, 2, HW, HW), lambda b, c: (b, 0, 0, 0))
    if has_state:
        in_specs.append(state_spec)
        args.append(s0_bd)
    o_f, o_b, s_out = pl.pallas_call(
        functools.partial(_scan_kernel, chunk_fn=chunk_fn, n_in=len(ops_f), n_const=len(consts), group=group,
                          has_state=has_state),
        grid=(nb // group, nc),
        in_specs=in_specs,
        out_specs=[pl.BlockSpec((group, CHUNK, HW), lambda b, c: (b, c, 0)),
                   pl.BlockSpec((group, CHUNK, HW), lambda b, c: (b, nc - 1 - c, 0)),
                   state_spec],
        out_shape=[jax.ShapeDtypeStruct((nb, t, HW), F32), jax.ShapeDtypeStruct((nb, t, HW), F32),
                   jax.ShapeDtypeStruct((nb, 2, HW, HW), F32)],
        scratch_shapes=[pltpu.VMEM((group, 2, HW, HW), F32)],
        compiler_params=_cparams(("parallel", "arbitrary")),
        name=name,
    )(*args)
    return o_f.reshape(nb * t, HW), o_b.reshape(nb * t, HW), s_out


def _hgrn_chunks_fn(ops_list, consts, st_refs, directions):
    lb_ref, mask_ref = consts
    return _gla_chunks(ops_list, [lb_ref[dr:dr + 1, :] for dr in directions], st_refs, mask_ref[...],
                       [dr == 1 for dr in directions])


def _state_to_bd(s):
    nb = s.shape[0]
    eye = jnp.eye(C_HEADS, dtype=F32)
    return jnp.einsum('bdhkv,hg->bdhvgk', s.astype(F32), eye).reshape(nb, 2, HW, HW)


def _state_from_bd(sbd):
    nb = sbd.shape[0]
    s6 = sbd.reshape(nb, 2, C_HEADS, HEAD_DIM, C_HEADS, HEAD_DIM)
    return jnp.stack([s6[:, :, h, :, h, :] for h in range(C_HEADS)], axis=2).transpose(0, 1, 2, 4, 3)


def _hgrn_call(z, row0, nb, t, lb, s0_bd, group):
    ops = lambda fcol: [(z, HW, COL['cq'] // HW), (z, HW, COL['ci'] // HW), (z, HW, fcol // HW)]
    return _scan_call("hgrn_scan", _hgrn_chunks_fn, ops(COL['cff']), ops(COL['cfb']), [lb, _head_mask()],
                      row0, nb, t, s0_bd, group)


def _gate_norm_kernel(ofp_ref, ofs_ref, obp_ref, obs_ref, g_ref, nw_ref, p_ref, o_ref, *, n_first):
    i = pl.program_id(0)
    o = _two_path(i, n_first, ofp_ref, ofs_ref) + _two_path(i, n_first, obp_ref, obs_ref)
    y = o * lax.rsqrt(_head_mean_sq(o, p_ref[...]) + EPS) * nw_ref[...]
    o_ref[...] = (y * _silu(g_ref[...])).astype(o_ref.dtype)


def _gate_norm_call(of_p, of_s, ob_p, ob_s, z, gcol, nw, n_first):
    n = of_p.shape[0] + of_s.shape[0]
    row = lambda i: (i, 0)
    return pl.pallas_call(
        functools.partial(_gate_norm_kernel, n_first=n_first),
        grid=(n // TM,),
        in_specs=_two_path_specs(HW, n_first) + _two_path_specs(HW, n_first)
        + [pl.BlockSpec((TM, HW), lambda i: (i, gcol // HW)),
           pl.BlockSpec((1, HW), lambda i: (0, 0)),
           pl.BlockSpec((HW, HW), lambda i: (0, 0))],
        out_specs=pl.BlockSpec((TM, HW), row),
        out_shape=jax.ShapeDtypeStruct((n, HW), BF16),
        compiler_params=_cparams(("parallel",)),
        name="gate_norm",
    )(of_p, of_s, ob_p, ob_s, z, jnp.tile(nw, HW // HEAD_DIM).reshape(1, HW), _seg_matrix(HW))


DQKV_W = 3 * HW
HALO = 8


def _softplus(x):
    return jnp.maximum(x, 0.0) + jnp.log1p(jnp.exp(-jnp.abs(x)))


def _delta_prep_kernel(x_ref, xp_ref, xn_ref, cw_ref, dab_ref, ex_ref, al_ref, dtb_ref, p_ref, qkv_ref, gb_ref,
                       *, n_single, tiles_per_seq):
    i = pl.program_id(0)
    j = jnp.maximum(i - n_single, 0) % tiles_per_seq
    first = (i < n_single) | (j == 0)
    last = (i < n_single) | (j == tiles_per_seq - 1)
    x = x_ref[...]
    tm = x.shape[0]
    prev = jnp.where(first, 0.0, xp_ref[...])
    nxt = jnp.where(last, 0.0, xn_ref[...])
    xe = jnp.concatenate([prev, x, nxt], axis=0)
    y = None
    for tap in range(CONV_K):
        lo = HALO + tap - CONV_K // 2
        term = cw_ref[tap:tap + 1, :] * xe[lo:lo + tm]
        y = term if y is None else y + term
    y = _silu(y)
    qk = y[:, :2 * HW]
    ssq = _seg_sum(qk * qk, p_ref[...])
    qkn = qk * lax.rsqrt(ssq + EPS)
    qkv_ref[...] = jnp.concatenate([qkn[:, :HW] * (HEAD_DIM ** -0.5), qkn[:, HW:], y[:, 2 * HW:]], axis=1)
    e = jnp.dot(dab_ref[...], ex_ref[...], preferred_element_type=F32, precision=HIGHEST)
    g = -jnp.exp(al_ref[...]) * _softplus(e[:, :2 * HW] + dtb_ref[...])
    gb_ref[...] = jnp.concatenate([g, jax.nn.sigmoid(e[:, 2 * HW:])], axis=1)


def _delta_prep_call(z, conv_w, a_log, dt_bias, n_single, tiles_per_seq):
    n = z.shape[0]
    cb = COL['dq'] // DQKV_W
    hb = TM // HALO
    nhb = n // HALO
    ex = np.zeros((LANES, 4 * HW), np.float32)
    for r in range(4 * D_HEADS):
        ex[r, r * HEAD_DIM:(r + 1) * HEAD_DIM] = 1.0
    cw = jnp.zeros((8, DQKV_W), F32).at[:CONV_K].set(conv_w.astype(F32))
    const = lambda i: (0, 0)
    return pl.pallas_call(
        functools.partial(_delta_prep_kernel, n_single=n_single, tiles_per_seq=tiles_per_seq),
        grid=(n // TM,),
        in_specs=[pl.BlockSpec((TM, DQKV_W), lambda i: (i, cb)),
                  pl.BlockSpec((HALO, DQKV_W), lambda i: (jnp.maximum(i * hb - 1, 0), cb)),
                  pl.BlockSpec((HALO, DQKV_W), lambda i: (jnp.minimum((i + 1) * hb, nhb - 1), cb)),
                  pl.BlockSpec((8, DQKV_W), const),
                  pl.BlockSpec((TM, LANES), lambda i: (i, COL['dab'] // LANES)),
                  pl.BlockSpec((LANES, 4 * HW), const),
                  pl.BlockSpec((1, 2 * HW), const), pl.BlockSpec((1, 2 * HW), const),
                  pl.BlockSpec((2 * HW, 2 * HW), const)],
        out_specs=[pl.BlockSpec((TM, DQKV_W), lambda i: (i, 0)), pl.BlockSpec((TM, 4 * HW), lambda i: (i, 0))],
        out_shape=[jax.ShapeDtypeStruct((n, DQKV_W), F32), jax.ShapeDtypeStruct((n, 4 * HW), F32)],
        compiler_params=_cparams(("parallel",)),
        name="delta_prep",
    )(z, z, z, cw, z, jnp.asarray(ex),
      jnp.repeat(a_log.astype(F32).reshape(-1), HEAD_DIM).reshape(1, 2 * HW),
      jnp.repeat(dt_bias.astype(F32).reshape(-1), HEAD_DIM).reshape(1, 2 * HW), _seg_matrix(2 * HW))


def _delta_consts():
    r = np.arange(HW)
    h = r // HEAD_DIM
    t = r % HEAD_DIM
    same_h = h[:, None] == h[None, :]
    s16 = same_h & (t[:, None] // 16 == t[None, :] // 16)
    s32 = same_h & (t[:, None] // 32 == t[None, :] // 32)
    mats = [same_h, r[:, None] == r[None, :], s16, s32 & ~s16, same_h & ~s32]
    return jnp.asarray(np.stack(mats).astype(np.float32))


def _tri_inv(abds, eye, m16, m32, m64):
    bf = lambda x: x.astype(BF16)
    a16 = [bf(a * m16) for a in abds]
    tin = [eye - a * m16 for a in abds]
    p = _each(_dot, a16, a16)
    for step in range(3):
        tin = _each(lambda t, pp: t + _dot(bf(t), bf(pp)), tin, p)
        if step < 2:
            p = _each(lambda pp: _dot(bf(pp), bf(pp)), p)
    for m in (m32, m64):
        tb = _each(bf, tin)
        inner = _each(lambda a, t: bf(_dot(bf(a * m), t)), abds, tb)
        tin = _each(lambda t, t_b, i_n: t - _dot(t_b, i_n), tin, tb, inner)
    return tin


def _delta_chunks(ops_list, st_refs, consts_ref, revs):
    mask = consts_ref[0]
    mask_bf = mask.astype(BF16)
    bf = lambda x: x.astype(BF16)
    pts = [_scan_rows(r) for r in revs]
    pss = [_scan_lanes(r) for r in revs]
    qs = [o[0][:, :HW] for o in ops_list]
    ks = [o[0][:, HW:2 * HW] for o in ops_list]
    vs = [o[0][:, 2 * HW:] for o in ops_list]
    betas = [o[2] for o in ops_list]
    gcs = _each(_cumsum_scan, [o[1] for o in ops_list], pts, revs)
    t_idx = lax.broadcasted_iota(jnp.int32, (CHUNK, HW), 0)
    s_idx = lax.broadcasted_iota(jnp.int32, (CHUNK, HW), 1) % CHUNK
    diag = t_idx == s_idx
    gcl = [jnp.sum(jnp.where(diag, gc, 0.0), axis=0, keepdims=True) for gc in gcs]
    gam = _each(lambda gc, gl: jnp.exp(jnp.minimum(gc - gl, 0.0)), gcs, gcl)
    g_last = _each(lambda gc, r: _row(gc, CHUNK - 1, r), gcs, revs)
    egc = [jnp.exp(gc) for gc in gcs]
    kb = _each(lambda k, b: k * b, ks, betas)
    r = _each(lambda k_b, q, k: _dot_t(bf(jnp.concatenate([k_b, q], axis=0)), _bd(k, mask_bf)), kb, qs, ks)
    a = _each(lambda rr, gm, ps, pt: jnp.where(ps < pt, rr[:CHUNK] * gm, 0.0), r, gam, pss, pts)
    aq = _each(lambda rr, gm, ps, pt: jnp.where(ps <= pt, rr[CHUNK:] * gm, 0.0), r, gam, pss, pts)
    abd = [jnp.concatenate([x] * (HW // CHUNK), axis=0) * mask for x in a]
    tin = _tri_inv(abd, consts_ref[1], consts_ref[2], consts_ref[3], consts_ref[4])
    rhs = _each(lambda v, b, k_b, e: jnp.concatenate([_bd(v * b, mask_bf), _bd(k_b * e, mask_bf)], axis=1),
                vs, betas, kb, egc)
    uw_bd = _each(lambda t, rh: _dot(bf(t), rh), tin, rhs)
    uw = [x[0:CHUNK] + x[CHUNK:2 * CHUNK] + x[2 * CHUNK:3 * CHUNK] + x[3 * CHUNK:] for x in uw_bd]
    st = [s[...] for s in st_refs]
    ws_qs = _each(lambda x, q, e, s: _dot_t(bf(jnp.concatenate([x[:, HW:], q * e], axis=0)), bf(s)),
                  uw, qs, egc, st)
    v_new = _each(lambda x, y: x[:, :HW] - y[:CHUNK], uw, ws_qs)
    o = _each(lambda y, aa, vn: y[CHUNK:] + _dot(bf(aa), _bd(vn, mask_bf)), ws_qs, aq, v_new)
    ke = _each(lambda k, gl, gc: bf(k * jnp.exp(gl - gc)), ks, g_last, gcs)
    upd = _each(lambda vn, kk: lax.dot_general(bf(vn), kk, (((0,), (0,)), ((), ())), preferred_element_type=F32),
                v_new, ke)
    for s_ref, gl, s, u in zip(st_refs, g_last, st, upd):
        s_ref[...] = jnp.exp(gl) * s + mask * u
    return o


def _delta_chunks_fn(ops_list, consts, st_refs, directions):
    return _delta_chunks(ops_list, st_refs, consts[0], [d == 1 for d in directions])


def _delta_call(qkv, gb, row0, nb, t, s0_bd, group):
    ops_f = [(qkv, DQKV_W, 0), (gb, HW, 0), (gb, HW, 2)]
    ops_b = [(qkv, DQKV_W, 0), (gb, HW, 1), (gb, HW, 3)]
    return _scan_call("delta_scan", _delta_chunks_fn, ops_f, ops_b, [_delta_consts()], row0, nb, t, s0_bd, group)


def kernel(x_prompt, x_sample, cache_a_k, cache_a_v, cache_b_k, cache_b_v, state_hgrn, state_delta, c, c_ctx, w_mod, b_mod, norm1, norm2, w_in, w_gate, a_qn, a_kn, b_qn, b_kn, b_rpb, c_lb, c_norm, d_conv, d_alog, d_dtbias, d_norm, w_branch, w_out, w_router, b_router, w_gu, b_gu, w_down, b_down):
    nbp, tp, d = x_prompt.shape
    nbs, ts, _ = x_sample.shape
    depth = w_mod.shape[0]
    past = cache_a_k.shape[2]
    n_p = nbp * tp
    n = n_p + nbs * ts
    assert tp == TM and ts % TM == 0 and n_p % ts == 0 and 1 + nbs <= 8 and n % TM_MM == 0
    assert d == ROW_TILE * LANES
    npt, tps = n_p // TM, ts // TM
    grp_p = next(g for g in (4, 2, 1) if nbp % g == 0)
    grp_s = next(g for g in (2, 1) if nbs % g == 0)

    def cond_of_tile(i):
        return jnp.where(i < npt, 0, 1 + (i - npt) // tps)

    x = jnp.concatenate([x_prompt.reshape(n_p, d), x_sample.reshape(nbs * ts, d)], axis=0)
    conds = jnp.zeros((8, d), F32).at[0].set(c_ctx).at[1:1 + nbs].set(c)
    mod = _mod_call(conds, w_mod, b_mod)

    pl_ = jax.nn.softmax(c_lb.astype(F32), axis=0)
    lbs = jnp.cumsum(pl_, axis=0) - pl_[0:1]
    dqkv0 = COL['cg'] + HW
    da0 = dqkv0 + DQKV_W
    dg0 = da0 + 4 * D_HEADS
    w_in_p = jnp.concatenate([w_in[:, :, :dqkv0], w_in[:, :, dg0:dg0 + HW], w_in[:, :, dqkv0:da0],
                              w_in[:, :, da0:dg0],
                              jnp.zeros((depth, d, IN_W_PAD - dg0 - HW), w_in.dtype)], axis=2).astype(BF16)
    w_gate_b = w_gate.astype(BF16)
    w_branch_b = w_branch.astype(BF16)
    w_out_b = w_out.astype(BF16)
    w_router_p = jnp.zeros((depth, d, LANES), F32).at[:, :, :N_EXPERTS].set(w_router)
    b_router_p = jnp.full((depth, 1, LANES), NEG, F32).at[:, 0, :N_EXPERTS].set(b_router)

    tt = jnp.arange(ts, dtype=jnp.int32)
    half = HEAD_DIM // 2
    inv_freq = 1.0 / (ROPE_THETA ** (jnp.arange(0, half, 2, dtype=F32) / half))
    ang_r = (tt // GRID_W).astype(F32)[:, None] * inv_freq[None, :]
    ang_c = (tt % GRID_W).astype(F32)[:, None] * inv_freq[None, :]
    ang = jnp.concatenate([ang_r, ang_r, ang_c, ang_c], axis=-1)
    sign = jnp.asarray(np.where((np.arange(HEAD_DIM) % 32) < 16, -1.0, 1.0).astype(np.float32))
    cos_all = jnp.concatenate([jnp.ones((n_p, HEAD_DIM), F32), jnp.tile(jnp.cos(ang), (nbs, 1))], axis=0)
    sin_all = jnp.concatenate([jnp.zeros((n_p, HEAD_DIM), F32), jnp.tile(jnp.sin(ang) * sign, (nbs, 1))], axis=0)
    rope_tabs = (jnp.tile(cos_all, (1, 2)), jnp.tile(sin_all, (1, 2)))
    scale = HEAD_DIM ** -0.5
    wa, wb = A_KV_HEADS * HEAD_DIM, B_HEADS * HEAD_DIM

    ak_l, av_l, bk_l, bv_l, sc_l, sd_l = [], [], [], [], [], []
    for l in range(depth):
        mod3 = mod[l].reshape(8, 1, N_MOD * d)
        h = _norm_call(x, norm1[l].reshape(1, d), mod3, cond_of_tile)
        z = _mm_call(h, w_in_p, l)

        (qa,) = _prep_call(z, COL['aq'], A_HEADS * HEAD_DIM, a_qn[l], scale, rope_tabs, False)
        ka_n, ka = _prep_call(z, COL['ak'], wa, a_kn[l], 1.0, rope_tabs, True)
        oa_p = _attn_call(qa, ka, z, COL['av'], 0, nbp, tp, A_HEADS, A_KV_HEADS)
        oa_s = _attn_call(qa, ka, z, COL['av'], n_p, nbs, ts, A_HEADS, A_KV_HEADS,
                          ctx=(cache_a_k[:, l].reshape(nbs, past, wa).astype(BF16),
                               cache_a_v[:, l].reshape(nbs, past, wa).astype(BF16)))

        (qb,) = _prep_call(z, COL['bq'], wb, b_qn[l], scale, None, False)
        kb_n, kb = _prep_call(z, COL['bk'], wb, b_kn[l], 1.0, None, True)
        ob_p = _attn_call(qb, kb, z, COL['bv'], 0, nbp, tp, B_HEADS, B_HEADS)
        ob_s = _nbr_call(qb, kb, z, COL['bv'], n_p, nbs, ts,
                         cache_b_k[:, l].reshape(nbs, past, wb).astype(BF16),
                         cache_b_v[:, l].reshape(nbs, past, wb).astype(BF16), _nbr_bias_table(b_rpb[l]))

        cf_p, cb_p, sc_p = _hgrn_call(z, 0, nbp, tp, lbs[l], None, grp_p)
        cf_s, cb_s, _ = _hgrn_call(z, n_p, nbs, ts, lbs[l], _state_to_bd(state_hgrn[:, l]), grp_s)
        oc = _gate_norm_call(cf_p, cf_s, cb_p, cb_s, z, COL['cg'], c_norm[l], npt)

        qkv, gb = _delta_prep_call(z, d_conv[l], d_alog[l], d_dtbias[l], npt, tps)
        df_p, db_p, sd_p = _delta_call(qkv, gb, 0, nbp, tp, None, grp_p)
        df_s, db_s, _ = _delta_call(qkv, gb, n_p, nbs, ts, _state_to_bd(state_delta[:, l]), grp_s)
        od = _gate_norm_call(df_p, df_s, db_p, db_s, z, COL['dg'], d_norm[l], npt)

        x = _merge_call(x, h, oa_p, oa_s, ob_p, ob_s, oc, od, w_gate_b, w_branch_b, w_out_b, l, mod3, cond_of_tile,
                        npt)

        h_rows, gates, sel, counts = _router_call(x, norm2[l].reshape(1, d), mod3, cond_of_tile,
                                                  w_router_p, b_router_p, l)
        dest, block_e, n_used, next_e, last_blk, n_blocks = _slot_plan(sel, counts, n)
        x_slots = _dispatch_call(dest, last_blk, n_used, h_rows, n_blocks * MOE_TM)
        y_slots = _gmm_call(block_e, n_used, next_e, x_slots, w_gu, b_gu, w_down, b_down, l)
        x = _combine_call(dest, x, gates, mod3, cond_of_tile, y_slots)

        ak_l.append(ka_n[:n_p].reshape(nbp, tp, A_KV_HEADS, HEAD_DIM))
        av_l.append(z[:n_p, COL['av']:COL['av'] + wa].reshape(nbp, tp, A_KV_HEADS, HEAD_DIM))
        bk_l.append(kb_n[:n_p].reshape(nbp, tp, B_HEADS, HEAD_DIM))
        bv_l.append(z[:n_p, COL['bv']:COL['bv'] + wb].reshape(nbp, tp, B_HEADS, HEAD_DIM))
        sc_l.append(_state_from_bd(sc_p))
        sd_l.append(_state_from_bd(sd_p))

    return (x[:n_p].reshape(nbp, tp, d), x[n_p:].reshape(nbs, ts, d),
            jnp.stack(ak_l, axis=1), jnp.stack(av_l, axis=1), jnp.stack(bk_l, axis=1), jnp.stack(bv_l, axis=1),
            jnp.stack(sc_l, axis=1), jnp.stack(sd_l, axis=1))
```

```python
import functools

import numpy as np
import jax
import jax.numpy as jnp
from jax import lax
from jax.experimental import pallas as pl
from jax.experimental.pallas import tpu as pltpu

F32 = jnp.float32
BF16 = jnp.bfloat16
HIGHEST = lax.Precision.HIGHEST

GRID_W = 64
HEAD_DIM = 64
A_HEADS = 8
A_KV_HEADS = 2
B_HEADS = 4
NB_ROWS = 8
NB_COLS = 16
C_HEADS = 4
D_HEADS = 4
CONV_K = 5
CHUNK = 64
N_EXPERTS = 32
TOP_K = 4
SWIGLU_LIMIT = 7.0
SWIGLU_ALPHA = 1.702
ROPE_THETA = 10000.0
EPS = 1e-6
MAX_INPUT_GATE = 1.0 - 1e-6
N_MOD = 6
HW = 256

LANES = 128
VMEM_LIMIT = 56 * 1024 * 1024
NEG = -1e30

TM = 256
TM_MM = 512
TN_MM = 1024
TQ = 128
MOE_TM = 256

COL = dict(aq=0, ak=512, av=640, bq=768, bk=1024, bv=1280, cq=1536, ci=1792, cff=2048, cfb=2304,
           cg=2560, dg=2816, dq=3072, dk=3328, dv=3584, dab=3840)
IN_W_PAD = 4096


def _cparams(sem, vmem=VMEM_LIMIT):
    return pltpu.CompilerParams(dimension_semantics=sem, vmem_limit_bytes=vmem)


def _silu(x):
    return x * jax.nn.sigmoid(x)


def _dot(a, b):
    return jnp.dot(a, b, preferred_element_type=F32)


def _dot_t(a, b):
    return lax.dot_general(a, b, (((1,), (1,)), ((), ())), preferred_element_type=F32)


def _each(f, *lists):
    return [f(*a) for a in zip(*lists)]


def _mod_kernel(c_ref, w_ref, b_ref, o_ref):
    a = _silu(c_ref[...]).astype(BF16)
    o_ref[...] = _dot(a, w_ref[...].astype(BF16)) + b_ref[...]


def _mod_call(conds, w_mod, b_mod):
    depth, d, n = w_mod.shape
    tn = 1024
    return pl.pallas_call(
        _mod_kernel,
        grid=(depth, n // tn),
        in_specs=[pl.BlockSpec((8, d), lambda l, j: (0, 0)),
                  pl.BlockSpec((None, d, tn), lambda l, j: (l, 0, j)),
                  pl.BlockSpec((None, 1, tn), lambda l, j: (l, 0, j))],
        out_specs=pl.BlockSpec((None, 8, tn), lambda l, j: (l, 0, j)),
        out_shape=jax.ShapeDtypeStruct((depth, 8, n), F32),
        compiler_params=_cparams(("parallel", "parallel")),
        name="mod",
    )(conds, w_mod, b_mod.reshape(depth, 1, n))


def _mod_norm(x, g, m, k_shift, k_scale, d):
    y = x * lax.rsqrt(jnp.mean(x * x, axis=-1, keepdims=True) + EPS) * g
    return y * (1.0 + m[:, k_scale * d:(k_scale + 1) * d]) + m[:, k_shift * d:(k_shift + 1) * d]


def _norm_kernel(x_ref, g_ref, mod_ref, h_ref, *, d):
    h_ref[...] = _mod_norm(x_ref[...], g_ref[...], mod_ref[...], 0, 1, d).astype(BF16)


def _norm_call(x, g, mod3, cond_of_tile):
    n, d = x.shape
    return pl.pallas_call(
        functools.partial(_norm_kernel, d=d),
        grid=(n // TM,),
        in_specs=[pl.BlockSpec((TM, d), lambda i: (i, 0)),
                  pl.BlockSpec((1, d), lambda i: (0, 0)),
                  pl.BlockSpec((None, 1, N_MOD * d), lambda i: (cond_of_tile(i), 0, 0))],
        out_specs=pl.BlockSpec((TM, d), lambda i: (i, 0)),
        out_shape=jax.ShapeDtypeStruct((n, d), BF16),
        compiler_params=_cparams(("parallel",)),
        name="norm1",
    )(x, g, mod3)


def _mm_kernel(a_ref, w_ref, o_ref):
    o_ref[...] = _dot(a_ref[...], w_ref[...]).astype(o_ref.dtype)


def _mm_call(a, w, l, out_dtype=F32):
    m, k = a.shape
    n = w.shape[2]
    return pl.pallas_call(
        _mm_kernel,
        grid=(n // TN_MM, m // TM_MM),
        in_specs=[pl.BlockSpec((TM_MM, k), lambda j, i: (i, 0)),
                  pl.BlockSpec((None, k, TN_MM), lambda j, i: (l, 0, j))],
        out_specs=pl.BlockSpec((TM_MM, TN_MM), lambda j, i: (i, j)),
        out_shape=jax.ShapeDtypeStruct((m, n), out_dtype),
        compiler_params=_cparams(("parallel", "parallel")),
        name="in_proj",
    )(a, w)


def _two_path(i, n_first, first_ref, second_ref):
    return jnp.where(i < n_first, first_ref[...], second_ref[...])


def _two_path_specs(width, n_first):
    return [pl.BlockSpec((TM, width), lambda i: (jnp.minimum(i, n_first - 1), 0)),
            pl.BlockSpec((TM, width), lambda i: (jnp.maximum(i - n_first, 0), 0))]


def _merge_kernel(x_ref, h_ref, oap_ref, oas_ref, obp_ref, obs_ref, oc_ref, od_ref, wg_ref, wb_ref, wo_ref, mod_ref,
                  o_ref, *, d, n_first):
    i = pl.program_id(0)
    h = h_ref[...]
    outs = (_two_path(i, n_first, oap_ref, oas_ref), _two_path(i, n_first, obp_ref, obs_ref), oc_ref[...], od_ref[...])
    merged = None
    off = 0
    for j in range(4):
        wdt = outs[j].shape[1]
        gate = jax.nn.sigmoid(_dot(h, wg_ref[:, j * d:(j + 1) * d]))
        br = gate * _dot(outs[j], wb_ref[off:off + wdt, :])
        merged = br if merged is None else merged + br
        off += wdt
    y = _dot(merged.astype(BF16), wo_ref[...])
    o_ref[...] = x_ref[...] + mod_ref[...][:, 2 * d:3 * d] * y


def _merge_call(x, h, oa_p, oa_s, ob_p, ob_s, oc, od, w_gate, w_branch, w_out, l, mod3, cond_of_tile, n_first):
    n, d = x.shape
    row = lambda i: (i, 0)
    const = lambda i: (l, 0, 0)
    once = pl.Buffered(1)
    return pl.pallas_call(
        functools.partial(_merge_kernel, d=d, n_first=n_first),
        grid=(n // TM,),
        in_specs=[pl.BlockSpec((TM, d), row), pl.BlockSpec((TM, d), row)]
        + _two_path_specs(oa_p.shape[1], n_first) + _two_path_specs(ob_p.shape[1], n_first)
        + [pl.BlockSpec((TM, oc.shape[1]), row), pl.BlockSpec((TM, od.shape[1]), row),
           pl.BlockSpec((None,) + w_gate.shape[1:], const, pipeline_mode=once),
           pl.BlockSpec((None,) + w_branch.shape[1:], const, pipeline_mode=once),
           pl.BlockSpec((None,) + w_out.shape[1:], const, pipeline_mode=once),
           pl.BlockSpec((None, 1, N_MOD * d), lambda i: (cond_of_tile(i), 0, 0))],
        out_specs=pl.BlockSpec((TM, d), row),
        out_shape=jax.ShapeDtypeStruct((n, d), F32),
        compiler_params=_cparams(("parallel",)),
        name="merge",
    )(x, h, oa_p, oa_s, ob_p, ob_s, oc, od, w_gate, w_branch, w_out, mod3)


def _router_kernel(x_ref, g_ref, mod_ref, wr_ref, br_ref, ltri_ref, h_ref, gate_ref, sel_ref, cnt_ref, run_s, *, d):
    i = pl.program_id(0)

    @pl.when(i == 0)
    def _():
        run_s[...] = jnp.zeros(run_s.shape, F32)

    h2 = _mod_norm(x_ref[...], g_ref[...], mod_ref[...], 3, 4, d)
    nsub = d // LANES
    for sub in range(nsub):
        h_ref[pl.ds(sub, x_ref.shape[0], stride=nsub), :] = h2[:, sub * LANES:(sub + 1) * LANES]
    logits = jnp.dot(h2, wr_ref[...], preferred_element_type=F32, precision=HIGHEST) + br_ref[...]
    lane = lax.broadcasted_iota(jnp.int32, logits.shape, 1)
    vals, idxs = [], []
    cur = logits
    for _ in range(TOP_K):
        m = jnp.max(cur, axis=-1, keepdims=True)
        ix = jnp.min(jnp.where(cur == m, lane, LANES), axis=-1, keepdims=True)
        vals.append(m)
        idxs.append(ix)
        cur = jnp.where(lane == ix, -jnp.inf, cur)
    es = [jnp.exp(v - vals[0]) for v in vals]
    tot = es[0] + es[1] + es[2] + es[3]
    gates = jnp.zeros(logits.shape, F32)
    sel = jnp.zeros(logits.shape, jnp.int32)
    base = run_s[...]
    ltri = ltri_ref[...]
    for k in range(TOP_K):
        onehot = jnp.where(lane == idxs[k], 1.0, 0.0)
        before = base + _dot(ltri, onehot.astype(BF16))
        rank = jnp.sum(onehot * before, axis=-1, keepdims=True).astype(jnp.int32)
        base = base + jnp.sum(onehot, axis=0, keepdims=True)
        gates = jnp.where(lane == k, es[k] / tot, gates)
        sel = jnp.where(lane == k, idxs[k], sel)
        sel = jnp.where(lane == TOP_K + k, rank, sel)
    run_s[...] = base
    gate_ref[...] = gates
    sel_ref[...] = sel
    cnt_ref[...] = base


def _router_call(x, g, mod3, cond_of_tile, w_router_p, b_router_p, l):
    n, d = x.shape
    row = lambda i: (i, 0)
    const = lambda i: (0, 0)
    layer = lambda i: (l, 0, 0)
    ltri = jnp.asarray(np.tril(np.ones((TM, TM), np.float32), -1)).astype(BF16)
    return pl.pallas_call(
        functools.partial(_router_kernel, d=d),
        grid=(n // TM,),
        in_specs=[pl.BlockSpec((TM, d), row),
                  pl.BlockSpec((1, d), const),
                  pl.BlockSpec((None, 1, N_MOD * d), lambda i: (cond_of_tile(i), 0, 0)),
                  pl.BlockSpec((None, d, LANES), layer),
                  pl.BlockSpec((None, 1, LANES), layer),
                  pl.BlockSpec((TM, TM), const)],
        out_specs=[pl.BlockSpec((TM * (d // LANES), LANES), row), pl.BlockSpec((TM, LANES), row),
                   pl.BlockSpec((TM, LANES), row), pl.BlockSpec((1, LANES), const)],
        out_shape=[jax.ShapeDtypeStruct((n * (d // LANES), LANES), F32), jax.ShapeDtypeStruct((n, LANES), F32),
                   jax.ShapeDtypeStruct((n, LANES), jnp.int32), jax.ShapeDtypeStruct((1, LANES), F32)],
        scratch_shapes=[pltpu.VMEM((1, LANES), F32)],
        compiler_params=_cparams(("arbitrary",)),
        name="router",
    )(x, g, mod3, w_router_p, b_router_p, ltri)


def _slot_plan(sel, counts, n_tok):
    cnt = counts[0, :N_EXPERTS].astype(jnp.int32)
    padded = (cnt + MOE_TM - 1) // MOE_TM * MOE_TM
    pad_end = jnp.cumsum(padded)
    pad_start = pad_end - padded
    dest = (pad_start[sel[:, :TOP_K]] + sel[:, TOP_K:2 * TOP_K]).reshape(n_tok * TOP_K)
    n_blocks = n_tok * TOP_K // MOE_TM + N_EXPERTS
    blk = jnp.arange(n_blocks, dtype=jnp.int32)
    block_e = jnp.minimum(jnp.sum((pad_end[None, :] <= (blk * MOE_TM)[:, None]).astype(jnp.int32), axis=1),
                          N_EXPERTS - 1)
    n_used = (pad_end[-1:] // MOE_TM).astype(jnp.int32)
    later = (blk[None, :] > blk[:, None]) & (block_e[None, :] != block_e[:, None]) & (blk[None, :] < n_used[0])
    first_later = jnp.argmax(later, axis=1)
    next_e = jnp.where(jnp.any(later, axis=1), block_e[first_later], -1).astype(jnp.int32)
    last_blk = jnp.where(padded > 0, pad_end // MOE_TM - 1, -1).astype(jnp.int32)
    return dest, block_e, n_used, next_e, last_blk, n_blocks


ROW_TILE = 8


def _row_slice(ref, r):
    return ref.at[pl.ds(pl.multiple_of(r * ROW_TILE, ROW_TILE), ROW_TILE)]


def _dispatch_kernel(dest_ref, last_ref, nu_ref, h_ref, o_ref, zero_s, sem, zsem):
    rows = h_ref.shape[0]
    tm = rows // ROW_TILE
    grp = zero_s.shape[0]
    n_groups = o_ref.shape[0] // grp
    base = pl.program_id(0) * (tm * TOP_K)

    @pl.when(pl.program_id(0) == 0)
    def _():
        zero_s[...] = jnp.zeros(zero_s.shape, zero_s.dtype)

        def fill(group):
            return pltpu.make_async_copy(zero_s, o_ref.at[pl.ds(pl.multiple_of(group * grp, grp), grp)], zsem)

        targets = [(last_ref[e] >= 0, last_ref[e]) for e in range(N_EXPERTS)]
        targets += [(nu_ref[0] + j < n_groups, nu_ref[0] + j) for j in range(N_EXPERTS)]
        for do, group in targets:
            @pl.when(do)
            def _():
                fill(group).start()
        for do, group in targets:
            @pl.when(do)
            def _():
                fill(group).wait()

    def body(t, carry):
        for k in range(TOP_K):
            pltpu.make_async_copy(_row_slice(h_ref, t), _row_slice(o_ref, dest_ref[base + t * TOP_K + k]),
                                  sem).start()
        return carry

    lax.fori_loop(0, tm, body, 0)
    for _ in range(TOP_K):
        pltpu.make_async_copy(h_ref, o_ref.at[pl.ds(0, rows)], sem).wait()


def _dispatch_call(dest, last_blk, n_used, h_rows, n_slots):
    n = h_rows.shape[0] // ROW_TILE
    grid_spec = pltpu.PrefetchScalarGridSpec(
        num_scalar_prefetch=3,
        grid=(n // TM,),
        in_specs=[pl.BlockSpec((TM * ROW_TILE, LANES), lambda i, dst, lst, nu: (i, 0))],
        out_specs=pl.BlockSpec(memory_space=pl.ANY),
        scratch_shapes=[pltpu.VMEM((MOE_TM * ROW_TILE, LANES), F32), pltpu.SemaphoreType.DMA(()),
                        pltpu.SemaphoreType.DMA(())],
    )
    return pl.pallas_call(
        _dispatch_kernel,
        grid_spec=grid_spec,
        out_shape=jax.ShapeDtypeStruct((n_slots * ROW_TILE, LANES), F32),
        compiler_params=_cparams(("arbitrary",)),
        name="dispatch",
    )(dest, last_blk, n_used, h_rows)


def _gmm_kernel(be_ref, nu_ref, nxt_ref, x_ref, wgu_hbm, bgu_ref, wd_hbm, bd_ref, y_ref, wgu_f, wd_f, wgu_s, wd_s,
                sems, *, f, layer):
    i = pl.program_id(0)
    used = i < nu_ref[0]
    changed = (i == 0) | (be_ref[i] != be_ref[jnp.maximum(i - 1, 0)])
    tm = x_ref.shape[0] // ROW_TILE

    def weight_copies(e):
        return (pltpu.make_async_copy(wgu_hbm.at[layer, e], wgu_f, sems.at[0]),
                pltpu.make_async_copy(wd_hbm.at[layer, e], wd_f, sems.at[1]))

    @pl.when(used & (i == 0))
    def _():
        for cp in weight_copies(be_ref[0]):
            cp.start()

    @pl.when(used & changed)
    def _():
        for cp in weight_copies(be_ref[i]):
            cp.wait()
        wgu_s[...] = wgu_f[...].astype(BF16)
        wd_s[...] = wd_f[...].astype(BF16)

        @pl.when(nxt_ref[i] >= 0)
        def _():
            for cp in weight_copies(nxt_ref[i]):
                cp.start()

    @pl.when(used)
    def _():
        x = jnp.concatenate([x_ref[pl.ds(sub, tm, stride=ROW_TILE), :] for sub in range(ROW_TILE)], axis=1)
        gu = _dot(x.astype(BF16), wgu_s[...]) + bgu_ref[...]
        gate = jnp.minimum(gu[:, :f], SWIGLU_LIMIT)
        up = jnp.clip(gu[:, f:], -SWIGLU_LIMIT, SWIGLU_LIMIT)
        act = gate * jax.nn.sigmoid(SWIGLU_ALPHA * gate) * (up + 1.0)
        y = _dot(act.astype(BF16), wd_s[...]) + bd_ref[...]
        for sub in range(ROW_TILE):
            y_ref[pl.ds(sub, tm, stride=ROW_TILE), :] = y[:, sub * LANES:(sub + 1) * LANES]

    @pl.when(jnp.logical_not(used))
    def _():
        y_ref[...] = jnp.zeros(y_ref.shape, y_ref.dtype)


def _gmm_call(block_e, n_used, next_e, x_slots, w_gu, b_gu, w_down, b_down, l):
    n_slots = x_slots.shape[0] // ROW_TILE
    depth, e, d, f2 = w_gu.shape
    f = f2 // 2
    n_blocks = n_slots // MOE_TM
    xrow = lambda i, be, nu, nx: (jnp.minimum(i, nu[0] - 1), 0)
    yrow = lambda i, be, nu, nx: (i, 0)
    bsel = lambda i, be, nu, nx: (l, be[i], 0, 0)
    grid_spec = pltpu.PrefetchScalarGridSpec(
        num_scalar_prefetch=3,
        grid=(n_blocks,),
        in_specs=[pl.BlockSpec((MOE_TM * ROW_TILE, LANES), xrow),
                  pl.BlockSpec(memory_space=pl.ANY),
                  pl.BlockSpec((None, None, 1, f2), bsel),
                  pl.BlockSpec(memory_space=pl.ANY),
                  pl.BlockSpec((None, None, 1, d), bsel)],
        out_specs=pl.BlockSpec((MOE_TM * ROW_TILE, LANES), yrow),
        scratch_shapes=[pltpu.VMEM((d, f2), F32), pltpu.VMEM((f, d), F32), pltpu.VMEM((d, f2), BF16),
                        pltpu.VMEM((f, d), BF16), pltpu.SemaphoreType.DMA((2,))],
    )
    return pl.pallas_call(
        functools.partial(_gmm_kernel, f=f, layer=l),
        grid_spec=grid_spec,
        out_shape=jax.ShapeDtypeStruct((n_slots * ROW_TILE, LANES), F32),
        compiler_params=_cparams(("arbitrary",)),
        name="experts",
    )(block_e, n_used, next_e, x_slots, w_gu, b_gu.reshape(depth, e, 1, f2), w_down, b_down.reshape(depth, e, 1, d))


def _combine_kernel(dest_ref, x_ref, gate_ref, mod_ref, y_ref, o_ref, buf, sems, *, d):
    tm = x_ref.shape[0]
    rows = tm * ROW_TILE
    i = pl.program_id(0)
    n_steps = pl.num_programs(0)

    def issue(step, slot):
        base = step * (tm * TOP_K)

        def body(t, carry):
            for k in range(TOP_K):
                dst = buf.at[pl.ds(pl.multiple_of((slot * TOP_K + k) * rows + t * ROW_TILE, ROW_TILE), ROW_TILE)]
                pltpu.make_async_copy(_row_slice(y_ref, dest_ref[base + t * TOP_K + k]), dst, sems.at[slot]).start()
            return carry

        lax.fori_loop(0, tm, body, 0)

    @pl.when(i == 0)
    def _():
        issue(0, 0)

    @pl.when(i + 1 < n_steps)
    def _():
        issue(i + 1, (i + 1) % 2)

    slot = i % 2
    for k in range(TOP_K):
        off = pl.multiple_of((slot * TOP_K + k) * rows, ROW_TILE)
        pltpu.make_async_copy(y_ref.at[pl.ds(0, rows)], buf.at[pl.ds(off, rows)], sems.at[slot]).wait()
    g = gate_ref[...]
    gk = [jnp.broadcast_to(g[:, k:k + 1], (tm, LANES)) for k in range(TOP_K)]
    g2 = mod_ref[...][:, 5 * d:6 * d]
    for sub in range(ROW_TILE):
        sl = slice(sub * LANES, (sub + 1) * LANES)
        acc = None
        for k in range(TOP_K):
            off = pl.multiple_of((slot * TOP_K + k) * rows, ROW_TILE)
            term = gk[k] * buf[pl.ds(off + sub, tm, stride=ROW_TILE), :]
            acc = term if acc is None else acc + term
        o_ref[:, sl] = x_ref[:, sl] + g2[:, sl] * acc


def _combine_call(dest, x, gates, mod3, cond_of_tile, y_slots):
    n, d = x.shape
    row = lambda i, dst: (i, 0)
    grid_spec = pltpu.PrefetchScalarGridSpec(
        num_scalar_prefetch=1,
        grid=(n // TM,),
        in_specs=[pl.BlockSpec((TM, d), row),
                  pl.BlockSpec((TM, LANES), row),
                  pl.BlockSpec((None, 1, N_MOD * d), lambda i, dst: (cond_of_tile(i), 0, 0)),
                  pl.BlockSpec(memory_space=pl.ANY)],
        out_specs=pl.BlockSpec((TM, d), row),
        scratch_shapes=[pltpu.VMEM((2 * TOP_K * TM * ROW_TILE, LANES), F32), pltpu.SemaphoreType.DMA((2,))],
    )
    return pl.pallas_call(
        functools.partial(_combine_kernel, d=d),
        grid_spec=grid_spec,
        out_shape=jax.ShapeDtypeStruct((n, d), F32),
        compiler_params=_cparams(("arbitrary",)),
        name="combine",
    )(dest, x, gates, mod3, y_slots)


def _seg_matrix(w):
    i = np.arange(w) // HEAD_DIM
    return jnp.asarray((i[:, None] == i[None, :]).astype(np.float32)).astype(BF16)


def _seg_sum(y, p):
    hi = y.astype(BF16)
    lo = (y - hi.astype(F32)).astype(BF16)
    return _dot(hi, p) + _dot(lo, p)


def _head_mean_sq(x, p):
    return _seg_sum(x * x, p) * (1.0 / HEAD_DIM)


def _prep_kernel(z_ref, nw_ref, p_ref, *rest, scale, rope, want_norm):
    if rope:
        cos_ref, sin_ref = rest[0], rest[1]
        rest = rest[2:]
    x = z_ref[...]
    w = x.shape[1]
    y = x * lax.rsqrt(_head_mean_sq(x, p_ref[...]) + EPS) * nw_ref[...]
    if want_norm:
        rest[0][...] = y
    out_ref = rest[-1]
    if rope:
        reps = w // LANES
        cos = jnp.concatenate([cos_ref[...]] * reps, axis=1) if reps > 1 else cos_ref[...]
        sin = jnp.concatenate([sin_ref[...]] * reps, axis=1) if reps > 1 else sin_ref[...]
        lane = lax.broadcasted_iota(jnp.int32, y.shape, 1)
        first = (lane % 32) < 16
        rot = jnp.where(first, pltpu.roll(y, w - 16, axis=1), pltpu.roll(y, 16, axis=1))
        y = y * cos + rot * sin
    out_ref[...] = (y * scale).astype(BF16)


def _prep_call(z, col, w, nw, scale, rope_tabs, want_norm):
    n = z.shape[0]
    cb = col // w
    row = lambda i: (i, 0)
    in_specs = [pl.BlockSpec((TM, w), lambda i: (i, cb)),
                pl.BlockSpec((1, w), lambda i: (0, 0)),
                pl.BlockSpec((w, w), lambda i: (0, 0))]
    args = [z, jnp.tile(nw, w // HEAD_DIM).reshape(1, w), _seg_matrix(w)]
    if rope_tabs is not None:
        in_specs += [pl.BlockSpec((TM, LANES), row), pl.BlockSpec((TM, LANES), row)]
        args += list(rope_tabs)
    out_specs, out_shape = [], []
    if want_norm:
        out_specs.append(pl.BlockSpec((TM, w), row))
        out_shape.append(jax.ShapeDtypeStruct((n, w), F32))
    out_specs.append(pl.BlockSpec((TM, w), row))
    out_shape.append(jax.ShapeDtypeStruct((n, w), BF16))
    return pl.pallas_call(
        functools.partial(_prep_kernel, scale=scale, rope=rope_tabs is not None, want_norm=want_norm),
        grid=(n // TM,),
        in_specs=in_specs, out_specs=out_specs, out_shape=out_shape,
        compiler_params=_cparams(("parallel",)),
        name="qk_prep",
    )(*args)


KEY_BLOCK = 512
LOG2E = 1.4426950408889634


def _ones_extended(v, heads):
    vb = v.astype(BF16).reshape(v.shape[:-1] + (heads, HEAD_DIM))
    return jnp.concatenate([vb, jnp.ones_like(vb)], axis=-1).reshape(v.shape[:-1] + (heads * 2 * HEAD_DIM,))


def _attn_kernel(q_ref, k_ref, v_ref, *rest, hq, hk, has_ctx, base2):
    if has_ctx:
        ck_ref, cv_ref, o_ref = rest
    else:
        (o_ref,) = rest
    g = hq // hk
    tq = q_ref.shape[0]
    t = k_ref.shape[0]
    dh = HEAD_DIM
    kb = min(KEY_BLOCK, t)
    ex = jnp.exp2 if base2 else jnp.exp
    blocks = [(ck_ref, cv_ref, 0, ck_ref.shape[0])] if has_ctx else []
    blocks += [(k_ref, v_ref, s0, kb) for s0 in range(0, t, kb)]
    outs = []
    for j in range(hk):
        q4 = jnp.concatenate([q_ref[:, (j * g + gi) * dh:(j * g + gi + 1) * dh] for gi in range(g)], axis=0)
        m = acc = None
        for kref, vref, s0, n in blocks:
            s = _dot_t(q4, kref[s0:s0 + n, j * dh:(j + 1) * dh])
            bm = jnp.max(s, axis=-1, keepdims=True)
            m_new = bm if m is None else jnp.maximum(m, bm)
            pv = _dot(ex(s - m_new).astype(BF16), vref[s0:s0 + n, 2 * j * dh:2 * (j + 1) * dh])
            acc = pv if m is None else ex(m - m_new) * acc + pv
            m = m_new
        o = acc[:, :dh] / acc[:, dh:]
        outs += [o[gi * tq:(gi + 1) * tq] for gi in range(g)]
    o_ref[...] = jnp.concatenate(outs, axis=1).astype(o_ref.dtype)


def _attn_call(q, k, v_ext, row0, nb, t, hq, hk, base2, ctx=None):
    tq = min(TQ, t)
    wq, wk = hq * HEAD_DIM, hk * HEAD_DIM
    qb0, kb0 = row0 // tq, row0 // t
    nq = t // tq
    in_specs = [pl.BlockSpec((tq, wq), lambda b, i: (qb0 + b * nq + i, 0)),
                pl.BlockSpec((t, wk), lambda b, i: (kb0 + b, 0)),
                pl.BlockSpec((t, 2 * wk), lambda b, i: (kb0 + b, 0))]
    args = [q, k, v_ext]
    if ctx is not None:
        p = ctx[0].shape[1]
        in_specs += [pl.BlockSpec((None, p, wk), lambda b, i: (b, 0, 0)),
                     pl.BlockSpec((None, p, 2 * wk), lambda b, i: (b, 0, 0))]
        args += list(ctx)
    return pl.pallas_call(
        functools.partial(_attn_kernel, hq=hq, hk=hk, has_ctx=ctx is not None, base2=base2),
        grid=(nb, nq),
        in_specs=in_specs,
        out_specs=pl.BlockSpec((tq, wq), lambda b, i: (b * nq + i, 0)),
        out_shape=jax.ShapeDtypeStruct((nb * t, wq), BF16),
        compiler_params=_cparams(("parallel", "parallel")),
        name="attn",
    )(*args)


def _nbr_bias_table(rpb):
    o = np.arange(NB_ROWS)[:, None, None, None]
    cc = np.arange(GRID_W)[None, :, None, None]
    i = np.arange(NB_ROWS)[None, None, :, None]
    j = np.arange(GRID_W)[None, None, None, :]
    col_start = np.clip(cc - NB_COLS // 2, 0, GRID_W - NB_COLS)
    valid = (j >= col_start) & (j < col_start + NB_COLS)
    sel_r = ((i - o + NB_ROWS - 1)[..., None] == np.arange(2 * NB_ROWS - 1)).astype(np.float32)[:, 0, :, 0]
    sel_c = (((j - cc + NB_COLS - 1)[..., None] == np.arange(2 * NB_COLS - 1)) & valid[..., None])
    sel_c = sel_c.astype(np.float32)[0, :, 0]
    tab = jnp.einsum('hrd,oir,cjd->hocij', rpb.astype(F32), jnp.asarray(sel_r), jnp.asarray(sel_c),
                     precision=HIGHEST)
    tab = tab + jnp.asarray(np.where(valid, 0.0, NEG).astype(np.float32))[None, :, :, :, :]
    return tab.reshape(rpb.shape[0], NB_ROWS, GRID_W, NB_ROWS * GRID_W)


def _nbr_kernel(q_ref, k_ref, v_ref, ck_ref, cv_ref, bias_ref, o_ref, *, rows):
    r = pl.program_id(1)
    rs = jnp.clip(r - NB_ROWS // 2, 0, rows - NB_ROWS)
    start = pl.multiple_of(rs * GRID_W, GRID_W)
    kw = k_ref[pl.ds(start, NB_ROWS * GRID_W), :]
    vw = v_ref[pl.ds(start, NB_ROWS * GRID_W), :].astype(BF16)
    dh = HEAD_DIM
    sls = [slice(h * dh, (h + 1) * dh) for h in range(B_HEADS)]
    rowmax = lambda x: jnp.max(x, axis=-1, keepdims=True)
    rowsum = lambda x: jnp.sum(x, axis=-1, keepdims=True)
    qs = [q_ref[:, sl] for sl in sls]
    s_loc = [_dot_t(q, kw[:, sl]) + bias_ref[h] for h, (q, sl) in enumerate(zip(qs, sls))]
    s_ctx = [_dot_t(q, ck_ref[:, sl]) for q, sl in zip(qs, sls)]
    m = _each(lambda a, b: jnp.maximum(rowmax(a), rowmax(b)), s_loc, s_ctx)
    p_loc = _each(lambda a, mm: jnp.exp(a - mm), s_loc, m)
    p_ctx = _each(lambda a, mm: jnp.exp(a - mm), s_ctx, m)
    l = _each(lambda a, b: rowsum(a) + rowsum(b), p_loc, p_ctx)
    o = [_dot(a.astype(BF16), vw[:, sl]) + _dot(b.astype(BF16), cv_ref[:, sl]) for a, b, sl in zip(p_loc, p_ctx, sls)]
    o_ref[...] = jnp.concatenate(_each(lambda a, b: a / b, o, l), axis=1).astype(o_ref.dtype)


def _nbr_call(q, k, z, vcol, row0, nb, t, ck, cv, bias):
    rows = t // GRID_W
    assert rows >= NB_ROWS
    w = B_HEADS * HEAD_DIM
    p = ck.shape[1]
    nwin = NB_ROWS * GRID_W

    def bias_map(b, r):
        return (0, r - jnp.clip(r - NB_ROWS // 2, 0, rows - NB_ROWS), 0, 0)

    return pl.pallas_call(
        functools.partial(_nbr_kernel, rows=rows),
        grid=(nb, rows),
        in_specs=[pl.BlockSpec((GRID_W, w), lambda b, r: (row0 // GRID_W + b * rows + r, 0)),
                  pl.BlockSpec((t, w), lambda b, r: (row0 // t + b, 0)),
                  pl.BlockSpec((t, w), lambda b, r: (row0 // t + b, vcol // w)),
                  pl.BlockSpec((None, p, w), lambda b, r: (b, 0, 0)),
                  pl.BlockSpec((None, p, w), lambda b, r: (b, 0, 0)),
                  pl.BlockSpec((B_HEADS, None, GRID_W, nwin), bias_map)],
        out_specs=pl.BlockSpec((GRID_W, w), lambda b, r: (b * rows + r, 0)),
        out_shape=jax.ShapeDtypeStruct((nb * t, w), BF16),
        compiler_params=_cparams(("parallel", "arbitrary")),
        name="nbr_attn",
    )(q, k, z, ck, cv, bias)


def _head_mask():
    i = np.arange(HW) // HEAD_DIM
    return jnp.asarray((i[:, None] == i[None, :]).astype(np.float32))


def _scan_rows(rev):
    t = lax.broadcasted_iota(jnp.int32, (CHUNK, HW), 0)
    return (CHUNK - 1 - t) if rev else t


def _scan_lanes(rev):
    s = lax.broadcasted_iota(jnp.int32, (CHUNK, HW), 1) % CHUNK
    return (CHUNK - 1 - s) if rev else s


def _sh(x, d, rev):
    s = (-d if rev else d) % CHUNK
    return x if s == 0 else pltpu.roll(x, s, axis=0)


def _cumsum_scan(x, pt, rev):
    d = 1
    while d < CHUNK:
        x = x + jnp.where(pt >= d, _sh(x, d, rev), 0.0)
        d *= 2
    return x


def _block_end(x, pt, n, rev):
    r = (n - 1) - pt % n
    bit = 1
    while bit < n:
        x = jnp.where((r & bit) != 0, _sh(x, -bit, rev), x)
        bit *= 2
    return x


def _pick16(x, j, rev):
    jj = (15 - j) if rev else j
    x3 = x.reshape(CHUNK // 16, 16, HW)
    return jnp.broadcast_to(x3[:, jj:jj + 1, :], x3.shape).reshape(CHUNK, HW)


def _row(x, p, rev):
    t = (CHUNK - 1 - p) if rev else p
    return x[t:t + 1, :]


def _bd(x, mask_bf):
    xb = x.astype(BF16)
    return jnp.concatenate([xb] * (HW // CHUNK), axis=0) * mask_bf


def _gla_chunks(ops_list, lbs, st_refs, mask, revs):
    mask_bf = mask.astype(BF16)
    bf = lambda x: x.astype(BF16)

    def prepare(ops, lb, rev):
        q_raw, v, f_raw = ops
        pt = _scan_rows(rev)
        q = _silu(q_raw)
        kk = jnp.minimum((1.0 - lb) * jax.nn.sigmoid(-f_raw), MAX_INPUT_GATE)
        b = _cumsum_scan(jnp.log1p(-kk), pt, rev)
        e4 = _block_end(b, pt, 4, rev)
        r4 = jnp.where(pt >= 4, _sh(e4, 4, rev), 0.0)
        r16 = jnp.where(pt >= 16, _sh(_pick16(b, 15, rev), 16, rev), 0.0)
        ks01 = [_bd(kk * jnp.exp(r4 - b), mask_bf)]
        ks2 = []
        for i in range(1, 4):
            ks01.append(_bd(kk * jnp.exp(jnp.minimum(_pick16(r4, 4 * i, rev) - b, 0.0)), mask_bf))
            ks2.append(_bd(kk * jnp.exp(jnp.minimum(_row(b, 16 * i - 1, rev) - b, 0.0)), mask_bf))
        return dict(pt=pt, ps=_scan_lanes(rev), v=v, kk=kk, b=b, b_last=_row(b, CHUNK - 1, rev),
                    q0=bf(q * jnp.exp(b - r4)), q2=bf(q * jnp.exp(b - r16)), qe=bf(q * jnp.exp(b)),
                    k01=jnp.concatenate(ks01, axis=0), k2=jnp.concatenate(ks2, axis=0))

    def select(c, r01, r2):
        pt, ps = c['pt'], c['ps']
        pt4, ps4, pt16, ps16 = pt // 4, ps // 4, pt // 16, ps // 16
        attn = jnp.where((pt4 == ps4) & (ps <= pt), r01[:, :HW], 0.0)
        for i in range(1, 4):
            attn = jnp.where((pt16 == ps16) & (ps4 < pt4) & (pt4 % 4 == i), r01[:, i * HW:(i + 1) * HW], attn)
            attn = jnp.where((ps16 < pt16) & (pt16 == i), r2[:, (i - 1) * HW:i * HW], attn)
        return bf(attn)

    cs = _each(prepare, ops_list, lbs, revs)
    r01 = [_dot_t(c['q0'], c['k01']) for c in cs]
    r2 = [_dot_t(c['q2'], c['k2']) for c in cs]
    attn = _each(select, cs, r01, r2)
    st = [s[...] for s in st_refs]
    o = _each(lambda c, s, aa: _dot_t(c['qe'], bf(s)) + _dot(aa, _bd(c['v'], mask_bf)), cs, st, attn)
    upd = [lax.dot_general(bf(c['v']), bf(c['kk'] * jnp.exp(c['b_last'] - c['b'])), (((0,), (0,)), ((), ())),
                           preferred_element_type=F32) for c in cs]
    for s_ref, c, s, u in zip(st_refs, cs, st, upd):
        s_ref[...] = jnp.exp(c['b_last']) * s + mask * u
    return o


def _scan_kernel(*refs, chunk_fn, n_in, n_const, group, has_state):
    n_op = group * 2 * n_in
    ins, consts, rest = refs[:n_op], refs[n_op:n_op + n_const], refs[n_op + n_const:]
    if has_state:
        s0_ref, of_ref, ob_ref, sout_ref, st_s = rest
    else:
        of_ref, ob_ref, sout_ref, st_s = rest
    c = pl.program_id(1)

    @pl.when(c == 0)
    def _():
        st_s[...] = s0_ref[...] if has_state else jnp.zeros(st_s.shape, F32)

    chains = [(g, direction) for g in range(group) for direction in (0, 1)]
    ops_list = [[r[...] for r in ins[(g * 2 + dr) * n_in:(g * 2 + dr + 1) * n_in]] for g, dr in chains]
    outs = chunk_fn(ops_list, consts, [st_s.at[g, dr] for g, dr in chains], [dr for _, dr in chains])
    for (g, dr), o in zip(chains, outs):
        (ob_ref if dr else of_ref)[g] = o

    @pl.when(c == pl.num_programs(1) - 1)
    def _():
        sout_ref[...] = st_s[...]


def _scan_call(name, chunk_fn, ops_f, ops_b, consts, row0, nb, t, s0_bd, group):
    assert nb % group == 0 and len(ops_f) == len(ops_b)
    nc = t // CHUNK
    r0 = row0 // CHUNK
    has_state = s0_bd is not None
    in_specs, args = [], []
    for g in range(group):
        for direction, ops in ((0, ops_f), (1, ops_b)):
            for arr, w, cb in ops:
                if direction == 0:
                    imap = functools.partial(lambda b, c, g, cb: (r0 + (b * group + g) * nc + c, cb), g=g, cb=cb)
                else:
                    imap = functools.partial(lambda b, c, g, cb: (r0 + (b * group + g) * nc + nc - 1 - c, cb),
                                             g=g, cb=cb)
                in_specs.append(pl.BlockSpec((CHUNK, w), imap))
                args.append(arr)
    for cst in consts:
        in_specs.append(pl.BlockSpec(cst.shape, functools.partial(lambda b, c, nd: (0,) * nd, nd=cst.ndim)))
        args.append(cst)
    state_spec = pl.BlockSpec((group, 2, HW, HW), lambda b, c: (b, 0, 0, 0))
    if has_state:
        in_specs.append(state_spec)
        args.append(s0_bd)
    o_f, o_b, s_out = pl.pallas_call(
        functools.partial(_scan_kernel, chunk_fn=chunk_fn, n_in=len(ops_f), n_const=len(consts), group=group,
                          has_state=has_state),
        grid=(nb // group, nc),
        in_specs=in_specs,
        out_specs=[pl.BlockSpec((group, CHUNK, HW), lambda b, c: (b, c, 0)),
                   pl.BlockSpec((group, CHUNK, HW), lambda b, c: (b, nc - 1 - c, 0)),
                   state_spec],
        out_shape=[jax.ShapeDtypeStruct((nb, t, HW), F32), jax.ShapeDtypeStruct((nb, t, HW), F32),
                   jax.ShapeDtypeStruct((nb, 2, HW, HW), F32)],
        scratch_shapes=[pltpu.VMEM((group, 2, HW, HW), F32)],
        compiler_params=_cparams(("parallel", "arbitrary")),
        name=name,
    )(*args)
    return o_f.reshape(nb * t, HW), o_b.reshape(nb * t, HW), s_out


def _hgrn_chunks_fn(ops_list, consts, st_refs, directions):
    lb_ref, mask_ref = consts
    return _gla_chunks(ops_list, [lb_ref[dr:dr + 1, :] for dr in directions], st_refs, mask_ref[...],
                       [dr == 1 for dr in directions])


def _state_to_bd(s):
    nb = s.shape[0]
    eye = jnp.eye(C_HEADS, dtype=F32)
    return jnp.einsum('bdhkv,hg->bdhvgk', s.astype(F32), eye).reshape(nb, 2, HW, HW)


def _state_from_bd(sbd):
    nb = sbd.shape[0]
    s6 = sbd.reshape(nb, 2, C_HEADS, HEAD_DIM, C_HEADS, HEAD_DIM)
    return jnp.stack([s6[:, :, h, :, h, :] for h in range(C_HEADS)], axis=2).transpose(0, 1, 2, 4, 3)


def _hgrn_call(z, row0, nb, t, lb, s0_bd, group):
    ops = lambda fcol: [(z, HW, COL['cq'] // HW), (z, HW, COL['ci'] // HW), (z, HW, fcol // HW)]
    return _scan_call("hgrn_scan", _hgrn_chunks_fn, ops(COL['cff']), ops(COL['cfb']), [lb, _head_mask()],
                      row0, nb, t, s0_bd, group)


def _gate_norm_kernel(ofp_ref, ofs_ref, obp_ref, obs_ref, g_ref, nw_ref, p_ref, o_ref, *, n_first):
    i = pl.program_id(0)
    o = _two_path(i, n_first, ofp_ref, ofs_ref) + _two_path(i, n_first, obp_ref, obs_ref)
    y = o * lax.rsqrt(_head_mean_sq(o, p_ref[...]) + EPS) * nw_ref[...]
    o_ref[...] = (y * _silu(g_ref[...])).astype(o_ref.dtype)


def _gate_norm_call(of_p, of_s, ob_p, ob_s, z, gcol, nw, n_first):
    n = of_p.shape[0] + of_s.shape[0]
    row = lambda i: (i, 0)
    return pl.pallas_call(
        functools.partial(_gate_norm_kernel, n_first=n_first),
        grid=(n // TM,),
        in_specs=_two_path_specs(HW, n_first) + _two_path_specs(HW, n_first)
        + [pl.BlockSpec((TM, HW), lambda i: (i, gcol // HW)),
           pl.BlockSpec((1, HW), lambda i: (0, 0)),
           pl.BlockSpec((HW, HW), lambda i: (0, 0))],
        out_specs=pl.BlockSpec((TM, HW), row),
        out_shape=jax.ShapeDtypeStruct((n, HW), BF16),
        compiler_params=_cparams(("parallel",)),
        name="gate_norm",
    )(of_p, of_s, ob_p, ob_s, z, jnp.tile(nw, HW // HEAD_DIM).reshape(1, HW), _seg_matrix(HW))


DQKV_W = 3 * HW
HALO = 8


def _softplus(x):
    return jnp.maximum(x, 0.0) + jnp.log1p(jnp.exp(-jnp.abs(x)))


def _delta_prep_kernel(x_ref, xp_ref, xn_ref, cw_ref, dab_ref, ex_ref, al_ref, dtb_ref, p_ref, qkv_ref, gb_ref,
                       *, n_single, tiles_per_seq):
    i = pl.program_id(0)
    j = jnp.maximum(i - n_single, 0) % tiles_per_seq
    first = (i < n_single) | (j == 0)
    last = (i < n_single) | (j == tiles_per_seq - 1)
    x = x_ref[...]
    tm = x.shape[0]
    prev = jnp.where(first, 0.0, xp_ref[...])
    nxt = jnp.where(last, 0.0, xn_ref[...])
    xe = jnp.concatenate([prev, x, nxt], axis=0)
    y = None
    for tap in range(CONV_K):
        lo = HALO + tap - CONV_K // 2
        term = cw_ref[tap:tap + 1, :] * xe[lo:lo + tm]
        y = term if y is None else y + term
    y = _silu(y)
    qk = y[:, :2 * HW]
    ssq = _seg_sum(qk * qk, p_ref[...])
    qkn = qk * lax.rsqrt(ssq + EPS)
    qkv_ref[...] = jnp.concatenate([qkn[:, :HW] * (HEAD_DIM ** -0.5), qkn[:, HW:], y[:, 2 * HW:]], axis=1)
    e = jnp.dot(dab_ref[...], ex_ref[...], preferred_element_type=F32, precision=HIGHEST)
    g = -jnp.exp(al_ref[...]) * _softplus(e[:, :2 * HW] + dtb_ref[...])
    gb_ref[...] = jnp.concatenate([g, jax.nn.sigmoid(e[:, 2 * HW:])], axis=1)


def _delta_prep_call(z, conv_w, a_log, dt_bias, n_single, tiles_per_seq):
    n = z.shape[0]
    cb = COL['dq'] // DQKV_W
    hb = TM // HALO
    nhb = n // HALO
    ex = np.zeros((LANES, 4 * HW), np.float32)
    for r in range(4 * D_HEADS):
        ex[r, r * HEAD_DIM:(r + 1) * HEAD_DIM] = 1.0
    cw = jnp.zeros((8, DQKV_W), F32).at[:CONV_K].set(conv_w.astype(F32))
    const = lambda i: (0, 0)
    return pl.pallas_call(
        functools.partial(_delta_prep_kernel, n_single=n_single, tiles_per_seq=tiles_per_seq),
        grid=(n // TM,),
        in_specs=[pl.BlockSpec((TM, DQKV_W), lambda i: (i, cb)),
                  pl.BlockSpec((HALO, DQKV_W), lambda i: (jnp.maximum(i * hb - 1, 0), cb)),
                  pl.BlockSpec((HALO, DQKV_W), lambda i: (jnp.minimum((i + 1) * hb, nhb - 1), cb)),
                  pl.BlockSpec((8, DQKV_W), const),
                  pl.BlockSpec((TM, LANES), lambda i: (i, COL['dab'] // LANES)),
                  pl.BlockSpec((LANES, 4 * HW), const),
                  pl.BlockSpec((1, 2 * HW), const), pl.BlockSpec((1, 2 * HW), const),
                  pl.BlockSpec((2 * HW, 2 * HW), const)],
        out_specs=[pl.BlockSpec((TM, DQKV_W), lambda i: (i, 0)), pl.BlockSpec((TM, 4 * HW), lambda i: (i, 0))],
        out_shape=[jax.ShapeDtypeStruct((n, DQKV_W), F32), jax.ShapeDtypeStruct((n, 4 * HW), F32)],
        compiler_params=_cparams(("parallel",)),
        name="delta_prep",
    )(z, z, z, cw, z, jnp.asarray(ex),
      jnp.repeat(a_log.astype(F32).reshape(-1), HEAD_DIM).reshape(1, 2 * HW),
      jnp.repeat(dt_bias.astype(F32).reshape(-1), HEAD_DIM).reshape(1, 2 * HW), _seg_matrix(2 * HW))


def _delta_consts():
    t = np.arange(CHUNK)[:, None]
    s = (np.arange(HW) % CHUNK)[None, :]
    s16 = t // 16 == s // 16
    s32 = t // 32 == s // 32
    mats = [t == s, s16, s32 & ~s16, ~s32]
    return jnp.asarray(np.stack(mats).astype(np.float32))


def _tri_inv(acs, mask_bf, eye, m16, m32, m64):
    bf = lambda x: x.astype(BF16)
    mm = lambda x, y: _dot(bf(x), _bd(y, mask_bf))
    a16 = [a * m16 for a in acs]
    tin = [eye - a for a in a16]
    p = _each(mm, a16, a16)
    for step in range(3):
        tin = _each(lambda t, pp: t + mm(t, pp), tin, p)
        if step < 2:
            p = _each(lambda pp: mm(pp, pp), p)
    for m in (m32, m64):
        inner = _each(lambda a, t: mm(a * m, t), acs, tin)
        tin = _each(lambda t, i_n: t - mm(t, i_n), tin, inner)
    return tin


def _delta_chunks(ops_list, st_refs, mask_ref, cmask_ref, revs):
    mask = mask_ref[...]
    mask_bf = mask.astype(BF16)
    bf = lambda x: x.astype(BF16)
    pts = [_scan_rows(r) for r in revs]
    pss = [_scan_lanes(r) for r in revs]
    qs = [o[0][:, :HW] for o in ops_list]
    ks = [o[0][:, HW:2 * HW] for o in ops_list]
    vs = [o[0][:, 2 * HW:] for o in ops_list]
    betas = [o[2] for o in ops_list]
    gcs = _each(_cumsum_scan, [o[1] for o in ops_list], pts, revs)
    t_idx = lax.broadcasted_iota(jnp.int32, (CHUNK, HW), 0)
    s_idx = lax.broadcasted_iota(jnp.int32, (CHUNK, HW), 1) % CHUNK
    diag = t_idx == s_idx
    gcl = [jnp.sum(jnp.where(diag, gc, 0.0), axis=0, keepdims=True) for gc in gcs]
    gam = _each(lambda gc, gl: jnp.exp(jnp.minimum(gc - gl, 0.0)), gcs, gcl)
    g_last = _each(lambda gc, r: _row(gc, CHUNK - 1, r), gcs, revs)
    egc = [jnp.exp(gc) for gc in gcs]
    kb = _each(lambda k, b: k * b, ks, betas)
    r = _each(lambda k_b, q, k: _dot_t(bf(jnp.concatenate([k_b, q], axis=0)), _bd(k, mask_bf)), kb, qs, ks)
    a = _each(lambda rr, gm, ps, pt: jnp.where(ps < pt, rr[:CHUNK] * gm, 0.0), r, gam, pss, pts)
    aq = _each(lambda rr, gm, ps, pt: jnp.where(ps <= pt, rr[CHUNK:] * gm, 0.0), r, gam, pss, pts)
    tin = _tri_inv(a, mask_bf, cmask_ref[0], cmask_ref[1], cmask_ref[2], cmask_ref[3])
    rhs = _each(lambda v, b, k_b, e: jnp.concatenate([_bd(v * b, mask_bf), _bd(k_b * e, mask_bf)], axis=1),
                vs, betas, kb, egc)
    uw = _each(lambda t, rh: _dot(bf(t), rh), tin, rhs)
    st = [s[...] for s in st_refs]
    ws_qs = _each(lambda x, q, e, s: _dot_t(bf(jnp.concatenate([x[:, HW:], q * e], axis=0)), bf(s)),
                  uw, qs, egc, st)
    v_new = _each(lambda x, y: x[:, :HW] - y[:CHUNK], uw, ws_qs)
    o = _each(lambda y, aa, vn: y[CHUNK:] + _dot(bf(aa), _bd(vn, mask_bf)), ws_qs, aq, v_new)
    ke = _each(lambda k, gl, gc: bf(k * jnp.exp(gl - gc)), ks, g_last, gcs)
    upd = _each(lambda vn, kk: lax.dot_general(bf(vn), kk, (((0,), (0,)), ((), ())), preferred_element_type=F32),
                v_new, ke)
    for s_ref, gl, s, u in zip(st_refs, g_last, st, upd):
        s_ref[...] = jnp.exp(gl) * s + mask * u
    return o


def _delta_chunks_fn(ops_list, consts, st_refs, directions):
    return _delta_chunks(ops_list, st_refs, consts[0], consts[1], [d == 1 for d in directions])


def _delta_call(qkv, gb, row0, nb, t, s0_bd, group):
    ops_f = [(qkv, DQKV_W, 0), (gb, HW, 0), (gb, HW, 2)]
    ops_b = [(qkv, DQKV_W, 0), (gb, HW, 1), (gb, HW, 3)]
    return _scan_call("delta_scan", _delta_chunks_fn, ops_f, ops_b, [_head_mask(), _delta_consts()], row0, nb, t, s0_bd, group)


def kernel(x_prompt, x_sample, cache_a_k, cache_a_v, cache_b_k, cache_b_v, state_hgrn, state_delta, c, c_ctx, w_mod, b_mod, norm1, norm2, w_in, w_gate, a_qn, a_kn, b_qn, b_kn, b_rpb, c_lb, c_norm, d_conv, d_alog, d_dtbias, d_norm, w_branch, w_out, w_router, b_router, w_gu, b_gu, w_down, b_down):
    nbp, tp, d = x_prompt.shape
    nbs, ts, _ = x_sample.shape
    depth = w_mod.shape[0]
    past = cache_a_k.shape[2]
    n_p = nbp * tp
    n = n_p + nbs * ts
    assert tp == TM and ts % TM == 0 and n_p % ts == 0 and 1 + nbs <= 8 and n % TM_MM == 0
    assert d == ROW_TILE * LANES
    npt, tps = n_p // TM, ts // TM
    grp_p = next(g for g in (4, 2, 1) if nbp % g == 0)
    grp_s = next(g for g in (2, 1) if nbs % g == 0)

    def cond_of_tile(i):
        return jnp.where(i < npt, 0, 1 + (i - npt) // tps)

    x = jnp.concatenate([x_prompt.reshape(n_p, d), x_sample.reshape(nbs * ts, d)], axis=0)
    conds = jnp.zeros((8, d), F32).at[0].set(c_ctx).at[1:1 + nbs].set(c)
    mod = _mod_call(conds, w_mod, b_mod)

    pl_ = jax.nn.softmax(c_lb.astype(F32), axis=0)
    lbs = jnp.cumsum(pl_, axis=0) - pl_[0:1]
    dqkv0 = COL['cg'] + HW
    da0 = dqkv0 + DQKV_W
    dg0 = da0 + 4 * D_HEADS
    w_in_p = jnp.concatenate([w_in[:, :, :dqkv0], w_in[:, :, dg0:dg0 + HW], w_in[:, :, dqkv0:da0],
                              w_in[:, :, da0:dg0],
                              jnp.zeros((depth, d, IN_W_PAD - dg0 - HW), w_in.dtype)], axis=2).astype(BF16)
    w_gate_b = w_gate.astype(BF16)
    w_branch_b = w_branch.astype(BF16)
    w_out_b = w_out.astype(BF16)
    w_router_p = jnp.zeros((depth, d, LANES), F32).at[:, :, :N_EXPERTS].set(w_router)
    b_router_p = jnp.full((depth, 1, LANES), NEG, F32).at[:, 0, :N_EXPERTS].set(b_router)

    tt = jnp.arange(ts, dtype=jnp.int32)
    half = HEAD_DIM // 2
    inv_freq = 1.0 / (ROPE_THETA ** (jnp.arange(0, half, 2, dtype=F32) / half))
    ang_r = (tt // GRID_W).astype(F32)[:, None] * inv_freq[None, :]
    ang_c = (tt % GRID_W).astype(F32)[:, None] * inv_freq[None, :]
    ang = jnp.concatenate([ang_r, ang_r, ang_c, ang_c], axis=-1)
    sign = jnp.asarray(np.where((np.arange(HEAD_DIM) % 32) < 16, -1.0, 1.0).astype(np.float32))
    cos_all = jnp.concatenate([jnp.ones((n_p, HEAD_DIM), F32), jnp.tile(jnp.cos(ang), (nbs, 1))], axis=0)
    sin_all = jnp.concatenate([jnp.zeros((n_p, HEAD_DIM), F32), jnp.tile(jnp.sin(ang) * sign, (nbs, 1))], axis=0)
    rope_tabs = (jnp.tile(cos_all, (1, 2)), jnp.tile(sin_all, (1, 2)))
    scale = HEAD_DIM ** -0.5
    wa, wb = A_KV_HEADS * HEAD_DIM, B_HEADS * HEAD_DIM

    ak_l, av_l, bk_l, bv_l, sc_l, sd_l = [], [], [], [], [], []
    for l in range(depth):
        mod3 = mod[l].reshape(8, 1, N_MOD * d)
        h = _norm_call(x, norm1[l].reshape(1, d), mod3, cond_of_tile)
        z = _mm_call(h, w_in_p, l)

        (qa,) = _prep_call(z, COL['aq'], A_HEADS * HEAD_DIM, a_qn[l], scale * LOG2E, rope_tabs, False)
        ka_n, ka = _prep_call(z, COL['ak'], wa, a_kn[l], 1.0, rope_tabs, True)
        va_ext = _ones_extended(z[:, COL['av']:COL['av'] + wa], A_KV_HEADS)
        oa_p = _attn_call(qa, ka, va_ext, 0, nbp, tp, A_HEADS, A_KV_HEADS, True)
        oa_s = _attn_call(qa, ka, va_ext, n_p, nbs, ts, A_HEADS, A_KV_HEADS, True,
                          ctx=(cache_a_k[:, l].reshape(nbs, past, wa).astype(BF16),
                               _ones_extended(cache_a_v[:, l].reshape(nbs, past, wa), A_KV_HEADS)))

        (qb,) = _prep_call(z, COL['bq'], wb, b_qn[l], scale, None, False)
        kb_n, kb = _prep_call(z, COL['bk'], wb, b_kn[l], 1.0, None, True)
        ob_p = _attn_call(qb, kb, _ones_extended(z[:n_p, COL['bv']:COL['bv'] + wb], B_HEADS), 0, nbp, tp,
                          B_HEADS, B_HEADS, False)
        ob_s = _nbr_call(qb, kb, z, COL['bv'], n_p, nbs, ts,
                         cache_b_k[:, l].reshape(nbs, past, wb).astype(BF16),
                         cache_b_v[:, l].reshape(nbs, past, wb).astype(BF16), _nbr_bias_table(b_rpb[l]))

        cf_p, cb_p, sc_p = _hgrn_call(z, 0, nbp, tp, lbs[l], None, grp_p)
        cf_s, cb_s, _ = _hgrn_call(z, n_p, nbs, ts, lbs[l], _state_to_bd(state_hgrn[:, l]), grp_s)
        oc = _gate_norm_call(cf_p, cf_s, cb_p, cb_s, z, COL['cg'], c_norm[l], npt)

        qkv, gb = _delta_prep_call(z, d_conv[l], d_alog[l], d_dtbias[l], npt, tps)
        df_p, db_p, sd_p = _delta_call(qkv, gb, 0, nbp, tp, None, grp_p)
        df_s, db_s, _ = _delta_call(qkv, gb, n_p, nbs, ts, _state_to_bd(state_delta[:, l]), grp_s)
        od = _gate_norm_call(df_p, df_s, db_p, db_s, z, COL['dg'], d_norm[l], npt)

        x = _merge_call(x, h, oa_p, oa_s, ob_p, ob_s, oc, od, w_gate_b, w_branch_b, w_out_b, l, mod3, cond_of_tile,
                        npt)

        h_rows, gates, sel, counts = _router_call(x, norm2[l].reshape(1, d), mod3, cond_of_tile,
                                                  w_router_p, b_router_p, l)
        dest, block_e, n_used, next_e, last_blk, n_blocks = _slot_plan(sel, counts, n)
        x_slots = _dispatch_call(dest, last_blk, n_used, h_rows, n_blocks * MOE_TM)
        y_slots = _gmm_call(block_e, n_used, next_e, x_slots, w_gu, b_gu, w_down, b_down, l)
        x = _combine_call(dest, x, gates, mod3, cond_of_tile, y_slots)

        ak_l.append(ka_n[:n_p].reshape(nbp, tp, A_KV_HEADS, HEAD_DIM))
        av_l.append(z[:n_p, COL['av']:COL['av'] + wa].reshape(nbp, tp, A_KV_HEADS, HEAD_DIM))
        bk_l.append(kb_n[:n_p].reshape(nbp, tp, B_HEADS, HEAD_DIM))
        bv_l.append(z[:n_p, COL['bv']:COL['bv'] + wb].reshape(nbp, tp, B_HEADS, HEAD_DIM))
        sc_l.append(_state_from_bd(sc_p))
        sd_l.append(_state_from_bd(sd_p))

    return (x[:n_p].reshape(nbp, tp, d), x[n_p:].reshape(nbs, ts, d),
            jnp.stack(ak_l, axis=1), jnp.stack(av_l, axis=1), jnp.stack(bk_l, axis=1), jnp.stack(bv_l, axis=1),
            jnp.stack(sc_l, axis=1), jnp.stack(sd_l, axis=1))
```

```python
import functools

import numpy as np
import jax
import jax.numpy as jnp
from jax import lax
from jax.experimental import pallas as pl
from jax.experimental.pallas import tpu as pltpu

F32 = jnp.float32
BF16 = jnp.bfloat16
HIGHEST = lax.Precision.HIGHEST

GRID_W = 64
HEAD_DIM = 64
A_HEADS = 8
A_KV_HEADS = 2
B_HEADS = 4
NB_ROWS = 8
NB_COLS = 16
C_HEADS = 4
D_HEADS = 4
CONV_K = 5
CHUNK = 64
N_EXPERTS = 32
TOP_K = 4
SWIGLU_LIMIT = 7.0
SWIGLU_ALPHA = 1.702
ROPE_THETA = 10000.0
EPS = 1e-6
MAX_INPUT_GATE = 1.0 - 1e-6
N_MOD = 6
HW = 256

LANES = 128
VMEM_LIMIT = 56 * 1024 * 1024
NEG = -1e30

TM = 256
TM_MM = 512
TN_MM = 1024
TQ = 256
MOE_TM = 256

COL = dict(aq=0, ak=512, av=640, bq=768, bk=1024, bv=1280, cq=1536, ci=1792, cff=2048, cfb=2304,
           cg=2560, dg=2816, dq=3072, dk=3328, dv=3584, dab=3840)
IN_W_PAD = 4096


def _cparams(sem, vmem=VMEM_LIMIT):
    return pltpu.CompilerParams(dimension_semantics=sem, vmem_limit_bytes=vmem)


def _silu(x):
    return x * jax.nn.sigmoid(x)


def _dot(a, b):
    return jnp.dot(a, b, preferred_element_type=F32)


def _dot_t(a, b):
    return lax.dot_general(a, b, (((1,), (1,)), ((), ())), preferred_element_type=F32)


def _each(f, *lists):
    return [f(*a) for a in zip(*lists)]


def _mod_kernel(c_ref, w_ref, b_ref, o_ref):
    a = _silu(c_ref[...]).astype(BF16)
    o_ref[...] = _dot(a, w_ref[...].astype(BF16)) + b_ref[...]


def _mod_call(conds, w_mod, b_mod):
    depth, d, n = w_mod.shape
    tn = 1024
    return pl.pallas_call(
        _mod_kernel,
        grid=(depth, n // tn),
        in_specs=[pl.BlockSpec((8, d), lambda l, j: (0, 0)),
                  pl.BlockSpec((None, d, tn), lambda l, j: (l, 0, j)),
                  pl.BlockSpec((None, 1, tn), lambda l, j: (l, 0, j))],
        out_specs=pl.BlockSpec((None, 8, tn), lambda l, j: (l, 0, j)),
        out_shape=jax.ShapeDtypeStruct((depth, 8, n), F32),
        compiler_params=_cparams(("parallel", "parallel")),
        name="mod",
    )(conds, w_mod, b_mod.reshape(depth, 1, n))


def _mod_norm(x, g, m, k_shift, k_scale, d):
    y = x * lax.rsqrt(jnp.mean(x * x, axis=-1, keepdims=True) + EPS) * g
    return y * (1.0 + m[:, k_scale * d:(k_scale + 1) * d]) + m[:, k_shift * d:(k_shift + 1) * d]


def _norm_kernel(x_ref, g_ref, mod_ref, h_ref, *, d):
    h_ref[...] = _mod_norm(x_ref[...], g_ref[...], mod_ref[...], 0, 1, d).astype(BF16)


def _norm_call(x, g, mod3, cond_of_tile):
    n, d = x.shape
    return pl.pallas_call(
        functools.partial(_norm_kernel, d=d),
        grid=(n // TM,),
        in_specs=[pl.BlockSpec((TM, d), lambda i: (i, 0)),
                  pl.BlockSpec((1, d), lambda i: (0, 0)),
                  pl.BlockSpec((None, 1, N_MOD * d), lambda i: (cond_of_tile(i), 0, 0))],
        out_specs=pl.BlockSpec((TM, d), lambda i: (i, 0)),
        out_shape=jax.ShapeDtypeStruct((n, d), BF16),
        compiler_params=_cparams(("parallel",)),
        name="norm1",
    )(x, g, mod3)


def _mm_kernel(a_ref, w_ref, o_ref):
    o_ref[...] = _dot(a_ref[...], w_ref[...]).astype(o_ref.dtype)


def _mm_call(a, w, l, out_dtype=F32):
    m, k = a.shape
    n = w.shape[2]
    return pl.pallas_call(
        _mm_kernel,
        grid=(n // TN_MM, m // TM_MM),
        in_specs=[pl.BlockSpec((TM_MM, k), lambda j, i: (i, 0)),
                  pl.BlockSpec((None, k, TN_MM), lambda j, i: (l, 0, j))],
        out_specs=pl.BlockSpec((TM_MM, TN_MM), lambda j, i: (i, j)),
        out_shape=jax.ShapeDtypeStruct((m, n), out_dtype),
        compiler_params=_cparams(("parallel", "parallel")),
        name="in_proj",
    )(a, w)


def _two_path(i, n_first, first_ref, second_ref):
    return jnp.where(i < n_first, first_ref[...], second_ref[...])


def _two_path_specs(width, n_first):
    return [pl.BlockSpec((TM, width), lambda i: (jnp.minimum(i, n_first - 1), 0)),
            pl.BlockSpec((TM, width), lambda i: (jnp.maximum(i - n_first, 0), 0))]


def _merge_kernel(x_ref, h_ref, oap_ref, oas_ref, obp_ref, obs_ref, oc_ref, od_ref, wg_ref, wb_ref, wo_ref, mod_ref,
                  o_ref, *, d, n_first):
    i = pl.program_id(0)
    h = h_ref[...]
    outs = (_two_path(i, n_first, oap_ref, oas_ref), _two_path(i, n_first, obp_ref, obs_ref), oc_ref[...], od_ref[...])
    merged = None
    off = 0
    for j in range(4):
        wdt = outs[j].shape[1]
        gate = jax.nn.sigmoid(_dot(h, wg_ref[:, j * d:(j + 1) * d]))
        br = gate * _dot(outs[j], wb_ref[off:off + wdt, :])
        merged = br if merged is None else merged + br
        off += wdt
    y = _dot(merged.astype(BF16), wo_ref[...])
    o_ref[...] = x_ref[...] + mod_ref[...][:, 2 * d:3 * d] * y


def _merge_call(x, h, oa_p, oa_s, ob_p, ob_s, oc, od, w_gate, w_branch, w_out, l, mod3, cond_of_tile, n_first):
    n, d = x.shape
    row = lambda i: (i, 0)
    const = lambda i: (l, 0, 0)
    once = pl.Buffered(1)
    return pl.pallas_call(
        functools.partial(_merge_kernel, d=d, n_first=n_first),
        grid=(n // TM,),
        in_specs=[pl.BlockSpec((TM, d), row), pl.BlockSpec((TM, d), row)]
        + _two_path_specs(oa_p.shape[1], n_first) + _two_path_specs(ob_p.shape[1], n_first)
        + [pl.BlockSpec((TM, oc.shape[1]), row), pl.BlockSpec((TM, od.shape[1]), row),
           pl.BlockSpec((None,) + w_gate.shape[1:], const, pipeline_mode=once),
           pl.BlockSpec((None,) + w_branch.shape[1:], const, pipeline_mode=once),
           pl.BlockSpec((None,) + w_out.shape[1:], const, pipeline_mode=once),
           pl.BlockSpec((None, 1, N_MOD * d), lambda i: (cond_of_tile(i), 0, 0))],
        out_specs=pl.BlockSpec((TM, d), row),
        out_shape=jax.ShapeDtypeStruct((n, d), F32),
        compiler_params=_cparams(("parallel",)),
        name="merge",
    )(x, h, oa_p, oa_s, ob_p, ob_s, oc, od, w_gate, w_branch, w_out, mod3)


def _router_kernel(x_ref, g_ref, mod_ref, wr_ref, br_ref, ltri_ref, h_ref, gate_ref, sel_ref, cnt_ref, run_s, *, d):
    i = pl.program_id(0)

    @pl.when(i == 0)
    def _():
        run_s[...] = jnp.zeros(run_s.shape, F32)

    h2 = _mod_norm(x_ref[...], g_ref[...], mod_ref[...], 3, 4, d)
    nsub = d // LANES
    for sub in range(nsub):
        h_ref[pl.ds(sub, x_ref.shape[0], stride=nsub), :] = h2[:, sub * LANES:(sub + 1) * LANES]
    logits = jnp.dot(h2, wr_ref[...], preferred_element_type=F32, precision=HIGHEST) + br_ref[...]
    lane = lax.broadcasted_iota(jnp.int32, logits.shape, 1)
    vals, idxs = [], []
    cur = logits
    for _ in range(TOP_K):
        m = jnp.max(cur, axis=-1, keepdims=True)
        ix = jnp.min(jnp.where(cur == m, lane, LANES), axis=-1, keepdims=True)
        vals.append(m)
        idxs.append(ix)
        cur = jnp.where(lane == ix, -jnp.inf, cur)
    es = [jnp.exp(v - vals[0]) for v in vals]
    tot = es[0] + es[1] + es[2] + es[3]
    gates = jnp.zeros(logits.shape, F32)
    sel = jnp.zeros(logits.shape, jnp.int32)
    base = run_s[...]
    ltri = ltri_ref[...]
    for k in range(TOP_K):
        onehot = jnp.where(lane == idxs[k], 1.0, 0.0)
        before = base + _dot(ltri, onehot.astype(BF16))
        rank = jnp.sum(onehot * before, axis=-1, keepdims=True).astype(jnp.int32)
        base = base + jnp.sum(onehot, axis=0, keepdims=True)
        gates = jnp.where(lane == k, es[k] / tot, gates)
        sel = jnp.where(lane == k, idxs[k], sel)
        sel = jnp.where(lane == TOP_K + k, rank, sel)
    run_s[...] = base
    gate_ref[...] = gates
    sel_ref[...] = sel
    cnt_ref[...] = base


def _router_call(x, g, mod3, cond_of_tile, w_router_p, b_router_p, l):
    n, d = x.shape
    row = lambda i: (i, 0)
    const = lambda i: (0, 0)
    layer = lambda i: (l, 0, 0)
    ltri = jnp.asarray(np.tril(np.ones((TM, TM), np.float32), -1)).astype(BF16)
    return pl.pallas_call(
        functools.partial(_router_kernel, d=d),
        grid=(n // TM,),
        in_specs=[pl.BlockSpec((TM, d), row),
                  pl.BlockSpec((1, d), const),
                  pl.BlockSpec((None, 1, N_MOD * d), lambda i: (cond_of_tile(i), 0, 0)),
                  pl.BlockSpec((None, d, LANES), layer),
                  pl.BlockSpec((None, 1, LANES), layer),
                  pl.BlockSpec((TM, TM), const)],
        out_specs=[pl.BlockSpec((TM * (d // LANES), LANES), row), pl.BlockSpec((TM, LANES), row),
                   pl.BlockSpec((TM, LANES), row), pl.BlockSpec((1, LANES), const)],
        out_shape=[jax.ShapeDtypeStruct((n * (d // LANES), LANES), F32), jax.ShapeDtypeStruct((n, LANES), F32),
                   jax.ShapeDtypeStruct((n, LANES), jnp.int32), jax.ShapeDtypeStruct((1, LANES), F32)],
        scratch_shapes=[pltpu.VMEM((1, LANES), F32)],
        compiler_params=_cparams(("arbitrary",)),
        name="router",
    )(x, g, mod3, w_router_p, b_router_p, ltri)


def _slot_plan(sel, counts, n_tok):
    cnt = counts[0, :N_EXPERTS].astype(jnp.int32)
    padded = (cnt + MOE_TM - 1) // MOE_TM * MOE_TM
    pad_end = jnp.cumsum(padded)
    pad_start = pad_end - padded
    dest = (pad_start[sel[:, :TOP_K]] + sel[:, TOP_K:2 * TOP_K]).reshape(n_tok * TOP_K)
    n_blocks = n_tok * TOP_K // MOE_TM + N_EXPERTS
    blk = jnp.arange(n_blocks, dtype=jnp.int32)
    block_e = jnp.minimum(jnp.sum((pad_end[None, :] <= (blk * MOE_TM)[:, None]).astype(jnp.int32), axis=1),
                          N_EXPERTS - 1)
    n_used = (pad_end[-1:] // MOE_TM).astype(jnp.int32)
    later = (blk[None, :] > blk[:, None]) & (block_e[None, :] != block_e[:, None]) & (blk[None, :] < n_used[0])
    first_later = jnp.argmax(later, axis=1)
    next_e = jnp.where(jnp.any(later, axis=1), block_e[first_later], -1).astype(jnp.int32)
    last_blk = jnp.where(padded > 0, pad_end // MOE_TM - 1, -1).astype(jnp.int32)
    return dest, block_e, n_used, next_e, last_blk, n_blocks


ROW_TILE = 8


def _row_slice(ref, r):
    return ref.at[pl.ds(pl.multiple_of(r * ROW_TILE, ROW_TILE), ROW_TILE)]


def _dispatch_kernel(dest_ref, last_ref, nu_ref, h_ref, o_ref, zero_s, sem, zsem):
    rows = h_ref.shape[0]
    tm = rows // ROW_TILE
    grp = zero_s.shape[0]
    n_groups = o_ref.shape[0] // grp
    base = pl.program_id(0) * (tm * TOP_K)

    @pl.when(pl.program_id(0) == 0)
    def _():
        zero_s[...] = jnp.zeros(zero_s.shape, zero_s.dtype)

        def fill(group):
            return pltpu.make_async_copy(zero_s, o_ref.at[pl.ds(pl.multiple_of(group * grp, grp), grp)], zsem)

        targets = [(last_ref[e] >= 0, last_ref[e]) for e in range(N_EXPERTS)]
        targets += [(nu_ref[0] + j < n_groups, nu_ref[0] + j) for j in range(N_EXPERTS)]
        for do, group in targets:
            @pl.when(do)
            def _():
                fill(group).start()
        for do, group in targets:
            @pl.when(do)
            def _():
                fill(group).wait()

    def body(t, carry):
        for k in range(TOP_K):
            pltpu.make_async_copy(_row_slice(h_ref, t), _row_slice(o_ref, dest_ref[base + t * TOP_K + k]),
                                  sem).start()
        return carry

    lax.fori_loop(0, tm, body, 0)
    for _ in range(TOP_K):
        pltpu.make_async_copy(h_ref, o_ref.at[pl.ds(0, rows)], sem).wait()


def _dispatch_call(dest, last_blk, n_used, h_rows, n_slots):
    n = h_rows.shape[0] // ROW_TILE
    grid_spec = pltpu.PrefetchScalarGridSpec(
        num_scalar_prefetch=3,
        grid=(n // TM,),
        in_specs=[pl.BlockSpec((TM * ROW_TILE, LANES), lambda i, dst, lst, nu: (i, 0))],
        out_specs=pl.BlockSpec(memory_space=pl.ANY),
        scratch_shapes=[pltpu.VMEM((MOE_TM * ROW_TILE, LANES), F32), pltpu.SemaphoreType.DMA(()),
                        pltpu.SemaphoreType.DMA(())],
    )
    return pl.pallas_call(
        _dispatch_kernel,
        grid_spec=grid_spec,
        out_shape=jax.ShapeDtypeStruct((n_slots * ROW_TILE, LANES), F32),
        compiler_params=_cparams(("arbitrary",)),
        name="dispatch",
    )(dest, last_blk, n_used, h_rows)


def _gmm_kernel(be_ref, nu_ref, nxt_ref, x_ref, wgu_hbm, bgu_ref, wd_hbm, bd_ref, y_ref, wgu_f, wd_f, wgu_s, wd_s,
                sems, *, f, layer):
    i = pl.program_id(0)
    used = i < nu_ref[0]
    changed = (i == 0) | (be_ref[i] != be_ref[jnp.maximum(i - 1, 0)])
    tm = x_ref.shape[0] // ROW_TILE

    def weight_copies(e):
        return (pltpu.make_async_copy(wgu_hbm.at[layer, e], wgu_f, sems.at[0]),
                pltpu.make_async_copy(wd_hbm.at[layer, e], wd_f, sems.at[1]))

    @pl.when(used & (i == 0))
    def _():
        for cp in weight_copies(be_ref[0]):
            cp.start()

    @pl.when(used & changed)
    def _():
        for cp in weight_copies(be_ref[i]):
            cp.wait()
        wgu_s[...] = wgu_f[...].astype(BF16)
        wd_s[...] = wd_f[...].astype(BF16)

        @pl.when(nxt_ref[i] >= 0)
        def _():
            for cp in weight_copies(nxt_ref[i]):
                cp.start()

    @pl.when(used)
    def _():
        x = jnp.concatenate([x_ref[pl.ds(sub, tm, stride=ROW_TILE), :] for sub in range(ROW_TILE)], axis=1)
        gu = _dot(x.astype(BF16), wgu_s[...]) + bgu_ref[...]
        gate = jnp.minimum(gu[:, :f], SWIGLU_LIMIT)
        up = jnp.clip(gu[:, f:], -SWIGLU_LIMIT, SWIGLU_LIMIT)
        act = gate * jax.nn.sigmoid(SWIGLU_ALPHA * gate) * (up + 1.0)
        y = _dot(act.astype(BF16), wd_s[...]) + bd_ref[...]
        for sub in range(ROW_TILE):
            y_ref[pl.ds(sub, tm, stride=ROW_TILE), :] = y[:, sub * LANES:(sub + 1) * LANES]

    @pl.when(jnp.logical_not(used))
    def _():
        y_ref[...] = jnp.zeros(y_ref.shape, y_ref.dtype)


def _gmm_call(block_e, n_used, next_e, x_slots, w_gu, b_gu, w_down, b_down, l):
    n_slots = x_slots.shape[0] // ROW_TILE
    depth, e, d, f2 = w_gu.shape
    f = f2 // 2
    n_blocks = n_slots // MOE_TM
    xrow = lambda i, be, nu, nx: (jnp.minimum(i, nu[0] - 1), 0)
    yrow = lambda i, be, nu, nx: (i, 0)
    bsel = lambda i, be, nu, nx: (l, be[i], 0, 0)
    grid_spec = pltpu.PrefetchScalarGridSpec(
        num_scalar_prefetch=3,
        grid=(n_blocks,),
        in_specs=[pl.BlockSpec((MOE_TM * ROW_TILE, LANES), xrow),
                  pl.BlockSpec(memory_space=pl.ANY),
                  pl.BlockSpec((None, None, 1, f2), bsel),
                  pl.BlockSpec(memory_space=pl.ANY),
                  pl.BlockSpec((None, None, 1, d), bsel)],
        out_specs=pl.BlockSpec((MOE_TM * ROW_TILE, LANES), yrow),
        scratch_shapes=[pltpu.VMEM((d, f2), F32), pltpu.VMEM((f, d), F32), pltpu.VMEM((d, f2), BF16),
                        pltpu.VMEM((f, d), BF16), pltpu.SemaphoreType.DMA((2,))],
    )
    return pl.pallas_call(
        functools.partial(_gmm_kernel, f=f, layer=l),
        grid_spec=grid_spec,
        out_shape=jax.ShapeDtypeStruct((n_slots * ROW_TILE, LANES), F32),
        compiler_params=_cparams(("arbitrary",)),
        name="experts",
    )(block_e, n_used, next_e, x_slots, w_gu, b_gu, w_down, b_down)


def _combine_kernel(dest_ref, x_ref, gate_ref, mod_ref, y_ref, o_ref, buf, sems, *, d):
    tm = x_ref.shape[0]
    rows = tm * ROW_TILE
    i = pl.program_id(0)
    n_steps = pl.num_programs(0)

    def issue(step, slot):
        base = step * (tm * TOP_K)

        def body(t, carry):
            for k in range(TOP_K):
                dst = buf.at[pl.ds(pl.multiple_of((slot * TOP_K + k) * rows + t * ROW_TILE, ROW_TILE), ROW_TILE)]
                pltpu.make_async_copy(_row_slice(y_ref, dest_ref[base + t * TOP_K + k]), dst, sems.at[slot]).start()
            return carry

        lax.fori_loop(0, tm, body, 0)

    @pl.when(i == 0)
    def _():
        issue(0, 0)

    @pl.when(i + 1 < n_steps)
    def _():
        issue(i + 1, (i + 1) % 2)

    slot = i % 2
    for k in range(TOP_K):
        off = pl.multiple_of((slot * TOP_K + k) * rows, ROW_TILE)
        pltpu.make_async_copy(y_ref.at[pl.ds(0, rows)], buf.at[pl.ds(off, rows)], sems.at[slot]).wait()
    g = gate_ref[...]
    gk = [jnp.broadcast_to(g[:, k:k + 1], (tm, LANES)) for k in range(TOP_K)]
    g2 = mod_ref[...][:, 5 * d:6 * d]
    for sub in range(ROW_TILE):
        sl = slice(sub * LANES, (sub + 1) * LANES)
        acc = None
        for k in range(TOP_K):
            off = pl.multiple_of((slot * TOP_K + k) * rows, ROW_TILE)
            term = gk[k] * buf[pl.ds(off + sub, tm, stride=ROW_TILE), :]
            acc = term if acc is None else acc + term
        o_ref[:, sl] = x_ref[:, sl] + g2[:, sl] * acc


def _combine_call(dest, x, gates, mod3, cond_of_tile, y_slots):
    n, d = x.shape
    row = lambda i, dst: (i, 0)
    grid_spec = pltpu.PrefetchScalarGridSpec(
        num_scalar_prefetch=1,
        grid=(n // TM,),
        in_specs=[pl.BlockSpec((TM, d), row),
                  pl.BlockSpec((TM, LANES), row),
                  pl.BlockSpec((None, 1, N_MOD * d), lambda i, dst: (cond_of_tile(i), 0, 0)),
                  pl.BlockSpec(memory_space=pl.ANY)],
        out_specs=pl.BlockSpec((TM, d), row),
        scratch_shapes=[pltpu.VMEM((2 * TOP_K * TM * ROW_TILE, LANES), F32), pltpu.SemaphoreType.DMA((2,))],
    )
    return pl.pallas_call(
        functools.partial(_combine_kernel, d=d),
        grid_spec=grid_spec,
        out_shape=jax.ShapeDtypeStruct((n, d), F32),
        compiler_params=_cparams(("arbitrary",)),
        name="combine",
    )(dest, x, gates, mod3, y_slots)


def _seg_matrix(w):
    i = np.arange(w) // HEAD_DIM
    return jnp.asarray((i[:, None] == i[None, :]).astype(np.float32)).astype(BF16)


def _seg_sum(y, p):
    hi = y.astype(BF16)
    lo = (y - hi.astype(F32)).astype(BF16)
    return _dot(hi, p) + _dot(lo, p)


def _head_mean_sq(x, p):
    return _seg_sum(x * x, p) * (1.0 / HEAD_DIM)


def _prep_kernel(z_ref, nw_ref, p_ref, *rest, scale, rope, want_norm):
    if rope:
        cos_ref, sin_ref = rest[0], rest[1]
        rest = rest[2:]
    x = z_ref[...]
    w = x.shape[1]
    y = x * lax.rsqrt(_head_mean_sq(x, p_ref[...]) + EPS) * nw_ref[...]
    if want_norm:
        rest[0][...] = y
    out_ref = rest[-1]
    if rope:
        reps = w // LANES
        cos = jnp.concatenate([cos_ref[...]] * reps, axis=1) if reps > 1 else cos_ref[...]
        sin = jnp.concatenate([sin_ref[...]] * reps, axis=1) if reps > 1 else sin_ref[...]
        lane = lax.broadcasted_iota(jnp.int32, y.shape, 1)
        first = (lane % 32) < 16
        rot = jnp.where(first, pltpu.roll(y, w - 16, axis=1), pltpu.roll(y, 16, axis=1))
        y = y * cos + rot * sin
    out_ref[...] = (y * scale).astype(BF16)


def _prep_call(z, col, w, nw, scale, rope_tabs, want_norm):
    n = z.shape[0]
    cb = col // w
    row = lambda i: (i, 0)
    in_specs = [pl.BlockSpec((TM, w), lambda i: (i, cb)),
                pl.BlockSpec((1, w), lambda i: (0, 0)),
                pl.BlockSpec((w, w), lambda i: (0, 0))]
    args = [z, jnp.tile(nw, w // HEAD_DIM).reshape(1, w), _seg_matrix(w)]
    if rope_tabs is not None:
        in_specs += [pl.BlockSpec((TM, LANES), row), pl.BlockSpec((TM, LANES), row)]
        args += list(rope_tabs)
    out_specs, out_shape = [], []
    if want_norm:
        out_specs.append(pl.BlockSpec((TM, w), row))
        out_shape.append(jax.ShapeDtypeStruct((n, w), F32))
    out_specs.append(pl.BlockSpec((TM, w), row))
    out_shape.append(jax.ShapeDtypeStruct((n, w), BF16))
    return pl.pallas_call(
        functools.partial(_prep_kernel, scale=scale, rope=rope_tabs is not None, want_norm=want_norm),
        grid=(n // TM,),
        in_specs=in_specs, out_specs=out_specs, out_shape=out_shape,
        compiler_params=_cparams(("parallel",)),
        name="qk_prep",
    )(*args)


KEY_BLOCK = 512
LOG2E = 1.4426950408889634


def _ones_extended(v, heads):
    vb = v.astype(BF16).reshape(v.shape[:-1] + (heads, HEAD_DIM))
    return jnp.concatenate([vb, jnp.ones_like(vb)], axis=-1).reshape(v.shape[:-1] + (heads * 2 * HEAD_DIM,))


def _attn_kernel(q_ref, k_ref, v_ref, *rest, hq, hk, has_ctx, base2):
    if has_ctx:
        ck_ref, cv_ref, o_ref = rest
    else:
        (o_ref,) = rest
    g = hq // hk
    tq = q_ref.shape[0]
    t = k_ref.shape[0]
    dh = HEAD_DIM
    kb = min(KEY_BLOCK, t)
    ex = jnp.exp2 if base2 else jnp.exp
    blocks = [(ck_ref, cv_ref, 0, ck_ref.shape[0])] if has_ctx else []
    blocks += [(k_ref, v_ref, s0, kb) for s0 in range(0, t, kb)]
    outs = []
    for j in range(hk):
        q4 = jnp.concatenate([q_ref[:, (j * g + gi) * dh:(j * g + gi + 1) * dh] for gi in range(g)], axis=0)
        m = acc = None
        for kref, vref, s0, n in blocks:
            s = _dot_t(q4, kref[s0:s0 + n, j * dh:(j + 1) * dh])
            bm = jnp.max(s, axis=-1, keepdims=True)
            m_new = bm if m is None else jnp.maximum(m, bm)
            pv = _dot(ex(s - m_new).astype(BF16), vref[s0:s0 + n, 2 * j * dh:2 * (j + 1) * dh])
            acc = pv if m is None else ex(m - m_new) * acc + pv
            m = m_new
        o = acc[:, :dh] / acc[:, dh:]
        outs += [o[gi * tq:(gi + 1) * tq] for gi in range(g)]
    o_ref[...] = jnp.concatenate(outs, axis=1).astype(o_ref.dtype)


def _attn_call(q, k, v_ext, row0, nb, t, hq, hk, base2, ctx=None):
    tq = min(TQ, t)
    wq, wk = hq * HEAD_DIM, hk * HEAD_DIM
    qb0, kb0 = row0 // tq, row0 // t
    nq = t // tq
    in_specs = [pl.BlockSpec((tq, wq), lambda b, i: (qb0 + b * nq + i, 0)),
                pl.BlockSpec((t, wk), lambda b, i: (kb0 + b, 0)),
                pl.BlockSpec((t, 2 * wk), lambda b, i: (kb0 + b, 0))]
    args = [q, k, v_ext]
    if ctx is not None:
        ck, cv, layer = ctx
        p = ck.shape[2]
        in_specs += [pl.BlockSpec((None, None, p, wk), lambda b, i: (b, layer, 0, 0)),
                     pl.BlockSpec((None, None, p, 2 * wk), lambda b, i: (b, layer, 0, 0))]
        args += [ck, cv]
    return pl.pallas_call(
        functools.partial(_attn_kernel, hq=hq, hk=hk, has_ctx=ctx is not None, base2=base2),
        grid=(nb, nq),
        in_specs=in_specs,
        out_specs=pl.BlockSpec((tq, wq), lambda b, i: (b * nq + i, 0)),
        out_shape=jax.ShapeDtypeStruct((nb * t, wq), BF16),
        compiler_params=_cparams(("parallel", "parallel")),
        name="attn",
    )(*args)


def _nbr_bias_table(rpb):
    o = np.arange(NB_ROWS)[:, None, None, None]
    cc = np.arange(GRID_W)[None, :, None, None]
    i = np.arange(NB_ROWS)[None, None, :, None]
    j = np.arange(GRID_W)[None, None, None, :]
    col_start = np.clip(cc - NB_COLS // 2, 0, GRID_W - NB_COLS)
    valid = (j >= col_start) & (j < col_start + NB_COLS)
    sel_r = ((i - o + NB_ROWS - 1)[..., None] == np.arange(2 * NB_ROWS - 1)).astype(np.float32)[:, 0, :, 0]
    sel_c = (((j - cc + NB_COLS - 1)[..., None] == np.arange(2 * NB_COLS - 1)) & valid[..., None])
    sel_c = sel_c.astype(np.float32)[0, :, 0]
    tab = jnp.einsum('lhrd,oir,cjd->lhocij', rpb.astype(F32), jnp.asarray(sel_r), jnp.asarray(sel_c),
                     precision=HIGHEST)
    tab = tab + jnp.asarray(np.where(valid, 0.0, NEG).astype(np.float32))[None, None, :, :, :, :]
    return tab.reshape(rpb.shape[0], rpb.shape[1], NB_ROWS, GRID_W, NB_ROWS * GRID_W)


def _nbr_kernel(q_ref, k_ref, v_ref, ck_ref, cv_ref, bias_ref, o_ref, *, rows):
    r = pl.program_id(1)
    rs = jnp.clip(r - NB_ROWS // 2, 0, rows - NB_ROWS)
    start = pl.multiple_of(rs * GRID_W, GRID_W)
    kw = k_ref[pl.ds(start, NB_ROWS * GRID_W), :]
    vw = v_ref[pl.ds(start, NB_ROWS * GRID_W), :].astype(BF16)
    dh = HEAD_DIM
    sls = [slice(h * dh, (h + 1) * dh) for h in range(B_HEADS)]
    rowmax = lambda x: jnp.max(x, axis=-1, keepdims=True)
    rowsum = lambda x: jnp.sum(x, axis=-1, keepdims=True)
    qs = [q_ref[:, sl] for sl in sls]
    s_loc = [_dot_t(q, kw[:, sl]) + bias_ref[h] for h, (q, sl) in enumerate(zip(qs, sls))]
    s_ctx = [_dot_t(q, ck_ref[:, sl]) for q, sl in zip(qs, sls)]
    m = _each(lambda a, b: jnp.maximum(rowmax(a), rowmax(b)), s_loc, s_ctx)
    p_loc = _each(lambda a, mm: jnp.exp(a - mm), s_loc, m)
    p_ctx = _each(lambda a, mm: jnp.exp(a - mm), s_ctx, m)
    l = _each(lambda a, b: rowsum(a) + rowsum(b), p_loc, p_ctx)
    o = [_dot(a.astype(BF16), vw[:, sl]) + _dot(b.astype(BF16), cv_ref[:, sl]) for a, b, sl in zip(p_loc, p_ctx, sls)]
    o_ref[...] = jnp.concatenate(_each(lambda a, b: a / b, o, l), axis=1).astype(o_ref.dtype)


def _nbr_call(q, k, z, vcol, row0, nb, t, ck, cv, bias, layer):
    rows = t // GRID_W
    assert rows >= NB_ROWS
    w = B_HEADS * HEAD_DIM
    p = ck.shape[2]
    nwin = NB_ROWS * GRID_W

    def bias_map(b, r):
        return (layer, 0, r - jnp.clip(r - NB_ROWS // 2, 0, rows - NB_ROWS), 0, 0)

    return pl.pallas_call(
        functools.partial(_nbr_kernel, rows=rows),
        grid=(nb, rows),
        in_specs=[pl.BlockSpec((GRID_W, w), lambda b, r: (row0 // GRID_W + b * rows + r, 0)),
                  pl.BlockSpec((t, w), lambda b, r: (row0 // t + b, 0)),
                  pl.BlockSpec((t, w), lambda b, r: (row0 // t + b, vcol // w)),
                  pl.BlockSpec((None, None, p, w), lambda b, r: (b, layer, 0, 0)),
                  pl.BlockSpec((None, None, p, w), lambda b, r: (b, layer, 0, 0)),
                  pl.BlockSpec((None, B_HEADS, None, GRID_W, nwin), bias_map)],
        out_specs=pl.BlockSpec((GRID_W, w), lambda b, r: (b * rows + r, 0)),
        out_shape=jax.ShapeDtypeStruct((nb * t, w), BF16),
        compiler_params=_cparams(("parallel", "arbitrary")),
        name="nbr_attn",
    )(q, k, z, ck, cv, bias)


def _head_mask():
    i = np.arange(HW) // HEAD_DIM
    return jnp.asarray((i[:, None] == i[None, :]).astype(np.float32))


def _scan_rows(rev):
    t = lax.broadcasted_iota(jnp.int32, (CHUNK, HW), 0)
    return (CHUNK - 1 - t) if rev else t


def _scan_lanes(rev):
    s = lax.broadcasted_iota(jnp.int32, (CHUNK, HW), 1) % CHUNK
    return (CHUNK - 1 - s) if rev else s


def _sh(x, d, rev):
    s = (-d if rev else d) % CHUNK
    return x if s == 0 else pltpu.roll(x, s, axis=0)


def _cumsum_scan(x, pt, rev):
    d = 1
    while d < CHUNK:
        x = x + jnp.where(pt >= d, _sh(x, d, rev), 0.0)
        d *= 2
    return x


def _block_end(x, pt, n, rev):
    r = (n - 1) - pt % n
    bit = 1
    while bit < n:
        x = jnp.where((r & bit) != 0, _sh(x, -bit, rev), x)
        bit *= 2
    return x


def _pick16(x, j, rev):
    jj = (15 - j) if rev else j
    x3 = x.reshape(CHUNK // 16, 16, HW)
    return jnp.broadcast_to(x3[:, jj:jj + 1, :], x3.shape).reshape(CHUNK, HW)


def _row(x, p, rev):
    t = (CHUNK - 1 - p) if rev else p
    return x[t:t + 1, :]


def _bd(x, mask_bf):
    xb = x.astype(BF16)
    return jnp.concatenate([xb] * (HW // CHUNK), axis=0) * mask_bf


def _gla_chunks(ops_list, lbs, st_refs, mask, revs):
    mask_bf = mask.astype(BF16)
    bf = lambda x: x.astype(BF16)

    def prepare(ops, lb, rev):
        q_raw, v, f_raw = ops
        pt = _scan_rows(rev)
        q = _silu(q_raw)
        kk = jnp.minimum((1.0 - lb) * jax.nn.sigmoid(-f_raw), MAX_INPUT_GATE)
        b = _cumsum_scan(jnp.log1p(-kk), pt, rev)
        e4 = _block_end(b, pt, 4, rev)
        r4 = jnp.where(pt >= 4, _sh(e4, 4, rev), 0.0)
        r16 = jnp.where(pt >= 16, _sh(_pick16(b, 15, rev), 16, rev), 0.0)
        ks01 = [_bd(kk * jnp.exp(r4 - b), mask_bf)]
        ks2 = []
        for i in range(1, 4):
            ks01.append(_bd(kk * jnp.exp(jnp.minimum(_pick16(r4, 4 * i, rev) - b, 0.0)), mask_bf))
            ks2.append(_bd(kk * jnp.exp(jnp.minimum(_row(b, 16 * i - 1, rev) - b, 0.0)), mask_bf))
        return dict(pt=pt, ps=_scan_lanes(rev), v=v, kk=kk, b=b, b_last=_row(b, CHUNK - 1, rev),
                    q0=bf(q * jnp.exp(b - r4)), q2=bf(q * jnp.exp(b - r16)), qe=bf(q * jnp.exp(b)),
                    k01=jnp.concatenate(ks01, axis=0), k2=jnp.concatenate(ks2, axis=0))

    def select(c, r01, r2):
        pt, ps = c['pt'], c['ps']
        pt4, ps4, pt16, ps16 = pt // 4, ps // 4, pt // 16, ps // 16
        attn = jnp.where((pt4 == ps4) & (ps <= pt), r01[:, :HW], 0.0)
        for i in range(1, 4):
            attn = jnp.where((pt16 == ps16) & (ps4 < pt4) & (pt4 % 4 == i), r01[:, i * HW:(i + 1) * HW], attn)
            attn = jnp.where((ps16 < pt16) & (pt16 == i), r2[:, (i - 1) * HW:i * HW], attn)
        return bf(attn)

    cs = _each(prepare, ops_list, lbs, revs)
    r01 = [_dot_t(c['q0'], c['k01']) for c in cs]
    r2 = [_dot_t(c['q2'], c['k2']) for c in cs]
    attn = _each(select, cs, r01, r2)
    st = [s[...] for s in st_refs]
    o = _each(lambda c, s, aa: _dot_t(c['qe'], bf(s)) + _dot(aa, _bd(c['v'], mask_bf)), cs, st, attn)
    upd = [lax.dot_general(bf(c['v']), bf(c['kk'] * jnp.exp(c['b_last'] - c['b'])), (((0,), (0,)), ((), ())),
                           preferred_element_type=F32) for c in cs]
    for s_ref, c, s, u in zip(st_refs, cs, st, upd):
        s_ref[...] = jnp.exp(c['b_last']) * s + mask * u
    return o


def _scan_kernel(*refs, chunk_fn, n_in, n_const, group, has_state):
    n_op = group * 2 * n_in
    ins, consts, rest = refs[:n_op], refs[n_op:n_op + n_const], refs[n_op + n_const:]
    if has_state:
        s0_ref, of_ref, ob_ref, sout_ref, st_s = rest
    else:
        of_ref, ob_ref, sout_ref, st_s = rest
    c = pl.program_id(1)

    @pl.when(c == 0)
    def _():
        st_s[...] = s0_ref[...] if has_state else jnp.zeros(st_s.shape, F32)

    chains = [(g, direction) for g in range(group) for direction in (0, 1)]
    ops_list = [[r[...] for r in ins[(g * 2 + dr) * n_in:(g * 2 + dr + 1) * n_in]] for g, dr in chains]
    outs = chunk_fn(ops_list, consts, [st_s.at[g, dr] for g, dr in chains], [dr for _, dr in chains])
    for (g, dr), o in zip(chains, outs):
        (ob_ref if dr else of_ref)[g] = o

    @pl.when(c == pl.num_programs(1) - 1)
    def _():
        sout_ref[...] = st_s[...]


def _scan_call(name, chunk_fn, ops_f, ops_b, consts, row0, nb, t, s0_bd, group):
    assert nb % group == 0 and len(ops_f) == len(ops_b)
    nc = t // CHUNK
    r0 = row0 // CHUNK
    has_state = s0_bd is not None
    in_specs, args = [], []
    for g in range(group):
        for direction, ops in ((0, ops_f), (1, ops_b)):
            for arr, w, cb in ops:
                if direction == 0:
                    imap = functools.partial(lambda b, c, g, cb: (r0 + (b * group + g) * nc + c, cb), g=g, cb=cb)
                else:
                    imap = functools.partial(lambda b, c, g, cb: (r0 + (b * group + g) * nc + nc - 1 - c, cb),
                                             g=g, cb=cb)
                in_specs.append(pl.BlockSpec((CHUNK, w), imap))
                args.append(arr)
    for cst in consts:
        in_specs.append(pl.BlockSpec(cst.shape, functools.partial(lambda b, c, nd: (0,) * nd, nd=cst.ndim)))
        args.append(cst)
    state_spec = pl.BlockSpec((group, 2, HW, HW), lambda b, c: (b, 0, 0, 0))
    if has_state:
        s0_all, layer = s0_bd
        in_specs.append(pl.BlockSpec((group, None, 2, HW, HW), lambda b, c: (b, layer, 0, 0, 0)))
        args.append(s0_all)
    o_f, o_b, s_out = pl.pallas_call(
        functools.partial(_scan_kernel, chunk_fn=chunk_fn, n_in=len(ops_f), n_const=len(consts), group=group,
                          has_state=has_state),
        grid=(nb // group, nc),
        in_specs=in_specs,
        out_specs=[pl.BlockSpec((group, CHUNK, HW), lambda b, c: (b, c, 0)),
                   pl.BlockSpec((group, CHUNK, HW), lambda b, c: (b, nc - 1 - c, 0)),
                   state_spec],
        out_shape=[jax.ShapeDtypeStruct((nb, t, HW), F32), jax.ShapeDtypeStruct((nb, t, HW), F32),
                   jax.ShapeDtypeStruct((nb, 2, HW, HW), F32)],
        scratch_shapes=[pltpu.VMEM((group, 2, HW, HW), F32)],
        compiler_params=_cparams(("parallel", "arbitrary")),
        name=name,
    )(*args)
    return o_f.reshape(nb * t, HW), o_b.reshape(nb * t, HW), s_out


def _hgrn_chunks_fn(ops_list, consts, st_refs, directions):
    lb_ref, mask_ref = consts
    return _gla_chunks(ops_list, [lb_ref[dr:dr + 1, :] for dr in directions], st_refs, mask_ref[...],
                       [dr == 1 for dr in directions])


def _state_to_bd(s):
    eye = jnp.eye(C_HEADS, dtype=F32)
    return jnp.einsum('...hkv,hg->...hvgk', s.astype(F32), eye).reshape(s.shape[:-3] + (HW, HW))


def _state_from_bd(sbd):
    s6 = sbd.reshape(sbd.shape[:-2] + (C_HEADS, HEAD_DIM, C_HEADS, HEAD_DIM))
    diag = jnp.stack([s6[..., h, :, h, :] for h in range(C_HEADS)], axis=-3)
    return jnp.swapaxes(diag, -1, -2)


def _hgrn_call(z, row0, nb, t, lb, s0_bd, group):
    ops = lambda fcol: [(z, HW, COL['cq'] // HW), (z, HW, COL['ci'] // HW), (z, HW, fcol // HW)]
    return _scan_call("hgrn_scan", _hgrn_chunks_fn, ops(COL['cff']), ops(COL['cfb']), [lb, _head_mask()],
                      row0, nb, t, s0_bd, group)


def _gate_norm_kernel(ofp_ref, ofs_ref, obp_ref, obs_ref, g_ref, nw_ref, p_ref, o_ref, *, n_first):
    i = pl.program_id(0)
    o = _two_path(i, n_first, ofp_ref, ofs_ref) + _two_path(i, n_first, obp_ref, obs_ref)
    y = o * lax.rsqrt(_head_mean_sq(o, p_ref[...]) + EPS) * nw_ref[...]
    o_ref[...] = (y * _silu(g_ref[...])).astype(o_ref.dtype)


def _gate_norm_call(of_p, of_s, ob_p, ob_s, z, gcol, nw, n_first):
    n = of_p.shape[0] + of_s.shape[0]
    row = lambda i: (i, 0)
    return pl.pallas_call(
        functools.partial(_gate_norm_kernel, n_first=n_first),
        grid=(n // TM,),
        in_specs=_two_path_specs(HW, n_first) + _two_path_specs(HW, n_first)
        + [pl.BlockSpec((TM, HW), lambda i: (i, gcol // HW)),
           pl.BlockSpec((1, HW), lambda i: (0, 0)),
           pl.BlockSpec((HW, HW), lambda i: (0, 0))],
        out_specs=pl.BlockSpec((TM, HW), row),
        out_shape=jax.ShapeDtypeStruct((n, HW), BF16),
        compiler_params=_cparams(("parallel",)),
        name="gate_norm",
    )(of_p, of_s, ob_p, ob_s, z, jnp.tile(nw, HW // HEAD_DIM).reshape(1, HW), _seg_matrix(HW))


DQKV_W = 3 * HW
HALO = 8


def _softplus(x):
    return jnp.maximum(x, 0.0) + jnp.log1p(jnp.exp(-jnp.abs(x)))


def _delta_prep_kernel(x_ref, xp_ref, xn_ref, cw_ref, dab_ref, ex_ref, al_ref, dtb_ref, p_ref, qkv_ref, gb_ref,
                       *, n_single, tiles_per_seq):
    i = pl.program_id(0)
    j = jnp.maximum(i - n_single, 0) % tiles_per_seq
    first = (i < n_single) | (j == 0)
    last = (i < n_single) | (j == tiles_per_seq - 1)
    x = x_ref[...]
    tm = x.shape[0]
    prev = jnp.where(first, 0.0, xp_ref[...])
    nxt = jnp.where(last, 0.0, xn_ref[...])
    xe = jnp.concatenate([prev, x, nxt], axis=0)
    y = None
    for tap in range(CONV_K):
        lo = HALO + tap - CONV_K // 2
        term = cw_ref[tap:tap + 1, :] * xe[lo:lo + tm]
        y = term if y is None else y + term
    y = _silu(y)
    qk = y[:, :2 * HW]
    ssq = _seg_sum(qk * qk, p_ref[...])
    qkn = qk * lax.rsqrt(ssq + EPS)
    qkv_ref[...] = jnp.concatenate([qkn[:, :HW] * (HEAD_DIM ** -0.5), qkn[:, HW:], y[:, 2 * HW:]], axis=1)
    e = jnp.dot(dab_ref[...], ex_ref[...], preferred_element_type=F32, precision=HIGHEST)
    g = -jnp.exp(al_ref[...]) * _softplus(e[:, :2 * HW] + dtb_ref[...])
    gb_ref[...] = jnp.concatenate([g, jax.nn.sigmoid(e[:, 2 * HW:])], axis=1)


def _delta_prep_call(z, conv_w, a_log, dt_bias, n_single, tiles_per_seq):
    n = z.shape[0]
    cb = COL['dq'] // DQKV_W
    hb = TM // HALO
    nhb = n // HALO
    ex = np.zeros((LANES, 4 * HW), np.float32)
    for r in range(4 * D_HEADS):
        ex[r, r * HEAD_DIM:(r + 1) * HEAD_DIM] = 1.0
    cw = jnp.zeros((8, DQKV_W), F32).at[:CONV_K].set(conv_w.astype(F32))
    const = lambda i: (0, 0)
    return pl.pallas_call(
        functools.partial(_delta_prep_kernel, n_single=n_single, tiles_per_seq=tiles_per_seq),
        grid=(n // TM,),
        in_specs=[pl.BlockSpec((TM, DQKV_W), lambda i: (i, cb)),
                  pl.BlockSpec((HALO, DQKV_W), lambda i: (jnp.maximum(i * hb - 1, 0), cb)),
                  pl.BlockSpec((HALO, DQKV_W), lambda i: (jnp.minimum((i + 1) * hb, nhb - 1), cb)),
                  pl.BlockSpec((8, DQKV_W), const),
                  pl.BlockSpec((TM, LANES), lambda i: (i, COL['dab'] // LANES)),
                  pl.BlockSpec((LANES, 4 * HW), const),
                  pl.BlockSpec((1, 2 * HW), const), pl.BlockSpec((1, 2 * HW), const),
                  pl.BlockSpec((2 * HW, 2 * HW), const)],
        out_specs=[pl.BlockSpec((TM, DQKV_W), lambda i: (i, 0)), pl.BlockSpec((TM, 4 * HW), lambda i: (i, 0))],
        out_shape=[jax.ShapeDtypeStruct((n, DQKV_W), F32), jax.ShapeDtypeStruct((n, 4 * HW), F32)],
        compiler_params=_cparams(("parallel",)),
        name="delta_prep",
    )(z, z, z, cw, z, jnp.asarray(ex),
      jnp.repeat(a_log.astype(F32).reshape(-1), HEAD_DIM).reshape(1, 2 * HW),
      jnp.repeat(dt_bias.astype(F32).reshape(-1), HEAD_DIM).reshape(1, 2 * HW), _seg_matrix(2 * HW))


def _delta_consts():
    t = np.arange(CHUNK)[:, None]
    s = (np.arange(HW) % CHUNK)[None, :]
    s16 = t // 16 == s // 16
    s32 = t // 32 == s // 32
    mats = [t == s, s16, s32 & ~s16, ~s32]
    return jnp.asarray(np.stack(mats).astype(np.float32))


def _tri_inv(acs, mask_bf, eye, m16, m32, m64):
    bf = lambda x: x.astype(BF16)
    mm = lambda x, y: _dot(bf(x), _bd(y, mask_bf))
    a16 = [a * m16 for a in acs]
    tin = [eye - a for a in a16]
    p = _each(mm, a16, a16)
    for step in range(3):
        tin = _each(lambda t, pp: t + mm(t, pp), tin, p)
        if step < 2:
            p = _each(lambda pp: mm(pp, pp), p)
    for m in (m32, m64):
        inner = _each(lambda a, t: mm(a * m, t), acs, tin)
        tin = _each(lambda t, i_n: t - mm(t, i_n), tin, inner)
    return tin


def _delta_chunks(ops_list, st_refs, mask_ref, cmask_ref, revs):
    mask = mask_ref[...]
    mask_bf = mask.astype(BF16)
    bf = lambda x: x.astype(BF16)
    pts = [_scan_rows(r) for r in revs]
    pss = [_scan_lanes(r) for r in revs]
    qs = [o[0][:, :HW] for o in ops_list]
    ks = [o[0][:, HW:2 * HW] for o in ops_list]
    vs = [o[0][:, 2 * HW:] for o in ops_list]
    betas = [o[2] for o in ops_list]
    gcs = _each(_cumsum_scan, [o[1] for o in ops_list], pts, revs)
    t_idx = lax.broadcasted_iota(jnp.int32, (CHUNK, HW), 0)
    s_idx = lax.broadcasted_iota(jnp.int32, (CHUNK, HW), 1) % CHUNK
    diag = t_idx == s_idx
    gcl = [jnp.sum(jnp.where(diag, gc, 0.0), axis=0, keepdims=True) for gc in gcs]
    gam = _each(lambda gc, gl: jnp.exp(jnp.minimum(gc - gl, 0.0)), gcs, gcl)
    g_last = _each(lambda gc, r: _row(gc, CHUNK - 1, r), gcs, revs)
    egc = [jnp.exp(gc) for gc in gcs]
    kb = _each(lambda k, b: k * b, ks, betas)
    r = _each(lambda k_b, q, k: _dot_t(bf(jnp.concatenate([k_b, q], axis=0)), _bd(k, mask_bf)), kb, qs, ks)
    a = _each(lambda rr, gm, ps, pt: jnp.where(ps < pt, rr[:CHUNK] * gm, 0.0), r, gam, pss, pts)
    aq = _each(lambda rr, gm, ps, pt: jnp.where(ps <= pt, rr[CHUNK:] * gm, 0.0), r, gam, pss, pts)
    tin = _tri_inv(a, mask_bf, cmask_ref[0], cmask_ref[1], cmask_ref[2], cmask_ref[3])
    rhs = _each(lambda v, b, k_b, e: jnp.concatenate([_bd(v * b, mask_bf), _bd(k_b * e, mask_bf)], axis=1),
                vs, betas, kb, egc)
    uw = _each(lambda t, rh: _dot(bf(t), rh), tin, rhs)
    st = [s[...] for s in st_refs]
    ws_qs = _each(lambda x, q, e, s: _dot_t(bf(jnp.concatenate([x[:, HW:], q * e], axis=0)), bf(s)),
                  uw, qs, egc, st)
    v_new = _each(lambda x, y: x[:, :HW] - y[:CHUNK], uw, ws_qs)
    o = _each(lambda y, aa, vn: y[CHUNK:] + _dot(bf(aa), _bd(vn, mask_bf)), ws_qs, aq, v_new)
    ke = _each(lambda k, gl, gc: bf(k * jnp.exp(gl - gc)), ks, g_last, gcs)
    upd = _each(lambda vn, kk: lax.dot_general(bf(vn), kk, (((0,), (0,)), ((), ())), preferred_element_type=F32),
                v_new, ke)
    for s_ref, gl, s, u in zip(st_refs, g_last, st, upd):
        s_ref[...] = jnp.exp(gl) * s + mask * u
    return o


def _delta_chunks_fn(ops_list, consts, st_refs, directions):
    return _delta_chunks(ops_list, st_refs, consts[0], consts[1], [d == 1 for d in directions])


def _delta_call(qkv, gb, row0, nb, t, s0_bd, group):
    ops_f = [(qkv, DQKV_W, 0), (gb, HW, 0), (gb, HW, 2)]
    ops_b = [(qkv, DQKV_W, 0), (gb, HW, 1), (gb, HW, 3)]
    return _scan_call("delta_scan", _delta_chunks_fn, ops_f, ops_b, [_head_mask(), _delta_consts()], row0, nb, t, s0_bd, group)


def kernel(x_prompt, x_sample, cache_a_k, cache_a_v, cache_b_k, cache_b_v, state_hgrn, state_delta, c, c_ctx, w_mod, b_mod, norm1, norm2, w_in, w_gate, a_qn, a_kn, b_qn, b_kn, b_rpb, c_lb, c_norm, d_conv, d_alog, d_dtbias, d_norm, w_branch, w_out, w_router, b_router, w_gu, b_gu, w_down, b_down):
    nbp, tp, d = x_prompt.shape
    nbs, ts, _ = x_sample.shape
    depth = w_mod.shape[0]
    past = cache_a_k.shape[2]
    n_p = nbp * tp
    n = n_p + nbs * ts
    assert tp == TM and ts % TM == 0 and n_p % ts == 0 and 1 + nbs <= 8 and n % TM_MM == 0
    assert d == ROW_TILE * LANES
    npt, tps = n_p // TM, ts // TM
    grp_p = next(g for g in (4, 2, 1) if nbp % g == 0)
    grp_s = next(g for g in (2, 1) if nbs % g == 0)

    def cond_of_tile(i):
        return jnp.where(i < npt, 0, 1 + (i - npt) // tps)

    x = jnp.concatenate([x_prompt.reshape(n_p, d), x_sample.reshape(nbs * ts, d)], axis=0)
    conds = jnp.zeros((8, d), F32).at[0].set(c_ctx).at[1:1 + nbs].set(c)
    mod = _mod_call(conds, w_mod, b_mod)

    pl_ = jax.nn.softmax(c_lb.astype(F32), axis=0)
    lbs = jnp.cumsum(pl_, axis=0) - pl_[0:1]
    dqkv0 = COL['cg'] + HW
    da0 = dqkv0 + DQKV_W
    dg0 = da0 + 4 * D_HEADS
    w_in_p = jnp.concatenate([w_in[:, :, :dqkv0], w_in[:, :, dg0:dg0 + HW], w_in[:, :, dqkv0:da0],
                              w_in[:, :, da0:dg0],
                              jnp.zeros((depth, d, IN_W_PAD - dg0 - HW), w_in.dtype)], axis=2).astype(BF16)
    w_gate_b = w_gate.astype(BF16)
    w_branch_b = w_branch.astype(BF16)
    w_out_b = w_out.astype(BF16)
    w_router_p = jnp.zeros((depth, d, LANES), F32).at[:, :, :N_EXPERTS].set(w_router)
    b_router_p = jnp.full((depth, 1, LANES), NEG, F32).at[:, 0, :N_EXPERTS].set(b_router)

    tt = jnp.arange(ts, dtype=jnp.int32)
    half = HEAD_DIM // 2
    inv_freq = 1.0 / (ROPE_THETA ** (jnp.arange(0, half, 2, dtype=F32) / half))
    ang_r = (tt // GRID_W).astype(F32)[:, None] * inv_freq[None, :]
    ang_c = (tt % GRID_W).astype(F32)[:, None] * inv_freq[None, :]
    ang = jnp.concatenate([ang_r, ang_r, ang_c, ang_c], axis=-1)
    sign = jnp.asarray(np.where((np.arange(HEAD_DIM) % 32) < 16, -1.0, 1.0).astype(np.float32))
    cos_all = jnp.concatenate([jnp.ones((n_p, HEAD_DIM), F32), jnp.tile(jnp.cos(ang), (nbs, 1))], axis=0)
    sin_all = jnp.concatenate([jnp.zeros((n_p, HEAD_DIM), F32), jnp.tile(jnp.sin(ang) * sign, (nbs, 1))], axis=0)
    rope_tabs = (jnp.tile(cos_all, (1, 2)), jnp.tile(sin_all, (1, 2)))
    scale = HEAD_DIM ** -0.5
    wa, wb = A_KV_HEADS * HEAD_DIM, B_HEADS * HEAD_DIM

    cak = cache_a_k.reshape(nbs, depth, past, wa).astype(BF16)
    cav = _ones_extended(cache_a_v.reshape(nbs, depth, past, wa), A_KV_HEADS)
    cbk = cache_b_k.reshape(nbs, depth, past, wb).astype(BF16)
    cbv = cache_b_v.reshape(nbs, depth, past, wb).astype(BF16)
    nbr_bias = _nbr_bias_table(b_rpb)
    sc0 = _state_to_bd(state_hgrn)
    sd0 = _state_to_bd(state_delta)
    b_gu4 = b_gu.reshape(depth, N_EXPERTS, 1, b_gu.shape[-1])
    b_down4 = b_down.reshape(depth, N_EXPERTS, 1, d)

    ak_l, av_l, bk_l, bv_l, sc_l, sd_l = [], [], [], [], [], []
    for l in range(depth):
        mod3 = mod[l].reshape(8, 1, N_MOD * d)
        h = _norm_call(x, norm1[l].reshape(1, d), mod3, cond_of_tile)
        z = _mm_call(h, w_in_p, l)

        (qa,) = _prep_call(z, COL['aq'], A_HEADS * HEAD_DIM, a_qn[l], scale * LOG2E, rope_tabs, False)
        ka_n, ka = _prep_call(z, COL['ak'], wa, a_kn[l], 1.0, rope_tabs, True)
        va_ext = _ones_extended(z[:, COL['av']:COL['av'] + wa], A_KV_HEADS)
        oa_p = _attn_call(qa, ka, va_ext, 0, nbp, tp, A_HEADS, A_KV_HEADS, True)
        oa_s = _attn_call(qa, ka, va_ext, n_p, nbs, ts, A_HEADS, A_KV_HEADS, True, ctx=(cak, cav, l))

        (qb,) = _prep_call(z, COL['bq'], wb, b_qn[l], scale, None, False)
        kb_n, kb = _prep_call(z, COL['bk'], wb, b_kn[l], 1.0, None, True)
        ob_p = _attn_call(qb, kb, _ones_extended(z[:n_p, COL['bv']:COL['bv'] + wb], B_HEADS), 0, nbp, tp,
                          B_HEADS, B_HEADS, False)
        ob_s = _nbr_call(qb, kb, z, COL['bv'], n_p, nbs, ts, cbk, cbv, nbr_bias, l)

        cf_p, cb_p, sc_p = _hgrn_call(z, 0, nbp, tp, lbs[l], None, grp_p)
        cf_s, cb_s, _ = _hgrn_call(z, n_p, nbs, ts, lbs[l], (sc0, l), grp_s)
        oc = _gate_norm_call(cf_p, cf_s, cb_p, cb_s, z, COL['cg'], c_norm[l], npt)

        qkv, gb = _delta_prep_call(z, d_conv[l], d_alog[l], d_dtbias[l], npt, tps)
        df_p, db_p, sd_p = _delta_call(qkv, gb, 0, nbp, tp, None, grp_p)
        df_s, db_s, _ = _delta_call(qkv, gb, n_p, nbs, ts, (sd0, l), grp_s)
        od = _gate_norm_call(df_p, df_s, db_p, db_s, z, COL['dg'], d_norm[l], npt)

        x = _merge_call(x, h, oa_p, oa_s, ob_p, ob_s, oc, od, w_gate_b, w_branch_b, w_out_b, l, mod3, cond_of_tile,
                        npt)

        h_rows, gates, sel, counts = _router_call(x, norm2[l].reshape(1, d), mod3, cond_of_tile,
                                                  w_router_p, b_router_p, l)
        dest, block_e, n_used, next_e, last_blk, n_blocks = _slot_plan(sel, counts, n)
        x_slots = _dispatch_call(dest, last_blk, n_used, h_rows, n_blocks * MOE_TM)
        y_slots = _gmm_call(block_e, n_used, next_e, x_slots, w_gu, b_gu4, w_down, b_down4, l)
        x = _combine_call(dest, x, gates, mod3, cond_of_tile, y_slots)

        ak_l.append(ka_n[:n_p].reshape(nbp, tp, A_KV_HEADS, HEAD_DIM))
        av_l.append(z[:n_p, COL['av']:COL['av'] + wa].reshape(nbp, tp, A_KV_HEADS, HEAD_DIM))
        bk_l.append(kb_n[:n_p].reshape(nbp, tp, B_HEADS, HEAD_DIM))
        bv_l.append(z[:n_p, COL['bv']:COL['bv'] + wb].reshape(nbp, tp, B_HEADS, HEAD_DIM))
        sc_l.append(sc_p)
        sd_l.append(sd_p)

    return (x[:n_p].reshape(nbp, tp, d), x[n_p:].reshape(nbs, ts, d),
            jnp.stack(ak_l, axis=1), jnp.stack(av_l, axis=1), jnp.stack(bk_l, axis=1), jnp.stack(bv_l, axis=1),
            _state_from_bd(jnp.stack(sc_l, axis=1)), _state_from_bd(jnp.stack(sd_l, axis=1)))
```

```python
import functools

import numpy as np
import jax
import jax.numpy as jnp
from jax import lax
from jax.experimental import pallas as pl
from jax.experimental.pallas import tpu as pltpu

F32 = jnp.float32
BF16 = jnp.bfloat16
HIGHEST = lax.Precision.HIGHEST

GRID_W = 64
HEAD_DIM = 64
A_HEADS = 8
A_KV_HEADS = 2
B_HEADS = 4
NB_ROWS = 8
NB_COLS = 16
C_HEADS = 4
D_HEADS = 4
CONV_K = 5
CHUNK = 64
N_EXPERTS = 32
TOP_K = 4
SWIGLU_LIMIT = 7.0
SWIGLU_ALPHA = 1.702
ROPE_THETA = 10000.0
EPS = 1e-6
MAX_INPUT_GATE = 1.0 - 1e-6
N_MOD = 6
HW = 256

LANES = 128
VMEM_LIMIT = 56 * 1024 * 1024
NEG = -1e30

TM = 256
TM_MM = 512
TN_MM = 1024
TQ = 256
MOE_TM = 256

COL = dict(aq=0, ak=512, av=640, bq=768, bk=1024, bv=1280, cq=1536, ci=1792, cff=2048, cfb=2304,
           cg=2560, dg=2816, dq=3072, dk=3328, dv=3584, dab=3840)
IN_W_PAD = 4096


def _cparams(sem, vmem=VMEM_LIMIT):
    return pltpu.CompilerParams(dimension_semantics=sem, vmem_limit_bytes=vmem)


def _silu(x):
    return x * jax.nn.sigmoid(x)


def _dot(a, b):
    return jnp.dot(a, b, preferred_element_type=F32)


def _dot_t(a, b):
    return lax.dot_general(a, b, (((1,), (1,)), ((), ())), preferred_element_type=F32)


def _each(f, *lists):
    return [f(*a) for a in zip(*lists)]


def _mod_kernel(c_ref, w_ref, b_ref, o_ref):
    a = _silu(c_ref[...]).astype(BF16)
    o_ref[...] = _dot(a, w_ref[...].astype(BF16)) + b_ref[...]


def _mod_call(conds, w_mod, b_mod):
    depth, d, n = w_mod.shape
    tn = 1024
    return pl.pallas_call(
        _mod_kernel,
        grid=(depth, n // tn),
        in_specs=[pl.BlockSpec((8, d), lambda l, j: (0, 0)),
                  pl.BlockSpec((None, d, tn), lambda l, j: (l, 0, j)),
                  pl.BlockSpec((None, 1, tn), lambda l, j: (l, 0, j))],
        out_specs=pl.BlockSpec((None, 8, tn), lambda l, j: (l, 0, j)),
        out_shape=jax.ShapeDtypeStruct((depth, 8, n), F32),
        compiler_params=_cparams(("parallel", "parallel")),
        name="mod",
    )(conds, w_mod, b_mod.reshape(depth, 1, n))


def _mod_norm(x, g, m, k_shift, k_scale, d):
    y = x * lax.rsqrt(jnp.mean(x * x, axis=-1, keepdims=True) + EPS) * g
    return y * (1.0 + m[:, k_scale * d:(k_scale + 1) * d]) + m[:, k_shift * d:(k_shift + 1) * d]


def _norm_kernel(x_ref, g_ref, mod_ref, h_ref, *, d):
    h_ref[...] = _mod_norm(x_ref[...], g_ref[...], mod_ref[...], 0, 1, d).astype(BF16)


def _norm_call(x, g, mod3, cond_of_tile):
    n, d = x.shape
    return pl.pallas_call(
        functools.partial(_norm_kernel, d=d),
        grid=(n // TM,),
        in_specs=[pl.BlockSpec((TM, d), lambda i: (i, 0)),
                  pl.BlockSpec((1, d), lambda i: (0, 0)),
                  pl.BlockSpec((None, 1, N_MOD * d), lambda i: (cond_of_tile(i), 0, 0))],
        out_specs=pl.BlockSpec((TM, d), lambda i: (i, 0)),
        out_shape=jax.ShapeDtypeStruct((n, d), BF16),
        compiler_params=_cparams(("parallel",)),
        name="norm1",
    )(x, g, mod3)


def _mm_kernel(a_ref, w_ref, o_ref):
    o_ref[...] = _dot(a_ref[...], w_ref[...]).astype(o_ref.dtype)


def _mm_call(a, w, l, out_dtype=F32):
    m, k = a.shape
    n = w.shape[2]
    return pl.pallas_call(
        _mm_kernel,
        grid=(n // TN_MM, m // TM_MM),
        in_specs=[pl.BlockSpec((TM_MM, k), lambda j, i: (i, 0)),
                  pl.BlockSpec((None, k, TN_MM), lambda j, i: (l, 0, j))],
        out_specs=pl.BlockSpec((TM_MM, TN_MM), lambda j, i: (i, j)),
        out_shape=jax.ShapeDtypeStruct((m, n), out_dtype),
        compiler_params=_cparams(("parallel", "parallel")),
        name="in_proj",
    )(a, w)


def _two_path(i, n_first, first_ref, second_ref):
    return jnp.where(i < n_first, first_ref[...], second_ref[...])


def _two_path_specs(width, n_first):
    return [pl.BlockSpec((TM, width), lambda i: (jnp.minimum(i, n_first - 1), 0)),
            pl.BlockSpec((TM, width), lambda i: (jnp.maximum(i - n_first, 0), 0))]


def _merge_kernel(x_ref, h_ref, oap_ref, oas_ref, obp_ref, obs_ref, oc_ref, od_ref, wg_ref, wb_ref, wo_ref, mod_ref,
                  o_ref, *, d, n_first):
    i = pl.program_id(0)
    h = h_ref[...]
    outs = (_two_path(i, n_first, oap_ref, oas_ref), _two_path(i, n_first, obp_ref, obs_ref), oc_ref[...], od_ref[...])
    merged = None
    off = 0
    for j in range(4):
        wdt = outs[j].shape[1]
        gate = jax.nn.sigmoid(_dot(h, wg_ref[:, j * d:(j + 1) * d]))
        br = gate * _dot(outs[j], wb_ref[off:off + wdt, :])
        merged = br if merged is None else merged + br
        off += wdt
    y = _dot(merged.astype(BF16), wo_ref[...])
    o_ref[...] = x_ref[...] + mod_ref[...][:, 2 * d:3 * d] * y


def _merge_call(x, h, oa_p, oa_s, ob_p, ob_s, oc, od, w_gate, w_branch, w_out, l, mod3, cond_of_tile, n_first):
    n, d = x.shape
    row = lambda i: (i, 0)
    const = lambda i: (l, 0, 0)
    once = pl.Buffered(1)
    return pl.pallas_call(
        functools.partial(_merge_kernel, d=d, n_first=n_first),
        grid=(n // TM,),
        in_specs=[pl.BlockSpec((TM, d), row), pl.BlockSpec((TM, d), row)]
        + _two_path_specs(oa_p.shape[1], n_first) + _two_path_specs(ob_p.shape[1], n_first)
        + [pl.BlockSpec((TM, oc.shape[1]), row), pl.BlockSpec((TM, od.shape[1]), row),
           pl.BlockSpec((None,) + w_gate.shape[1:], const, pipeline_mode=once),
           pl.BlockSpec((None,) + w_branch.shape[1:], const, pipeline_mode=once),
           pl.BlockSpec((None,) + w_out.shape[1:], const, pipeline_mode=once),
           pl.BlockSpec((None, 1, N_MOD * d), lambda i: (cond_of_tile(i), 0, 0))],
        out_specs=pl.BlockSpec((TM, d), row),
        out_shape=jax.ShapeDtypeStruct((n, d), F32),
        compiler_params=_cparams(("parallel",)),
        name="merge",
    )(x, h, oa_p, oa_s, ob_p, ob_s, oc, od, w_gate, w_branch, w_out, mod3)


def _router_kernel(x_ref, g_ref, mod_ref, wr_ref, br_ref, ltri_ref, h_ref, gate_ref, sel_ref, cnt_ref, run_s, *, d):
    i = pl.program_id(0)

    @pl.when(i == 0)
    def _():
        run_s[...] = jnp.zeros(run_s.shape, F32)

    h2 = _mod_norm(x_ref[...], g_ref[...], mod_ref[...], 3, 4, d)
    nsub = d // LANES
    for sub in range(nsub):
        h_ref[pl.ds(sub, x_ref.shape[0], stride=nsub), :] = h2[:, sub * LANES:(sub + 1) * LANES]
    logits = jnp.dot(h2, wr_ref[...], preferred_element_type=F32, precision=HIGHEST) + br_ref[...]
    lane = lax.broadcasted_iota(jnp.int32, logits.shape, 1)
    vals, idxs = [], []
    cur = logits
    for _ in range(TOP_K):
        m = jnp.max(cur, axis=-1, keepdims=True)
        ix = jnp.min(jnp.where(cur == m, lane, LANES), axis=-1, keepdims=True)
        vals.append(m)
        idxs.append(ix)
        cur = jnp.where(lane == ix, -jnp.inf, cur)
    es = [jnp.exp(v - vals[0]) for v in vals]
    tot = es[0] + es[1] + es[2] + es[3]
    gates = jnp.zeros(logits.shape, F32)
    sel = jnp.zeros(logits.shape, jnp.int32)
    base = run_s[...]
    ltri = ltri_ref[...]
    for k in range(TOP_K):
        onehot = jnp.where(lane == idxs[k], 1.0, 0.0)
        before = base + _dot(ltri, onehot.astype(BF16))
        rank = jnp.sum(onehot * before, axis=-1, keepdims=True).astype(jnp.int32)
        base = base + jnp.sum(onehot, axis=0, keepdims=True)
        gates = jnp.where(lane == k, es[k] / tot, gates)
        sel = jnp.where(lane == k, idxs[k], sel)
        sel = jnp.where(lane == TOP_K + k, rank, sel)
    run_s[...] = base
    gate_ref[...] = gates
    sel_ref[...] = sel
    cnt_ref[...] = base


def _router_call(x, g, mod3, cond_of_tile, w_router_p, b_router_p, l):
    n, d = x.shape
    row = lambda i: (i, 0)
    const = lambda i: (0, 0)
    layer = lambda i: (l, 0, 0)
    ltri = jnp.asarray(np.tril(np.ones((TM, TM), np.float32), -1)).astype(BF16)
    return pl.pallas_call(
        functools.partial(_router_kernel, d=d),
        grid=(n // TM,),
        in_specs=[pl.BlockSpec((TM, d), row),
                  pl.BlockSpec((1, d), const),
                  pl.BlockSpec((None, 1, N_MOD * d), lambda i: (cond_of_tile(i), 0, 0)),
                  pl.BlockSpec((None, d, LANES), layer),
                  pl.BlockSpec((None, 1, LANES), layer),
                  pl.BlockSpec((TM, TM), const)],
        out_specs=[pl.BlockSpec((TM * (d // LANES), LANES), row), pl.BlockSpec((TM, LANES), row),
                   pl.BlockSpec((TM, LANES), row), pl.BlockSpec((1, LANES), const)],
        out_shape=[jax.ShapeDtypeStruct((n * (d // LANES), LANES), F32), jax.ShapeDtypeStruct((n, LANES), F32),
                   jax.ShapeDtypeStruct((n, LANES), jnp.int32), jax.ShapeDtypeStruct((1, LANES), F32)],
        scratch_shapes=[pltpu.VMEM((1, LANES), F32)],
        compiler_params=_cparams(("arbitrary",)),
        name="router",
    )(x, g, mod3, w_router_p, b_router_p, ltri)


def _slot_plan(sel, counts, n_tok):
    cnt = counts[0, :N_EXPERTS].astype(jnp.int32)
    padded = (cnt + MOE_TM - 1) // MOE_TM * MOE_TM
    pad_end = jnp.cumsum(padded)
    pad_start = pad_end - padded
    dest = (pad_start[sel[:, :TOP_K]] + sel[:, TOP_K:2 * TOP_K]).reshape(n_tok * TOP_K)
    n_blocks = n_tok * TOP_K // MOE_TM + N_EXPERTS
    blk = jnp.arange(n_blocks, dtype=jnp.int32)
    block_e = jnp.minimum(jnp.sum((pad_end[None, :] <= (blk * MOE_TM)[:, None]).astype(jnp.int32), axis=1),
                          N_EXPERTS - 1)
    n_used = (pad_end[-1:] // MOE_TM).astype(jnp.int32)
    later = (blk[None, :] > blk[:, None]) & (block_e[None, :] != block_e[:, None]) & (blk[None, :] < n_used[0])
    first_later = jnp.argmax(later, axis=1)
    next_e = jnp.where(jnp.any(later, axis=1), block_e[first_later], -1).astype(jnp.int32)
    last_blk = jnp.where(padded > 0, pad_end // MOE_TM - 1, -1).astype(jnp.int32)
    return dest, block_e, n_used, next_e, last_blk, n_blocks


ROW_TILE = 8


def _row_slice(ref, r):
    return ref.at[pl.ds(pl.multiple_of(r * ROW_TILE, ROW_TILE), ROW_TILE)]


def _dispatch_kernel(dest_ref, last_ref, nu_ref, h_ref, o_ref, zero_s, sem, zsem):
    rows = h_ref.shape[0]
    tm = rows // ROW_TILE
    grp = zero_s.shape[0]
    n_groups = o_ref.shape[0] // grp
    base = pl.program_id(0) * (tm * TOP_K)

    @pl.when(pl.program_id(0) == 0)
    def _():
        zero_s[...] = jnp.zeros(zero_s.shape, zero_s.dtype)

        def fill(group):
            return pltpu.make_async_copy(zero_s, o_ref.at[pl.ds(pl.multiple_of(group * grp, grp), grp)], zsem)

        targets = [(last_ref[e] >= 0, last_ref[e]) for e in range(N_EXPERTS)]
        targets += [(nu_ref[0] + j < n_groups, nu_ref[0] + j) for j in range(N_EXPERTS)]
        for do, group in targets:
            @pl.when(do)
            def _():
                fill(group).start()
        for do, group in targets:
            @pl.when(do)
            def _():
                fill(group).wait()

    def body(t, carry):
        for k in range(TOP_K):
            pltpu.make_async_copy(_row_slice(h_ref, t), _row_slice(o_ref, dest_ref[base + t * TOP_K + k]),
                                  sem).start()
        return carry

    lax.fori_loop(0, tm, body, 0)
    for _ in range(TOP_K):
        pltpu.make_async_copy(h_ref, o_ref.at[pl.ds(0, rows)], sem).wait()


def _dispatch_call(dest, last_blk, n_used, h_rows, n_slots):
    n = h_rows.shape[0] // ROW_TILE
    grid_spec = pltpu.PrefetchScalarGridSpec(
        num_scalar_prefetch=3,
        grid=(n // TM,),
        in_specs=[pl.BlockSpec((TM * ROW_TILE, LANES), lambda i, dst, lst, nu: (i, 0))],
        out_specs=pl.BlockSpec(memory_space=pl.ANY),
        scratch_shapes=[pltpu.VMEM((MOE_TM * ROW_TILE, LANES), F32), pltpu.SemaphoreType.DMA(()),
                        pltpu.SemaphoreType.DMA(())],
    )
    return pl.pallas_call(
        _dispatch_kernel,
        grid_spec=grid_spec,
        out_shape=jax.ShapeDtypeStruct((n_slots * ROW_TILE, LANES), F32),
        compiler_params=_cparams(("arbitrary",)),
        name="dispatch",
    )(dest, last_blk, n_used, h_rows)


def _gmm_kernel(be_ref, nu_ref, nxt_ref, x_ref, wgu_hbm, bgu_ref, wd_hbm, bd_ref, y_ref, wgu_f, wd_f, wgu_s, wd_s,
                sems, *, f, layer):
    i = pl.program_id(0)
    used = i < nu_ref[0]
    changed = (i == 0) | (be_ref[i] != be_ref[jnp.maximum(i - 1, 0)])
    tm = x_ref.shape[0] // ROW_TILE

    def weight_copies(e):
        return (pltpu.make_async_copy(wgu_hbm.at[layer, e], wgu_f, sems.at[0]),
                pltpu.make_async_copy(wd_hbm.at[layer, e], wd_f, sems.at[1]))

    @pl.when(used & (i == 0))
    def _():
        for cp in weight_copies(be_ref[0]):
            cp.start()

    @pl.when(used & changed)
    def _():
        for cp in weight_copies(be_ref[i]):
            cp.wait()
        wgu_s[...] = wgu_f[...].astype(BF16)
        wd_s[...] = wd_f[...].astype(BF16)

        @pl.when(nxt_ref[i] >= 0)
        def _():
            for cp in weight_copies(nxt_ref[i]):
                cp.start()

    @pl.when(used)
    def _():
        x = jnp.concatenate([x_ref[pl.ds(sub, tm, stride=ROW_TILE), :] for sub in range(ROW_TILE)], axis=1)
        gu = _dot(x.astype(BF16), wgu_s[...]) + bgu_ref[...]
        gate = jnp.minimum(gu[:, :f], SWIGLU_LIMIT)
        up = jnp.clip(gu[:, f:], -SWIGLU_LIMIT, SWIGLU_LIMIT)
        act = gate * jax.nn.sigmoid(SWIGLU_ALPHA * gate) * (up + 1.0)
        y = _dot(act.astype(BF16), wd_s[...]) + bd_ref[...]
        for sub in range(ROW_TILE):
            y_ref[pl.ds(sub, tm, stride=ROW_TILE), :] = y[:, sub * LANES:(sub + 1) * LANES]

    @pl.when(jnp.logical_not(used))
    def _():
        y_ref[...] = jnp.zeros(y_ref.shape, y_ref.dtype)


def _gmm_call(block_e, n_used, next_e, x_slots, w_gu, b_gu, w_down, b_down, l):
    n_slots = x_slots.shape[0] // ROW_TILE
    depth, e, d, f2 = w_gu.shape
    f = f2 // 2
    n_blocks = n_slots // MOE_TM
    xrow = lambda i, be, nu, nx: (jnp.minimum(i, nu[0] - 1), 0)
    yrow = lambda i, be, nu, nx: (i, 0)
    bsel = lambda i, be, nu, nx: (l, be[i], 0, 0)
    grid_spec = pltpu.PrefetchScalarGridSpec(
        num_scalar_prefetch=3,
        grid=(n_blocks,),
        in_specs=[pl.BlockSpec((MOE_TM * ROW_TILE, LANES), xrow),
                  pl.BlockSpec(memory_space=pl.ANY),
                  pl.BlockSpec((None, None, 1, f2), bsel),
                  pl.BlockSpec(memory_space=pl.ANY),
                  pl.BlockSpec((None, None, 1, d), bsel)],
        out_specs=pl.BlockSpec((MOE_TM * ROW_TILE, LANES), yrow),
        scratch_shapes=[pltpu.VMEM((d, f2), F32), pltpu.VMEM((f, d), F32), pltpu.VMEM((d, f2), BF16),
                        pltpu.VMEM((f, d), BF16), pltpu.SemaphoreType.DMA((2,))],
    )
    return pl.pallas_call(
        functools.partial(_gmm_kernel, f=f, layer=l),
        grid_spec=grid_spec,
        out_shape=jax.ShapeDtypeStruct((n_slots * ROW_TILE, LANES), F32),
        compiler_params=_cparams(("arbitrary",)),
        name="experts",
    )(block_e, n_used, next_e, x_slots, w_gu, b_gu, w_down, b_down)


def _combine_kernel(dest_ref, x_ref, gate_ref, mod_ref, y_ref, o_ref, buf, sems, *, d):
    tm = x_ref.shape[0]
    rows = tm * ROW_TILE
    i = pl.program_id(0)
    n_steps = pl.num_programs(0)

    def issue(step, slot):
        base = step * (tm * TOP_K)

        def body(t, carry):
            for k in range(TOP_K):
                dst = buf.at[pl.ds(pl.multiple_of((slot * TOP_K + k) * rows + t * ROW_TILE, ROW_TILE), ROW_TILE)]
                pltpu.make_async_copy(_row_slice(y_ref, dest_ref[base + t * TOP_K + k]), dst, sems.at[slot]).start()
            return carry

        lax.fori_loop(0, tm, body, 0)

    @pl.when(i == 0)
    def _():
        issue(0, 0)

    @pl.when(i + 1 < n_steps)
    def _():
        issue(i + 1, (i + 1) % 2)

    slot = i % 2
    for k in range(TOP_K):
        off = pl.multiple_of((slot * TOP_K + k) * rows, ROW_TILE)
        pltpu.make_async_copy(y_ref.at[pl.ds(0, rows)], buf.at[pl.ds(off, rows)], sems.at[slot]).wait()
    g = gate_ref[...]
    gk = [jnp.broadcast_to(g[:, k:k + 1], (tm, LANES)) for k in range(TOP_K)]
    g2 = mod_ref[...][:, 5 * d:6 * d]
    for sub in range(ROW_TILE):
        sl = slice(sub * LANES, (sub + 1) * LANES)
        acc = None
        for k in range(TOP_K):
            off = pl.multiple_of((slot * TOP_K + k) * rows, ROW_TILE)
            term = gk[k] * buf[pl.ds(off + sub, tm, stride=ROW_TILE), :]
            acc = term if acc is None else acc + term
        o_ref[:, sl] = x_ref[:, sl] + g2[:, sl] * acc


def _combine_call(dest, x, gates, mod3, cond_of_tile, y_slots):
    n, d = x.shape
    row = lambda i, dst: (i, 0)
    grid_spec = pltpu.PrefetchScalarGridSpec(
        num_scalar_prefetch=1,
        grid=(n // TM,),
        in_specs=[pl.BlockSpec((TM, d), row),
                  pl.BlockSpec((TM, LANES), row),
                  pl.BlockSpec((None, 1, N_MOD * d), lambda i, dst: (cond_of_tile(i), 0, 0)),
                  pl.BlockSpec(memory_space=pl.ANY)],
        out_specs=pl.BlockSpec((TM, d), row),
        scratch_shapes=[pltpu.VMEM((2 * TOP_K * TM * ROW_TILE, LANES), F32), pltpu.SemaphoreType.DMA((2,))],
    )
    return pl.pallas_call(
        functools.partial(_combine_kernel, d=d),
        grid_spec=grid_spec,
        out_shape=jax.ShapeDtypeStruct((n, d), F32),
        compiler_params=_cparams(("arbitrary",)),
        name="combine",
    )(dest, x, gates, mod3, y_slots)


def _seg_matrix(w):
    i = np.arange(w) // HEAD_DIM
    return jnp.asarray((i[:, None] == i[None, :]).astype(np.float32)).astype(BF16)


def _seg_sum(y, p):
    hi = y.astype(BF16)
    lo = (y - hi.astype(F32)).astype(BF16)
    return _dot(hi, p) + _dot(lo, p)


def _head_mean_sq(x, p):
    return _seg_sum(x * x, p) * (1.0 / HEAD_DIM)


def _prep_kernel(z_ref, nw_ref, p_ref, *rest, scale, rope, want_norm, with_v):
    if rope:
        cos_ref, sin_ref = rest[0], rest[1]
        rest = rest[2:]
    x = z_ref[...]
    w = x.shape[1]
    if with_v:
        v = rest[0][...].astype(BF16)
        ones = jnp.ones((v.shape[0], HEAD_DIM), BF16)
        pieces = []
        for hd in range(w // HEAD_DIM):
            pieces += [v[:, hd * HEAD_DIM:(hd + 1) * HEAD_DIM], ones]
        rest[-1][...] = jnp.concatenate(pieces, axis=1)
        rest = rest[1:-1]
    y = x * lax.rsqrt(_head_mean_sq(x, p_ref[...]) + EPS) * nw_ref[...]
    if want_norm:
        rest[0][...] = y
    out_ref = rest[-1]
    if rope:
        reps = w // LANES
        cos = jnp.concatenate([cos_ref[...]] * reps, axis=1) if reps > 1 else cos_ref[...]
        sin = jnp.concatenate([sin_ref[...]] * reps, axis=1) if reps > 1 else sin_ref[...]
        lane = lax.broadcasted_iota(jnp.int32, y.shape, 1)
        first = (lane % 32) < 16
        rot = jnp.where(first, pltpu.roll(y, w - 16, axis=1), pltpu.roll(y, 16, axis=1))
        y = y * cos + rot * sin
    out_ref[...] = (y * scale).astype(BF16)


def _prep_call(z, col, w, nw, scale, rope_tabs, want_norm, v_col=None):
    n = z.shape[0]
    cb = col // w
    row = lambda i: (i, 0)
    in_specs = [pl.BlockSpec((TM, w), lambda i: (i, cb)),
                pl.BlockSpec((1, w), lambda i: (0, 0)),
                pl.BlockSpec((w, w), lambda i: (0, 0))]
    args = [z, jnp.tile(nw, w // HEAD_DIM).reshape(1, w), _seg_matrix(w)]
    if rope_tabs is not None:
        in_specs += [pl.BlockSpec((TM, LANES), row), pl.BlockSpec((TM, LANES), row)]
        args += list(rope_tabs)
    if v_col is not None:
        in_specs.append(pl.BlockSpec((TM, w), lambda i: (i, v_col // w)))
        args.append(z)
    out_specs, out_shape = [], []
    if want_norm:
        out_specs.append(pl.BlockSpec((TM, w), row))
        out_shape.append(jax.ShapeDtypeStruct((n, w), F32))
    out_specs.append(pl.BlockSpec((TM, w), row))
    out_shape.append(jax.ShapeDtypeStruct((n, w), BF16))
    if v_col is not None:
        out_specs.append(pl.BlockSpec((TM, 2 * w), row))
        out_shape.append(jax.ShapeDtypeStruct((n, 2 * w), BF16))
    return pl.pallas_call(
        functools.partial(_prep_kernel, scale=scale, rope=rope_tabs is not None, want_norm=want_norm,
                          with_v=v_col is not None),
        grid=(n // TM,),
        in_specs=in_specs, out_specs=out_specs, out_shape=out_shape,
        compiler_params=_cparams(("parallel",)),
        name="qk_prep",
    )(*args)


KEY_BLOCK = 512
LOG2E = 1.4426950408889634


def _ones_extended(v, heads):
    vb = v.astype(BF16).reshape(v.shape[:-1] + (heads, HEAD_DIM))
    return jnp.concatenate([vb, jnp.ones_like(vb)], axis=-1).reshape(v.shape[:-1] + (heads * 2 * HEAD_DIM,))


def _attn_kernel(q_ref, k_ref, v_ref, *rest, hq, hk, has_ctx, base2):
    if has_ctx:
        ck_ref, cv_ref, o_ref = rest
    else:
        (o_ref,) = rest
    g = hq // hk
    tq = q_ref.shape[0]
    t = k_ref.shape[0]
    dh = HEAD_DIM
    kb = min(KEY_BLOCK, t)
    ex = jnp.exp2 if base2 else jnp.exp
    blocks = [(ck_ref, cv_ref, 0, ck_ref.shape[0])] if has_ctx else []
    blocks += [(k_ref, v_ref, s0, kb) for s0 in range(0, t, kb)]
    outs = []
    for j in range(hk):
        q4 = jnp.concatenate([q_ref[:, (j * g + gi) * dh:(j * g + gi + 1) * dh] for gi in range(g)], axis=0)
        m = acc = None
        for kref, vref, s0, n in blocks:
            s = _dot_t(q4, kref[s0:s0 + n, j * dh:(j + 1) * dh])
            bm = jnp.max(s, axis=-1, keepdims=True)
            m_new = bm if m is None else jnp.maximum(m, bm)
            pv = _dot(ex(s - m_new).astype(BF16), vref[s0:s0 + n, 2 * j * dh:2 * (j + 1) * dh])
            acc = pv if m is None else ex(m - m_new) * acc + pv
            m = m_new
        o = acc[:, :dh] / acc[:, dh:]
        outs += [o[gi * tq:(gi + 1) * tq] for gi in range(g)]
    o_ref[...] = jnp.concatenate(outs, axis=1).astype(o_ref.dtype)


def _attn_call(q, k, v_ext, row0, nb, t, hq, hk, base2, ctx=None):
    tq = min(TQ, t)
    wq, wk = hq * HEAD_DIM, hk * HEAD_DIM
    qb0, kb0 = row0 // tq, row0 // t
    nq = t // tq
    in_specs = [pl.BlockSpec((tq, wq), lambda b, i: (qb0 + b * nq + i, 0)),
                pl.BlockSpec((t, wk), lambda b, i: (kb0 + b, 0)),
                pl.BlockSpec((t, 2 * wk), lambda b, i: (kb0 + b, 0))]
    args = [q, k, v_ext]
    if ctx is not None:
        ck, cv, layer = ctx
        p = ck.shape[2]
        in_specs += [pl.BlockSpec((None, None, p, wk), lambda b, i: (b, layer, 0, 0)),
                     pl.BlockSpec((None, None, p, 2 * wk), lambda b, i: (b, layer, 0, 0))]
        args += [ck, cv]
    return pl.pallas_call(
        functools.partial(_attn_kernel, hq=hq, hk=hk, has_ctx=ctx is not None, base2=base2),
        grid=(nb, nq),
        in_specs=in_specs,
        out_specs=pl.BlockSpec((tq, wq), lambda b, i: (b * nq + i, 0)),
        out_shape=jax.ShapeDtypeStruct((nb * t, wq), BF16),
        compiler_params=_cparams(("parallel", "parallel")),
        name="attn",
    )(*args)


def _nbr_bias_table(rpb):
    o = np.arange(NB_ROWS)[:, None, None, None]
    cc = np.arange(GRID_W)[None, :, None, None]
    i = np.arange(NB_ROWS)[None, None, :, None]
    j = np.arange(GRID_W)[None, None, None, :]
    col_start = np.clip(cc - NB_COLS // 2, 0, GRID_W - NB_COLS)
    valid = (j >= col_start) & (j < col_start + NB_COLS)
    sel_r = ((i - o + NB_ROWS - 1)[..., None] == np.arange(2 * NB_ROWS - 1)).astype(np.float32)[:, 0, :, 0]
    sel_c = (((j - cc + NB_COLS - 1)[..., None] == np.arange(2 * NB_COLS - 1)) & valid[..., None])
    sel_c = sel_c.astype(np.float32)[0, :, 0]
    tab = jnp.einsum('lhrd,oir,cjd->lhocij', rpb.astype(F32), jnp.asarray(sel_r), jnp.asarray(sel_c),
                     precision=HIGHEST)
    tab = tab * LOG2E + jnp.asarray(np.where(valid, 0.0, NEG).astype(np.float32))[None, None, :, :, :, :]
    return tab.reshape(rpb.shape[0], rpb.shape[1], NB_ROWS, GRID_W, NB_ROWS * GRID_W)


def _nbr_kernel(q_ref, k_ref, v_ref, ck_ref, cv_ref, bias_ref, o_ref, *, rows):
    r = pl.program_id(1)
    rs = jnp.clip(r - NB_ROWS // 2, 0, rows - NB_ROWS)
    start = pl.multiple_of(rs * GRID_W, GRID_W)
    kw = k_ref[pl.ds(start, NB_ROWS * GRID_W), :]
    vw = v_ref[pl.ds(start, NB_ROWS * GRID_W), :]
    dh = HEAD_DIM
    sls = [slice(h * dh, (h + 1) * dh) for h in range(B_HEADS)]
    vsl = [slice(2 * h * dh, 2 * (h + 1) * dh) for h in range(B_HEADS)]
    rowmax = lambda x: jnp.max(x, axis=-1, keepdims=True)
    qs = [q_ref[:, sl] for sl in sls]
    s_loc = [_dot_t(q, kw[:, sl]) + bias_ref[h] for h, (q, sl) in enumerate(zip(qs, sls))]
    s_ctx = [_dot_t(q, ck_ref[:, sl]) for q, sl in zip(qs, sls)]
    m = _each(lambda a, b: jnp.maximum(rowmax(a), rowmax(b)), s_loc, s_ctx)
    p_loc = _each(lambda a, mm: jnp.exp2(a - mm).astype(BF16), s_loc, m)
    p_ctx = _each(lambda a, mm: jnp.exp2(a - mm).astype(BF16), s_ctx, m)
    o = [_dot(a, vw[:, sl]) + _dot(b, cv_ref[:, sl]) for a, b, sl in zip(p_loc, p_ctx, vsl)]
    o_ref[...] = jnp.concatenate([x[:, :dh] / x[:, dh:] for x in o], axis=1).astype(o_ref.dtype)


def _nbr_call(q, k, v_ext, row0, nb, t, ck, cv, bias, layer):
    rows = t // GRID_W
    assert rows >= NB_ROWS
    w = B_HEADS * HEAD_DIM
    p = ck.shape[2]
    nwin = NB_ROWS * GRID_W

    def bias_map(b, r):
        return (layer, 0, r - jnp.clip(r - NB_ROWS // 2, 0, rows - NB_ROWS), 0, 0)

    return pl.pallas_call(
        functools.partial(_nbr_kernel, rows=rows),
        grid=(nb, rows),
        in_specs=[pl.BlockSpec((GRID_W, w), lambda b, r: (row0 // GRID_W + b * rows + r, 0)),
                  pl.BlockSpec((t, w), lambda b, r: (row0 // t + b, 0)),
                  pl.BlockSpec((t, 2 * w), lambda b, r: (row0 // t + b, 0)),
                  pl.BlockSpec((None, None, p, w), lambda b, r: (b, layer, 0, 0)),
                  pl.BlockSpec((None, None, p, 2 * w), lambda b, r: (b, layer, 0, 0)),
                  pl.BlockSpec((None, B_HEADS, None, GRID_W, nwin), bias_map)],
        out_specs=pl.BlockSpec((GRID_W, w), lambda b, r: (b * rows + r, 0)),
        out_shape=jax.ShapeDtypeStruct((nb * t, w), BF16),
        compiler_params=_cparams(("parallel", "arbitrary")),
        name="nbr_attn",
    )(q, k, v_ext, ck, cv, bias)


def _head_mask():
    i = np.arange(HW) // HEAD_DIM
    return jnp.asarray((i[:, None] == i[None, :]).astype(np.float32))


def _scan_rows(rev):
    t = lax.broadcasted_iota(jnp.int32, (CHUNK, HW), 0)
    return (CHUNK - 1 - t) if rev else t


def _scan_lanes(rev):
    s = lax.broadcasted_iota(jnp.int32, (CHUNK, HW), 1) % CHUNK
    return (CHUNK - 1 - s) if rev else s


def _sh(x, d, rev):
    s = (-d if rev else d) % CHUNK
    return x if s == 0 else pltpu.roll(x, s, axis=0)


def _cumsum_scan(x, pt, rev):
    d = 1
    while d < CHUNK:
        x = x + jnp.where(pt >= d, _sh(x, d, rev), 0.0)
        d *= 2
    return x


def _block_end(x, pt, n, rev):
    r = (n - 1) - pt % n
    bit = 1
    while bit < n:
        x = jnp.where((r & bit) != 0, _sh(x, -bit, rev), x)
        bit *= 2
    return x


def _pick16(x, j, rev):
    jj = (15 - j) if rev else j
    x3 = x.reshape(CHUNK // 16, 16, HW)
    return jnp.broadcast_to(x3[:, jj:jj + 1, :], x3.shape).reshape(CHUNK, HW)


def _row(x, p, rev):
    t = (CHUNK - 1 - p) if rev else p
    return x[t:t + 1, :]


def _bd(x, mask_bf):
    xb = x.astype(BF16)
    return jnp.concatenate([xb] * (HW // CHUNK), axis=0) * mask_bf


def _gla_chunks(ops_list, lbs, st_refs, mask, revs):
    mask_bf = mask.astype(BF16)
    bf = lambda x: x.astype(BF16)

    def prepare(ops, lb, rev):
        q_raw, v, f_raw = ops
        pt = _scan_rows(rev)
        q = _silu(q_raw)
        kk = jnp.minimum((1.0 - lb) * jax.nn.sigmoid(-f_raw), MAX_INPUT_GATE)
        b = _cumsum_scan(jnp.log1p(-kk), pt, rev)
        e4 = _block_end(b, pt, 4, rev)
        r4 = jnp.where(pt >= 4, _sh(e4, 4, rev), 0.0)
        r16 = jnp.where(pt >= 16, _sh(_pick16(b, 15, rev), 16, rev), 0.0)
        ks01 = [_bd(kk * jnp.exp(r4 - b), mask_bf)]
        ks2 = []
        for i in range(1, 4):
            ks01.append(_bd(kk * jnp.exp(jnp.minimum(_pick16(r4, 4 * i, rev) - b, 0.0)), mask_bf))
            ks2.append(_bd(kk * jnp.exp(jnp.minimum(_row(b, 16 * i - 1, rev) - b, 0.0)), mask_bf))
        return dict(pt=pt, ps=_scan_lanes(rev), v=v, kk=kk, b=b, b_last=_row(b, CHUNK - 1, rev),
                    q0=bf(q * jnp.exp(b - r4)), q2=bf(q * jnp.exp(b - r16)), qe=bf(q * jnp.exp(b)),
                    k01=jnp.concatenate(ks01, axis=0), k2=jnp.concatenate(ks2, axis=0))

    def select(c, r01, r2):
        pt, ps = c['pt'], c['ps']
        pt4, ps4, pt16, ps16 = pt // 4, ps // 4, pt // 16, ps // 16
        attn = jnp.where((pt4 == ps4) & (ps <= pt), r01[:, :HW], 0.0)
        for i in range(1, 4):
            attn = jnp.where((pt16 == ps16) & (ps4 < pt4) & (pt4 % 4 == i), r01[:, i * HW:(i + 1) * HW], attn)
            attn = jnp.where((ps16 < pt16) & (pt16 == i), r2[:, (i - 1) * HW:i * HW], attn)
        return bf(attn)

    cs = _each(prepare, ops_list, lbs, revs)
    r01 = [_dot_t(c['q0'], c['k01']) for c in cs]
    r2 = [_dot_t(c['q2'], c['k2']) for c in cs]
    attn = _each(select, cs, r01, r2)
    st = [s[...] for s in st_refs]
    o = _each(lambda c, s, aa: _dot_t(c['qe'], bf(s)) + _dot(aa, _bd(c['v'], mask_bf)), cs, st, attn)
    upd = [lax.dot_general(bf(c['v']), bf(c['kk'] * jnp.exp(c['b_last'] - c['b'])), (((0,), (0,)), ((), ())),
                           preferred_element_type=F32) for c in cs]
    for s_ref, c, s, u in zip(st_refs, cs, st, upd):
        s_ref[...] = jnp.exp(c['b_last']) * s + mask * u
    return o


def _scan_kernel(*refs, chunk_fn, n_in, n_const, group, has_state):
    n_op = group * 2 * n_in
    ins, consts, rest = refs[:n_op], refs[n_op:n_op + n_const], refs[n_op + n_const:]
    if has_state:
        s0_ref, of_ref, ob_ref, sout_ref, st_s = rest
    else:
        of_ref, ob_ref, sout_ref, st_s = rest
    c = pl.program_id(1)

    @pl.when(c == 0)
    def _():
        st_s[...] = s0_ref[...] if has_state else jnp.zeros(st_s.shape, F32)

    chains = [(g, direction) for g in range(group) for direction in (0, 1)]
    ops_list = [[r[...] for r in ins[(g * 2 + dr) * n_in:(g * 2 + dr + 1) * n_in]] for g, dr in chains]
    outs = chunk_fn(ops_list, consts, [st_s.at[g, dr] for g, dr in chains], [dr for _, dr in chains])
    for (g, dr), o in zip(chains, outs):
        (ob_ref if dr else of_ref)[g] = o

    @pl.when(c == pl.num_programs(1) - 1)
    def _():
        sout_ref[...] = st_s[...]


def _scan_call(name, chunk_fn, ops_f, ops_b, consts, row0, nb, t, s0_bd, group):
    assert nb % group == 0 and len(ops_f) == len(ops_b)
    nc = t // CHUNK
    r0 = row0 // CHUNK
    has_state = s0_bd is not None
    in_specs, args = [], []
    for g in range(group):
        for direction, ops in ((0, ops_f), (1, ops_b)):
            for arr, w, cb in ops:
                if direction == 0:
                    imap = functools.partial(lambda b, c, g, cb: (r0 + (b * group + g) * nc + c, cb), g=g, cb=cb)
                else:
                    imap = functools.partial(lambda b, c, g, cb: (r0 + (b * group + g) * nc + nc - 1 - c, cb),
                                             g=g, cb=cb)
                in_specs.append(pl.BlockSpec((CHUNK, w), imap))
                args.append(arr)
    for cst in consts:
        in_specs.append(pl.BlockSpec(cst.shape, functools.partial(lambda b, c, nd: (0,) * nd, nd=cst.ndim)))
        args.append(cst)
    state_spec = pl.BlockSpec((group, 2, HW, HW), lambda b, c: (b, 0, 0, 0))
    if has_state:
        s0_all, layer = s0_bd
        in_specs.append(pl.BlockSpec((group, None, 2, HW, HW), lambda b, c: (b, layer, 0, 0, 0)))
        args.append(s0_all)
    o_f, o_b, s_out = pl.pallas_call(
        functools.partial(_scan_kernel, chunk_fn=chunk_fn, n_in=len(ops_f), n_const=len(consts), group=group,
                          has_state=has_state),
        grid=(nb // group, nc),
        in_specs=in_specs,
        out_specs=[pl.BlockSpec((group, CHUNK, HW), lambda b, c: (b, c, 0)),
                   pl.BlockSpec((group, CHUNK, HW), lambda b, c: (b, nc - 1 - c, 0)),
                   state_spec],
        out_shape=[jax.ShapeDtypeStruct((nb, t, HW), F32), jax.ShapeDtypeStruct((nb, t, HW), F32),
                   jax.ShapeDtypeStruct((nb, 2, HW, HW), F32)],
        scratch_shapes=[pltpu.VMEM((group, 2, HW, HW), F32)],
        compiler_params=_cparams(("parallel", "arbitrary")),
        name=name,
    )(*args)
    return o_f.reshape(nb * t, HW), o_b.reshape(nb * t, HW), s_out


def _hgrn_chunks_fn(ops_list, consts, st_refs, directions):
    lb_ref, mask_ref = consts
    return _gla_chunks(ops_list, [lb_ref[dr:dr + 1, :] for dr in directions], st_refs, mask_ref[...],
                       [dr == 1 for dr in directions])


def _state_to_bd(s):
    eye = jnp.eye(C_HEADS, dtype=F32)
    return jnp.einsum('...hkv,hg->...hvgk', s.astype(F32), eye).reshape(s.shape[:-3] + (HW, HW))


def _state_from_bd(sbd):
    s6 = sbd.reshape(sbd.shape[:-2] + (C_HEADS, HEAD_DIM, C_HEADS, HEAD_DIM))
    diag = jnp.stack([s6[..., h, :, h, :] for h in range(C_HEADS)], axis=-3)
    return jnp.swapaxes(diag, -1, -2)


def _hgrn_call(z, row0, nb, t, lb, s0_bd, group):
    ops = lambda fcol: [(z, HW, COL['cq'] // HW), (z, HW, COL['ci'] // HW), (z, HW, fcol // HW)]
    return _scan_call("hgrn_scan", _hgrn_chunks_fn, ops(COL['cff']), ops(COL['cfb']), [lb, _head_mask()],
                      row0, nb, t, s0_bd, group)


def _gate_norm_kernel(ofp_ref, ofs_ref, obp_ref, obs_ref, g_ref, nw_ref, p_ref, o_ref, *, n_first):
    i = pl.program_id(0)
    o = _two_path(i, n_first, ofp_ref, ofs_ref) + _two_path(i, n_first, obp_ref, obs_ref)
    y = o * lax.rsqrt(_head_mean_sq(o, p_ref[...]) + EPS) * nw_ref[...]
    o_ref[...] = (y * _silu(g_ref[...])).astype(o_ref.dtype)


def _gate_norm_call(of_p, of_s, ob_p, ob_s, z, gcol, nw, n_first):
    n = of_p.shape[0] + of_s.shape[0]
    row = lambda i: (i, 0)
    return pl.pallas_call(
        functools.partial(_gate_norm_kernel, n_first=n_first),
        grid=(n // TM,),
        in_specs=_two_path_specs(HW, n_first) + _two_path_specs(HW, n_first)
        + [pl.BlockSpec((TM, HW), lambda i: (i, gcol // HW)),
           pl.BlockSpec((1, HW), lambda i: (0, 0)),
           pl.BlockSpec((HW, HW), lambda i: (0, 0))],
        out_specs=pl.BlockSpec((TM, HW), row),
        out_shape=jax.ShapeDtypeStruct((n, HW), BF16),
        compiler_params=_cparams(("parallel",)),
        name="gate_norm",
    )(of_p, of_s, ob_p, ob_s, z, jnp.tile(nw, HW // HEAD_DIM).reshape(1, HW), _seg_matrix(HW))


DQKV_W = 3 * HW
HALO = 8


def _softplus(x):
    return jnp.maximum(x, 0.0) + jnp.log1p(jnp.exp(-jnp.abs(x)))


def _delta_prep_kernel(x_ref, xp_ref, xn_ref, cw_ref, dab_ref, ex_ref, al_ref, dtb_ref, p_ref, qkv_ref, gb_ref,
                       *, n_single, tiles_per_seq):
    i = pl.program_id(0)
    j = jnp.maximum(i - n_single, 0) % tiles_per_seq
    first = (i < n_single) | (j == 0)
    last = (i < n_single) | (j == tiles_per_seq - 1)
    x = x_ref[...]
    tm = x.shape[0]
    prev = jnp.where(first, 0.0, xp_ref[...])
    nxt = jnp.where(last, 0.0, xn_ref[...])
    xe = jnp.concatenate([prev, x, nxt], axis=0)
    y = None
    for tap in range(CONV_K):
        lo = HALO + tap - CONV_K // 2
        term = cw_ref[tap:tap + 1, :] * xe[lo:lo + tm]
        y = term if y is None else y + term
    y = _silu(y)
    qk = y[:, :2 * HW]
    ssq = _seg_sum(qk * qk, p_ref[...])
    qkn = qk * lax.rsqrt(ssq + EPS)
    qkv_ref[...] = jnp.concatenate([qkn[:, :HW] * (HEAD_DIM ** -0.5), qkn[:, HW:], y[:, 2 * HW:]], axis=1)
    e = jnp.dot(dab_ref[...], ex_ref[...], preferred_element_type=F32, precision=HIGHEST)
    g = -jnp.exp(al_ref[...]) * _softplus(e[:, :2 * HW] + dtb_ref[...])
    gb_ref[...] = jnp.concatenate([g, jax.nn.sigmoid(e[:, 2 * HW:])], axis=1)


def _delta_prep_call(z, conv_w, a_log, dt_bias, n_single, tiles_per_seq):
    n = z.shape[0]
    cb = COL['dq'] // DQKV_W
    hb = TM // HALO
    nhb = n // HALO
    ex = np.zeros((LANES, 4 * HW), np.float32)
    for r in range(4 * D_HEADS):
        ex[r, r * HEAD_DIM:(r + 1) * HEAD_DIM] = 1.0
    cw = jnp.zeros((8, DQKV_W), F32).at[:CONV_K].set(conv_w.astype(F32))
    const = lambda i: (0, 0)
    return pl.pallas_call(
        functools.partial(_delta_prep_kernel, n_single=n_single, tiles_per_seq=tiles_per_seq),
        grid=(n // TM,),
        in_specs=[pl.BlockSpec((TM, DQKV_W), lambda i: (i, cb)),
                  pl.BlockSpec((HALO, DQKV_W), lambda i: (jnp.maximum(i * hb - 1, 0), cb)),
                  pl.BlockSpec((HALO, DQKV_W), lambda i: (jnp.minimum((i + 1) * hb, nhb - 1), cb)),
                  pl.BlockSpec((8, DQKV_W), const),
                  pl.BlockSpec((TM, LANES), lambda i: (i, COL['dab'] // LANES)),
                  pl.BlockSpec((LANES, 4 * HW), const),
                  pl.BlockSpec((1, 2 * HW), const), pl.BlockSpec((1, 2 * HW), const),
                  pl.BlockSpec((2 * HW, 2 * HW), const)],
        out_specs=[pl.BlockSpec((TM, DQKV_W), lambda i: (i, 0)), pl.BlockSpec((TM, 4 * HW), lambda i: (i, 0))],
        out_shape=[jax.ShapeDtypeStruct((n, DQKV_W), F32), jax.ShapeDtypeStruct((n, 4 * HW), F32)],
        compiler_params=_cparams(("parallel",)),
        name="delta_prep",
    )(z, z, z, cw, z, jnp.asarray(ex),
      jnp.repeat(a_log.astype(F32).reshape(-1), HEAD_DIM).reshape(1, 2 * HW),
      jnp.repeat(dt_bias.astype(F32).reshape(-1), HEAD_DIM).reshape(1, 2 * HW), _seg_matrix(2 * HW))


def _delta_consts():
    t = np.arange(CHUNK)[:, None]
    s = (np.arange(HW) % CHUNK)[None, :]
    s16 = t // 16 == s // 16
    s32 = t // 32 == s // 32
    mats = [t == s, s16, s32 & ~s16, ~s32]
    return jnp.asarray(np.stack(mats).astype(np.float32))


def _tri_inv(acs, mask_bf, eye, m16, m32, m64):
    bf = lambda x: x.astype(BF16)
    mm = lambda x, y: _dot(bf(x), _bd(y, mask_bf))
    a16 = [a * m16 for a in acs]
    tin = [eye - a for a in a16]
    p = _each(mm, a16, a16)
    for step in range(3):
        tin = _each(lambda t, pp: t + mm(t, pp), tin, p)
        if step < 2:
            p = _each(lambda pp: mm(pp, pp), p)
    for m in (m32, m64):
        inner = _each(lambda a, t: mm(a * m, t), acs, tin)
        tin = _each(lambda t, i_n: t - mm(t, i_n), tin, inner)
    return tin


def _delta_chunks(ops_list, st_refs, mask_ref, cmask_ref, revs):
    mask = mask_ref[...]
    mask_bf = mask.astype(BF16)
    bf = lambda x: x.astype(BF16)
    pts = [_scan_rows(r) for r in revs]
    pss = [_scan_lanes(r) for r in revs]
    qs = [o[0][:, :HW] for o in ops_list]
    ks = [o[0][:, HW:2 * HW] for o in ops_list]
    vs = [o[0][:, 2 * HW:] for o in ops_list]
    betas = [o[2] for o in ops_list]
    gcs = _each(_cumsum_scan, [o[1] for o in ops_list], pts, revs)
    t_idx = lax.broadcasted_iota(jnp.int32, (CHUNK, HW), 0)
    s_idx = lax.broadcasted_iota(jnp.int32, (CHUNK, HW), 1) % CHUNK
    diag = t_idx == s_idx
    gcl = [jnp.sum(jnp.where(diag, gc, 0.0), axis=0, keepdims=True) for gc in gcs]
    gam = _each(lambda gc, gl: jnp.exp(jnp.minimum(gc - gl, 0.0)), gcs, gcl)
    g_last = _each(lambda gc, r: _row(gc, CHUNK - 1, r), gcs, revs)
    egc = [jnp.exp(gc) for gc in gcs]
    kb = _each(lambda k, b: k * b, ks, betas)
    r = _each(lambda k_b, q, k: _dot_t(bf(jnp.concatenate([k_b, q], axis=0)), _bd(k, mask_bf)), kb, qs, ks)
    a = _each(lambda rr, gm, ps, pt: jnp.where(ps < pt, rr[:CHUNK] * gm, 0.0), r, gam, pss, pts)
    aq = _each(lambda rr, gm, ps, pt: jnp.where(ps <= pt, rr[CHUNK:] * gm, 0.0), r, gam, pss, pts)
    tin = _tri_inv(a, mask_bf, cmask_ref[0], cmask_ref[1], cmask_ref[2], cmask_ref[3])
    rhs = _each(lambda v, b, k_b, e: jnp.concatenate([_bd(v * b, mask_bf), _bd(k_b * e, mask_bf)], axis=1),
                vs, betas, kb, egc)
    uw = _each(lambda t, rh: _dot(bf(t), rh), tin, rhs)
    st = [s[...] for s in st_refs]
    ws_qs = _each(lambda x, q, e, s: _dot_t(bf(jnp.concatenate([x[:, HW:], q * e], axis=0)), bf(s)),
                  uw, qs, egc, st)
    v_new = _each(lambda x, y: x[:, :HW] - y[:CHUNK], uw, ws_qs)
    o = _each(lambda y, aa, vn: y[CHUNK:] + _dot(bf(aa), _bd(vn, mask_bf)), ws_qs, aq, v_new)
    ke = _each(lambda k, gl, gc: bf(k * jnp.exp(gl - gc)), ks, g_last, gcs)
    upd = _each(lambda vn, kk: lax.dot_general(bf(vn), kk, (((0,), (0,)), ((), ())), preferred_element_type=F32),
                v_new, ke)
    for s_ref, gl, s, u in zip(st_refs, g_last, st, upd):
        s_ref[...] = jnp.exp(gl) * s + mask * u
    return o


def _delta_chunks_fn(ops_list, consts, st_refs, directions):
    return _delta_chunks(ops_list, st_refs, consts[0], consts[1], [d == 1 for d in directions])


def _delta_call(qkv, gb, row0, nb, t, s0_bd, group):
    ops_f = [(qkv, DQKV_W, 0), (gb, HW, 0), (gb, HW, 2)]
    ops_b = [(qkv, DQKV_W, 0), (gb, HW, 1), (gb, HW, 3)]
    return _scan_call("delta_scan", _delta_chunks_fn, ops_f, ops_b, [_head_mask(), _delta_consts()], row0, nb, t, s0_bd, group)


def kernel(x_prompt, x_sample, cache_a_k, cache_a_v, cache_b_k, cache_b_v, state_hgrn, state_delta, c, c_ctx, w_mod, b_mod, norm1, norm2, w_in, w_gate, a_qn, a_kn, b_qn, b_kn, b_rpb, c_lb, c_norm, d_conv, d_alog, d_dtbias, d_norm, w_branch, w_out, w_router, b_router, w_gu, b_gu, w_down, b_down):
    nbp, tp, d = x_prompt.shape
    nbs, ts, _ = x_sample.shape
    depth = w_mod.shape[0]
    past = cache_a_k.shape[2]
    n_p = nbp * tp
    n = n_p + nbs * ts
    assert tp == TM and ts % TM == 0 and n_p % ts == 0 and 1 + nbs <= 8 and n % TM_MM == 0
    assert d == ROW_TILE * LANES
    npt, tps = n_p // TM, ts // TM
    grp_p = next(g for g in (4, 2, 1) if nbp % g == 0)
    grp_s = next(g for g in (2, 1) if nbs % g == 0)

    def cond_of_tile(i):
        return jnp.where(i < npt, 0, 1 + (i - npt) // tps)

    x = jnp.concatenate([x_prompt.reshape(n_p, d), x_sample.reshape(nbs * ts, d)], axis=0)
    conds = jnp.zeros((8, d), F32).at[0].set(c_ctx).at[1:1 + nbs].set(c)
    mod = _mod_call(conds, w_mod, b_mod)

    pl_ = jax.nn.softmax(c_lb.astype(F32), axis=0)
    lbs = jnp.cumsum(pl_, axis=0) - pl_[0:1]
    dqkv0 = COL['cg'] + HW
    da0 = dqkv0 + DQKV_W
    dg0 = da0 + 4 * D_HEADS
    w_in_p = jnp.concatenate([w_in[:, :, :dqkv0], w_in[:, :, dg0:dg0 + HW], w_in[:, :, dqkv0:da0],
                              w_in[:, :, da0:dg0],
                              jnp.zeros((depth, d, IN_W_PAD - dg0 - HW), w_in.dtype)], axis=2).astype(BF16)
    w_gate_b = w_gate.astype(BF16)
    w_branch_b = w_branch.astype(BF16)
    w_out_b = w_out.astype(BF16)
    w_router_p = jnp.zeros((depth, d, LANES), F32).at[:, :, :N_EXPERTS].set(w_router)
    b_router_p = jnp.full((depth, 1, LANES), NEG, F32).at[:, 0, :N_EXPERTS].set(b_router)

    tt = jnp.arange(ts, dtype=jnp.int32)
    half = HEAD_DIM // 2
    inv_freq = 1.0 / (ROPE_THETA ** (jnp.arange(0, half, 2, dtype=F32) / half))
    ang_r = (tt // GRID_W).astype(F32)[:, None] * inv_freq[None, :]
    ang_c = (tt % GRID_W).astype(F32)[:, None] * inv_freq[None, :]
    ang = jnp.concatenate([ang_r, ang_r, ang_c, ang_c], axis=-1)
    sign = jnp.asarray(np.where((np.arange(HEAD_DIM) % 32) < 16, -1.0, 1.0).astype(np.float32))
    cos_all = jnp.concatenate([jnp.ones((n_p, HEAD_DIM), F32), jnp.tile(jnp.cos(ang), (nbs, 1))], axis=0)
    sin_all = jnp.concatenate([jnp.zeros((n_p, HEAD_DIM), F32), jnp.tile(jnp.sin(ang) * sign, (nbs, 1))], axis=0)
    rope_tabs = (jnp.tile(cos_all, (1, 2)), jnp.tile(sin_all, (1, 2)))
    scale = HEAD_DIM ** -0.5
    wa, wb = A_KV_HEADS * HEAD_DIM, B_HEADS * HEAD_DIM

    cak = cache_a_k.reshape(nbs, depth, past, wa).astype(BF16)
    cav = _ones_extended(cache_a_v.reshape(nbs, depth, past, wa), A_KV_HEADS)
    cbk = cache_b_k.reshape(nbs, depth, past, wb).astype(BF16)
    cbv = _ones_extended(cache_b_v.reshape(nbs, depth, past, wb), B_HEADS)
    nbr_bias = _nbr_bias_table(b_rpb)
    sc0 = _state_to_bd(state_hgrn)
    sd0 = _state_to_bd(state_delta)
    b_gu4 = b_gu.reshape(depth, N_EXPERTS, 1, b_gu.shape[-1])
    b_down4 = b_down.reshape(depth, N_EXPERTS, 1, d)

    ak_l, av_l, bk_l, bv_l, sc_l, sd_l = [], [], [], [], [], []
    for l in range(depth):
        mod3 = mod[l].reshape(8, 1, N_MOD * d)
        h = _norm_call(x, norm1[l].reshape(1, d), mod3, cond_of_tile)
        z = _mm_call(h, w_in_p, l)

        (qa,) = _prep_call(z, COL['aq'], A_HEADS * HEAD_DIM, a_qn[l], scale * LOG2E, rope_tabs, False)
        ka_n, ka, va_ext = _prep_call(z, COL['ak'], wa, a_kn[l], 1.0, rope_tabs, True, v_col=COL['av'])
        oa_p = _attn_call(qa, ka, va_ext, 0, nbp, tp, A_HEADS, A_KV_HEADS, True)
        oa_s = _attn_call(qa, ka, va_ext, n_p, nbs, ts, A_HEADS, A_KV_HEADS, True, ctx=(cak, cav, l))

        (qb,) = _prep_call(z, COL['bq'], wb, b_qn[l], scale * LOG2E, None, False)
        kb_n, kb, vb_ext = _prep_call(z, COL['bk'], wb, b_kn[l], 1.0, None, True, v_col=COL['bv'])
        ob_p = _attn_call(qb, kb, vb_ext, 0, nbp, tp, B_HEADS, B_HEADS, True)
        ob_s = _nbr_call(qb, kb, vb_ext, n_p, nbs, ts, cbk, cbv, nbr_bias, l)

        cf_p, cb_p, sc_p = _hgrn_call(z, 0, nbp, tp, lbs[l], None, grp_p)
        cf_s, cb_s, _ = _hgrn_call(z, n_p, nbs, ts, lbs[l], (sc0, l), grp_s)
        oc = _gate_norm_call(cf_p, cf_s, cb_p, cb_s, z, COL['cg'], c_norm[l], npt)

        qkv, gb = _delta_prep_call(z, d_conv[l], d_alog[l], d_dtbias[l], npt, tps)
        df_p, db_p, sd_p = _delta_call(qkv, gb, 0, nbp, tp, None, grp_p)
        df_s, db_s, _ = _delta_call(qkv, gb, n_p, nbs, ts, (sd0, l), grp_s)
        od = _gate_norm_call(df_p, df_s, db_p, db_s, z, COL['dg'], d_norm[l], npt)

        x = _merge_call(x, h, oa_p, oa_s, ob_p, ob_s, oc, od, w_gate_b, w_branch_b, w_out_b, l, mod3, cond_of_tile,
                        npt)

        h_rows, gates, sel, counts = _router_call(x, norm2[l].reshape(1, d), mod3, cond_of_tile,
                                                  w_router_p, b_router_p, l)
        dest, block_e, n_used, next_e, last_blk, n_blocks = _slot_plan(sel, counts, n)
        x_slots = _dispatch_call(dest, last_blk, n_used, h_rows, n_blocks * MOE_TM)
        y_slots = _gmm_call(block_e, n_used, next_e, x_slots, w_gu, b_gu4, w_down, b_down4, l)
        x = _combine_call(dest, x, gates, mod3, cond_of_tile, y_slots)

        ak_l.append(ka_n[:n_p].reshape(nbp, tp, A_KV_HEADS, HEAD_DIM))
        av_l.append(z[:n_p, COL['av']:COL['av'] + wa].reshape(nbp, tp, A_KV_HEADS, HEAD_DIM))
        bk_l.append(kb_n[:n_p].reshape(nbp, tp, B_HEADS, HEAD_DIM))
        bv_l.append(z[:n_p, COL['bv']:COL['bv'] + wb].reshape(nbp, tp, B_HEADS, HEAD_DIM))
        sc_l.append(sc_p)
        sd_l.append(sd_p)

    return (x[:n_p].reshape(nbp, tp, d), x[n_p:].reshape(nbs, ts, d),
            jnp.stack(ak_l, axis=1), jnp.stack(av_l, axis=1), jnp.stack(bk_l, axis=1), jnp.stack(bv_l, axis=1),
            _state_from_bd(jnp.stack(sc_l, axis=1)), _state_from_bd(jnp.stack(sd_l, axis=1)))
```

```python
import functools

import numpy as np
import jax
import jax.numpy as jnp
from jax import lax
from jax.experimental import pallas as pl
from jax.experimental.pallas import tpu as pltpu

F32 = jnp.float32
BF16 = jnp.bfloat16
HIGHEST = lax.Precision.HIGHEST

GRID_W = 64
HEAD_DIM = 64
A_HEADS = 8
A_KV_HEADS = 2
B_HEADS = 4
NB_ROWS = 8
NB_COLS = 16
C_HEADS = 4
D_HEADS = 4
CONV_K = 5
CHUNK = 64
N_EXPERTS = 32
TOP_K = 4
SWIGLU_LIMIT = 7.0
SWIGLU_ALPHA = 1.702
ROPE_THETA = 10000.0
EPS = 1e-6
MAX_INPUT_GATE = 1.0 - 1e-6
N_MOD = 6
HW = 256

LANES = 128
VMEM_LIMIT = 56 * 1024 * 1024
NEG = -1e30

TM = 256
PREP_TM = 1024
TM_MM = 512
TN_MM = 1024
TQ = 256
MOE_TM = 256

COL = dict(aq=0, ak=512, av=640, bq=768, bk=1024, bv=1280, cq=1536, ci=1792, cff=2048, cfb=2304,
           cg=2560, dg=2816, dq=3072, dk=3328, dv=3584, dab=3840)
IN_W_PAD = 4096


def _cparams(sem, vmem=VMEM_LIMIT):
    return pltpu.CompilerParams(dimension_semantics=sem, vmem_limit_bytes=vmem)


def _silu(x):
    return x * jax.nn.sigmoid(x)


def _dot(a, b):
    return jnp.dot(a, b, preferred_element_type=F32)


def _dot_t(a, b):
    return lax.dot_general(a, b, (((1,), (1,)), ((), ())), preferred_element_type=F32)


def _each(f, *lists):
    return [f(*a) for a in zip(*lists)]


def _mod_kernel(c_ref, w_ref, b_ref, o_ref):
    a = _silu(c_ref[...]).astype(BF16)
    o_ref[...] = _dot(a, w_ref[...].astype(BF16)) + b_ref[...]


def _mod_call(conds, w_mod, b_mod):
    depth, d, n = w_mod.shape
    tn = 1024
    return pl.pallas_call(
        _mod_kernel,
        grid=(depth, n // tn),
        in_specs=[pl.BlockSpec((8, d), lambda l, j: (0, 0)),
                  pl.BlockSpec((None, d, tn), lambda l, j: (l, 0, j)),
                  pl.BlockSpec((None, 1, tn), lambda l, j: (l, 0, j))],
        out_specs=pl.BlockSpec((None, 8, tn), lambda l, j: (l, 0, j)),
        out_shape=jax.ShapeDtypeStruct((depth, 8, n), F32),
        compiler_params=_cparams(("parallel", "parallel")),
        name="mod",
    )(conds, w_mod, b_mod.reshape(depth, 1, n))


def _mod_norm(x, g, m, k_shift, k_scale, d):
    y = x * lax.rsqrt(jnp.mean(x * x, axis=-1, keepdims=True) + EPS) * g
    return y * (1.0 + m[:, k_scale * d:(k_scale + 1) * d]) + m[:, k_shift * d:(k_shift + 1) * d]


def _norm_kernel(x_ref, g_ref, mod_ref, h_ref, *, d):
    h_ref[...] = _mod_norm(x_ref[...], g_ref[...], mod_ref[...], 0, 1, d).astype(BF16)


def _norm_call(x, g, mod3, cond_of_tile):
    n, d = x.shape
    return pl.pallas_call(
        functools.partial(_norm_kernel, d=d),
        grid=(n // TM,),
        in_specs=[pl.BlockSpec((TM, d), lambda i: (i, 0)),
                  pl.BlockSpec((1, d), lambda i: (0, 0)),
                  pl.BlockSpec((None, 1, N_MOD * d), lambda i: (cond_of_tile(i), 0, 0))],
        out_specs=pl.BlockSpec((TM, d), lambda i: (i, 0)),
        out_shape=jax.ShapeDtypeStruct((n, d), BF16),
        compiler_params=_cparams(("parallel",)),
        name="norm1",
    )(x, g, mod3)


def _mm_kernel(a_ref, w_ref, o_ref):
    o_ref[...] = _dot(a_ref[...], w_ref[...]).astype(o_ref.dtype)


def _mm_call(a, w, l, out_dtype=F32):
    m, k = a.shape
    n = w.shape[2]
    return pl.pallas_call(
        _mm_kernel,
        grid=(n // TN_MM, m // TM_MM),
        in_specs=[pl.BlockSpec((TM_MM, k), lambda j, i: (i, 0)),
                  pl.BlockSpec((None, k, TN_MM), lambda j, i: (l, 0, j))],
        out_specs=pl.BlockSpec((TM_MM, TN_MM), lambda j, i: (i, j)),
        out_shape=jax.ShapeDtypeStruct((m, n), out_dtype),
        compiler_params=_cparams(("parallel", "parallel")),
        name="in_proj",
    )(a, w)


def _two_path(i, n_first, first_ref, second_ref):
    return jnp.where(i < n_first, first_ref[...], second_ref[...])


def _two_path_specs(width, n_first):
    return [pl.BlockSpec((TM, width), lambda i: (jnp.minimum(i, n_first - 1), 0)),
            pl.BlockSpec((TM, width), lambda i: (jnp.maximum(i - n_first, 0), 0))]


def _merge_kernel(x_ref, h_ref, oap_ref, oas_ref, obp_ref, obs_ref, oc_ref, od_ref, wg_ref, wb_ref, wo_ref, mod_ref,
                  o_ref, *, d, n_first):
    i = pl.program_id(0)
    h = h_ref[...]
    outs = (_two_path(i, n_first, oap_ref, oas_ref), _two_path(i, n_first, obp_ref, obs_ref), oc_ref[...], od_ref[...])
    merged = None
    off = 0
    for j in range(4):
        wdt = outs[j].shape[1]
        gate = jax.nn.sigmoid(_dot(h, wg_ref[:, j * d:(j + 1) * d]))
        br = gate * _dot(outs[j], wb_ref[off:off + wdt, :])
        merged = br if merged is None else merged + br
        off += wdt
    y = _dot(merged.astype(BF16), wo_ref[...])
    o_ref[...] = x_ref[...] + mod_ref[...][:, 2 * d:3 * d] * y


def _merge_call(x, h, oa_p, oa_s, ob_p, ob_s, oc, od, w_gate, w_branch, w_out, l, mod3, cond_of_tile, n_first):
    n, d = x.shape
    row = lambda i: (i, 0)
    const = lambda i: (l, 0, 0)
    once = pl.Buffered(1)
    return pl.pallas_call(
        functools.partial(_merge_kernel, d=d, n_first=n_first),
        grid=(n // TM,),
        in_specs=[pl.BlockSpec((TM, d), row), pl.BlockSpec((TM, d), row)]
        + _two_path_specs(oa_p.shape[1], n_first) + _two_path_specs(ob_p.shape[1], n_first)
        + [pl.BlockSpec((TM, oc.shape[1]), row), pl.BlockSpec((TM, od.shape[1]), row),
           pl.BlockSpec((None,) + w_gate.shape[1:], const, pipeline_mode=once),
           pl.BlockSpec((None,) + w_branch.shape[1:], const, pipeline_mode=once),
           pl.BlockSpec((None,) + w_out.shape[1:], const, pipeline_mode=once),
           pl.BlockSpec((None, 1, N_MOD * d), lambda i: (cond_of_tile(i), 0, 0))],
        out_specs=pl.BlockSpec((TM, d), row),
        out_shape=jax.ShapeDtypeStruct((n, d), F32),
        compiler_params=_cparams(("parallel",)),
        name="merge",
    )(x, h, oa_p, oa_s, ob_p, ob_s, oc, od, w_gate, w_branch, w_out, mod3)


def _router_kernel(x_ref, g_ref, mod_ref, wr_ref, br_ref, ltri_ref, h_ref, gate_ref, sel_ref, cnt_ref, run_s, *, d):
    i = pl.program_id(0)

    @pl.when(i == 0)
    def _():
        run_s[...] = jnp.zeros(run_s.shape, F32)

    h2 = _mod_norm(x_ref[...], g_ref[...], mod_ref[...], 3, 4, d)
    nsub = d // LANES
    for sub in range(nsub):
        h_ref[pl.ds(sub, x_ref.shape[0], stride=nsub), :] = h2[:, sub * LANES:(sub + 1) * LANES]
    logits = jnp.dot(h2, wr_ref[...], preferred_element_type=F32, precision=HIGHEST) + br_ref[...]
    lane = lax.broadcasted_iota(jnp.int32, logits.shape, 1)
    vals, idxs = [], []
    cur = logits
    for _ in range(TOP_K):
        m = jnp.max(cur, axis=-1, keepdims=True)
        ix = jnp.min(jnp.where(cur == m, lane, LANES), axis=-1, keepdims=True)
        vals.append(m)
        idxs.append(ix)
        cur = jnp.where(lane == ix, -jnp.inf, cur)
    es = [jnp.exp(v - vals[0]) for v in vals]
    tot = es[0] + es[1] + es[2] + es[3]
    gates = jnp.zeros(logits.shape, F32)
    sel = jnp.zeros(logits.shape, jnp.int32)
    base = run_s[...]
    ltri = ltri_ref[...]
    for k in range(TOP_K):
        onehot = jnp.where(lane == idxs[k], 1.0, 0.0)
        before = base + _dot(ltri, onehot.astype(BF16))
        rank = jnp.sum(onehot * before, axis=-1, keepdims=True).astype(jnp.int32)
        base = base + jnp.sum(onehot, axis=0, keepdims=True)
        gates = jnp.where(lane == k, es[k] / tot, gates)
        sel = jnp.where(lane == k, idxs[k], sel)
        sel = jnp.where(lane == TOP_K + k, rank, sel)
    run_s[...] = base
    gate_ref[...] = gates
    sel_ref[...] = sel
    cnt_ref[...] = base


def _router_call(x, g, mod3, cond_of_tile, w_router_p, b_router_p, l):
    n, d = x.shape
    row = lambda i: (i, 0)
    const = lambda i: (0, 0)
    layer = lambda i: (l, 0, 0)
    ltri = jnp.asarray(np.tril(np.ones((TM, TM), np.float32), -1)).astype(BF16)
    return pl.pallas_call(
        functools.partial(_router_kernel, d=d),
        grid=(n // TM,),
        in_specs=[pl.BlockSpec((TM, d), row),
                  pl.BlockSpec((1, d), const),
                  pl.BlockSpec((None, 1, N_MOD * d), lambda i: (cond_of_tile(i), 0, 0)),
                  pl.BlockSpec((None, d, LANES), layer),
                  pl.BlockSpec((None, 1, LANES), layer),
                  pl.BlockSpec((TM, TM), const)],
        out_specs=[pl.BlockSpec((TM * (d // LANES), LANES), row), pl.BlockSpec((TM, LANES), row),
                   pl.BlockSpec((TM, LANES), row), pl.BlockSpec((1, LANES), const)],
        out_shape=[jax.ShapeDtypeStruct((n * (d // LANES), LANES), F32), jax.ShapeDtypeStruct((n, LANES), F32),
                   jax.ShapeDtypeStruct((n, LANES), jnp.int32), jax.ShapeDtypeStruct((1, LANES), F32)],
        scratch_shapes=[pltpu.VMEM((1, LANES), F32)],
        compiler_params=_cparams(("arbitrary",)),
        name="router",
    )(x, g, mod3, w_router_p, b_router_p, ltri)


def _slot_plan(sel, counts, n_tok):
    cnt = counts[0, :N_EXPERTS].astype(jnp.int32)
    padded = (cnt + MOE_TM - 1) // MOE_TM * MOE_TM
    pad_end = jnp.cumsum(padded)
    pad_start = pad_end - padded
    dest = (pad_start[sel[:, :TOP_K]] + sel[:, TOP_K:2 * TOP_K]).reshape(n_tok * TOP_K)
    n_blocks = n_tok * TOP_K // MOE_TM + N_EXPERTS
    blk = jnp.arange(n_blocks, dtype=jnp.int32)
    block_e = jnp.minimum(jnp.sum((pad_end[None, :] <= (blk * MOE_TM)[:, None]).astype(jnp.int32), axis=1),
                          N_EXPERTS - 1)
    n_used = (pad_end[-1:] // MOE_TM).astype(jnp.int32)
    later = (blk[None, :] > blk[:, None]) & (block_e[None, :] != block_e[:, None]) & (blk[None, :] < n_used[0])
    first_later = jnp.argmax(later, axis=1)
    next_e = jnp.where(jnp.any(later, axis=1), block_e[first_later], -1).astype(jnp.int32)
    last_blk = jnp.where(padded > 0, pad_end // MOE_TM - 1, -1).astype(jnp.int32)
    return dest, block_e, n_used, next_e, last_blk, n_blocks


ROW_TILE = 8


def _row_slice(ref, r):
    return ref.at[pl.ds(pl.multiple_of(r * ROW_TILE, ROW_TILE), ROW_TILE)]


def _dispatch_kernel(dest_ref, last_ref, nu_ref, h_ref, o_ref, zero_s, sem, zsem):
    rows = h_ref.shape[0]
    tm = rows // ROW_TILE
    grp = zero_s.shape[0]
    n_groups = o_ref.shape[0] // grp
    base = pl.program_id(0) * (tm * TOP_K)

    @pl.when(pl.program_id(0) == 0)
    def _():
        zero_s[...] = jnp.zeros(zero_s.shape, zero_s.dtype)

        def fill(group):
            return pltpu.make_async_copy(zero_s, o_ref.at[pl.ds(pl.multiple_of(group * grp, grp), grp)], zsem)

        targets = [(last_ref[e] >= 0, last_ref[e]) for e in range(N_EXPERTS)]
        targets += [(nu_ref[0] + j < n_groups, nu_ref[0] + j) for j in range(N_EXPERTS)]
        for do, group in targets:
            @pl.when(do)
            def _():
                fill(group).start()
        for do, group in targets:
            @pl.when(do)
            def _():
                fill(group).wait()

    def body(t, carry):
        for k in range(TOP_K):
            pltpu.make_async_copy(_row_slice(h_ref, t), _row_slice(o_ref, dest_ref[base + t * TOP_K + k]),
                                  sem).start()
        return carry

    lax.fori_loop(0, tm, body, 0)
    for _ in range(TOP_K):
        pltpu.make_async_copy(h_ref, o_ref.at[pl.ds(0, rows)], sem).wait()


def _dispatch_call(dest, last_blk, n_used, h_rows, n_slots):
    n = h_rows.shape[0] // ROW_TILE
    grid_spec = pltpu.PrefetchScalarGridSpec(
        num_scalar_prefetch=3,
        grid=(n // TM,),
        in_specs=[pl.BlockSpec((TM * ROW_TILE, LANES), lambda i, dst, lst, nu: (i, 0))],
        out_specs=pl.BlockSpec(memory_space=pl.ANY),
        scratch_shapes=[pltpu.VMEM((MOE_TM * ROW_TILE, LANES), F32), pltpu.SemaphoreType.DMA(()),
                        pltpu.SemaphoreType.DMA(())],
    )
    return pl.pallas_call(
        _dispatch_kernel,
        grid_spec=grid_spec,
        out_shape=jax.ShapeDtypeStruct((n_slots * ROW_TILE, LANES), F32),
        compiler_params=_cparams(("arbitrary",)),
        name="dispatch",
    )(dest, last_blk, n_used, h_rows)


def _gmm_kernel(be_ref, nu_ref, nxt_ref, x_ref, wgu_hbm, bgu_ref, wd_hbm, bd_ref, y_ref, wgu_f, wd_f, wgu_s, wd_s,
                sems, *, f, layer):
    i = pl.program_id(0)
    used = i < nu_ref[0]
    changed = (i == 0) | (be_ref[i] != be_ref[jnp.maximum(i - 1, 0)])
    tm = x_ref.shape[0] // ROW_TILE

    def weight_copies(e):
        return (pltpu.make_async_copy(wgu_hbm.at[layer, e], wgu_f, sems.at[0]),
                pltpu.make_async_copy(wd_hbm.at[layer, e], wd_f, sems.at[1]))

    @pl.when(used & (i == 0))
    def _():
        for cp in weight_copies(be_ref[0]):
            cp.start()

    @pl.when(used & changed)
    def _():
        for cp in weight_copies(be_ref[i]):
            cp.wait()
        wgu_s[...] = wgu_f[...].astype(BF16)
        wd_s[...] = wd_f[...].astype(BF16)

        @pl.when(nxt_ref[i] >= 0)
        def _():
            for cp in weight_copies(nxt_ref[i]):
                cp.start()

    @pl.when(used)
    def _():
        x = jnp.concatenate([x_ref[pl.ds(sub, tm, stride=ROW_TILE), :] for sub in range(ROW_TILE)], axis=1)
        gu = _dot(x.astype(BF16), wgu_s[...]) + bgu_ref[...]
        gate = jnp.minimum(gu[:, :f], SWIGLU_LIMIT)
        up = jnp.clip(gu[:, f:], -SWIGLU_LIMIT, SWIGLU_LIMIT)
        act = gate * jax.nn.sigmoid(SWIGLU_ALPHA * gate) * (up + 1.0)
        y = _dot(act.astype(BF16), wd_s[...]) + bd_ref[...]
        for sub in range(ROW_TILE):
            y_ref[pl.ds(sub, tm, stride=ROW_TILE), :] = y[:, sub * LANES:(sub + 1) * LANES]

    @pl.when(jnp.logical_not(used))
    def _():
        y_ref[...] = jnp.zeros(y_ref.shape, y_ref.dtype)


def _gmm_call(block_e, n_used, next_e, x_slots, w_gu, b_gu, w_down, b_down, l):
    n_slots = x_slots.shape[0] // ROW_TILE
    depth, e, d, f2 = w_gu.shape
    f = f2 // 2
    n_blocks = n_slots // MOE_TM
    xrow = lambda i, be, nu, nx: (jnp.minimum(i, nu[0] - 1), 0)
    yrow = lambda i, be, nu, nx: (i, 0)
    bsel = lambda i, be, nu, nx: (l, be[i], 0, 0)
    grid_spec = pltpu.PrefetchScalarGridSpec(
        num_scalar_prefetch=3,
        grid=(n_blocks,),
        in_specs=[pl.BlockSpec((MOE_TM * ROW_TILE, LANES), xrow),
                  pl.BlockSpec(memory_space=pl.ANY),
                  pl.BlockSpec((None, None, 1, f2), bsel),
                  pl.BlockSpec(memory_space=pl.ANY),
                  pl.BlockSpec((None, None, 1, d), bsel)],
        out_specs=pl.BlockSpec((MOE_TM * ROW_TILE, LANES), yrow),
        scratch_shapes=[pltpu.VMEM((d, f2), F32), pltpu.VMEM((f, d), F32), pltpu.VMEM((d, f2), BF16),
                        pltpu.VMEM((f, d), BF16), pltpu.SemaphoreType.DMA((2,))],
    )
    return pl.pallas_call(
        functools.partial(_gmm_kernel, f=f, layer=l),
        grid_spec=grid_spec,
        out_shape=jax.ShapeDtypeStruct((n_slots * ROW_TILE, LANES), F32),
        compiler_params=_cparams(("arbitrary",)),
        name="experts",
    )(block_e, n_used, next_e, x_slots, w_gu, b_gu, w_down, b_down)


def _combine_kernel(dest_ref, x_ref, gate_ref, mod_ref, y_ref, o_ref, buf, sems, *, d):
    tm = x_ref.shape[0]
    rows = tm * ROW_TILE
    i = pl.program_id(0)
    n_steps = pl.num_programs(0)

    def issue(step, slot):
        base = step * (tm * TOP_K)

        def body(t, carry):
            for k in range(TOP_K):
                dst = buf.at[pl.ds(pl.multiple_of((slot * TOP_K + k) * rows + t * ROW_TILE, ROW_TILE), ROW_TILE)]
                pltpu.make_async_copy(_row_slice(y_ref, dest_ref[base + t * TOP_K + k]), dst, sems.at[slot]).start()
            return carry

        lax.fori_loop(0, tm, body, 0)

    @pl.when(i == 0)
    def _():
        issue(0, 0)

    @pl.when(i + 1 < n_steps)
    def _():
        issue(i + 1, (i + 1) % 2)

    slot = i % 2
    for k in range(TOP_K):
        off = pl.multiple_of((slot * TOP_K + k) * rows, ROW_TILE)
        pltpu.make_async_copy(y_ref.at[pl.ds(0, rows)], buf.at[pl.ds(off, rows)], sems.at[slot]).wait()
    g = gate_ref[...]
    gk = [jnp.broadcast_to(g[:, k:k + 1], (tm, LANES)) for k in range(TOP_K)]
    g2 = mod_ref[...][:, 5 * d:6 * d]
    for sub in range(ROW_TILE):
        sl = slice(sub * LANES, (sub + 1) * LANES)
        acc = None
        for k in range(TOP_K):
            off = pl.multiple_of((slot * TOP_K + k) * rows, ROW_TILE)
            term = gk[k] * buf[pl.ds(off + sub, tm, stride=ROW_TILE), :]
            acc = term if acc is None else acc + term
        o_ref[:, sl] = x_ref[:, sl] + g2[:, sl] * acc


def _combine_call(dest, x, gates, mod3, cond_of_tile, y_slots):
    n, d = x.shape
    row = lambda i, dst: (i, 0)
    grid_spec = pltpu.PrefetchScalarGridSpec(
        num_scalar_prefetch=1,
        grid=(n // TM,),
        in_specs=[pl.BlockSpec((TM, d), row),
                  pl.BlockSpec((TM, LANES), row),
                  pl.BlockSpec((None, 1, N_MOD * d), lambda i, dst: (cond_of_tile(i), 0, 0)),
                  pl.BlockSpec(memory_space=pl.ANY)],
        out_specs=pl.BlockSpec((TM, d), row),
        scratch_shapes=[pltpu.VMEM((2 * TOP_K * TM * ROW_TILE, LANES), F32), pltpu.SemaphoreType.DMA((2,))],
    )
    return pl.pallas_call(
        functools.partial(_combine_kernel, d=d),
        grid_spec=grid_spec,
        out_shape=jax.ShapeDtypeStruct((n, d), F32),
        compiler_params=_cparams(("arbitrary",)),
        name="combine",
    )(dest, x, gates, mod3, y_slots)


def _seg_matrix(w):
    i = np.arange(w) // HEAD_DIM
    return jnp.asarray((i[:, None] == i[None, :]).astype(np.float32)).astype(BF16)


def _seg_sum(y, p):
    hi = y.astype(BF16)
    lo = (y - hi.astype(F32)).astype(BF16)
    return _dot(hi, p) + _dot(lo, p)


def _head_mean_sq(x, p):
    return _seg_sum(x * x, p) * (1.0 / HEAD_DIM)


def _prep_kernel(z_ref, nw_ref, p_ref, *rest, scale, rope, want_norm, with_v):
    if rope:
        cos_ref, sin_ref = rest[0], rest[1]
        rest = rest[2:]
    x = z_ref[...]
    w = x.shape[1]
    if with_v:
        v = rest[0][...].astype(BF16)
        ones = jnp.ones((v.shape[0], HEAD_DIM), BF16)
        pieces = []
        for hd in range(w // HEAD_DIM):
            pieces += [v[:, hd * HEAD_DIM:(hd + 1) * HEAD_DIM], ones]
        rest[-1][...] = jnp.concatenate(pieces, axis=1)
        rest = rest[1:-1]
    y = x * lax.rsqrt(_head_mean_sq(x, p_ref[...]) + EPS) * nw_ref[...]
    if want_norm:
        rest[0][...] = y
    out_ref = rest[-1]
    if rope:
        reps = w // LANES
        cos = jnp.concatenate([cos_ref[...]] * reps, axis=1) if reps > 1 else cos_ref[...]
        sin = jnp.concatenate([sin_ref[...]] * reps, axis=1) if reps > 1 else sin_ref[...]
        lane = lax.broadcasted_iota(jnp.int32, y.shape, 1)
        first = (lane % 32) < 16
        rot = jnp.where(first, pltpu.roll(y, w - 16, axis=1), pltpu.roll(y, 16, axis=1))
        y = y * cos + rot * sin
    out_ref[...] = (y * scale).astype(BF16)


def _prep_call(z, col, w, nw, scale, rope_tabs, want_norm, v_col=None):
    n = z.shape[0]
    cb = col // w
    tm = next(c for c in (PREP_TM, TM) if n % c == 0)
    row = lambda i: (i, 0)
    in_specs = [pl.BlockSpec((tm, w), lambda i: (i, cb)),
                pl.BlockSpec((1, w), lambda i: (0, 0)),
                pl.BlockSpec((w, w), lambda i: (0, 0))]
    args = [z, jnp.tile(nw, w // HEAD_DIM).reshape(1, w), _seg_matrix(w)]
    if rope_tabs is not None:
        in_specs += [pl.BlockSpec((tm, LANES), row), pl.BlockSpec((tm, LANES), row)]
        args += list(rope_tabs)
    if v_col is not None:
        in_specs.append(pl.BlockSpec((tm, w), lambda i: (i, v_col // w)))
        args.append(z)
    out_specs, out_shape = [], []
    if want_norm:
        out_specs.append(pl.BlockSpec((tm, w), row))
        out_shape.append(jax.ShapeDtypeStruct((n, w), F32))
    out_specs.append(pl.BlockSpec((tm, w), row))
    out_shape.append(jax.ShapeDtypeStruct((n, w), BF16))
    if v_col is not None:
        out_specs.append(pl.BlockSpec((tm, 2 * w), row))
        out_shape.append(jax.ShapeDtypeStruct((n, 2 * w), BF16))
    return pl.pallas_call(
        functools.partial(_prep_kernel, scale=scale, rope=rope_tabs is not None, want_norm=want_norm,
                          with_v=v_col is not None),
        grid=(n // tm,),
        in_specs=in_specs, out_specs=out_specs, out_shape=out_shape,
        compiler_params=_cparams(("parallel",)),
        name="qk_prep",
    )(*args)


KEY_BLOCK = 512
LOG2E = 1.4426950408889634


def _ones_extended(v, heads):
    vb = v.astype(BF16).reshape(v.shape[:-1] + (heads, HEAD_DIM))
    return jnp.concatenate([vb, jnp.ones_like(vb)], axis=-1).reshape(v.shape[:-1] + (heads * 2 * HEAD_DIM,))


def _attn_kernel(q_ref, k_ref, v_ref, *rest, hq, hk, has_ctx, base2):
    if has_ctx:
        ck_ref, cv_ref, o_ref = rest
    else:
        (o_ref,) = rest
    g = hq // hk
    tq = q_ref.shape[0]
    t = k_ref.shape[0]
    dh = HEAD_DIM
    kb = min(KEY_BLOCK, t)
    ex = jnp.exp2 if base2 else jnp.exp
    blocks = [(ck_ref, cv_ref, 0, ck_ref.shape[0])] if has_ctx else []
    blocks += [(k_ref, v_ref, s0, kb) for s0 in range(0, t, kb)]
    outs = []
    for j in range(hk):
        q4 = jnp.concatenate([q_ref[:, (j * g + gi) * dh:(j * g + gi + 1) * dh] for gi in range(g)], axis=0)
        m = acc = None
        for kref, vref, s0, n in blocks:
            s = _dot_t(q4, kref[s0:s0 + n, j * dh:(j + 1) * dh])
            bm = jnp.max(s, axis=-1, keepdims=True)
            m_new = bm if m is None else jnp.maximum(m, bm)
            pv = _dot(ex(s - m_new).astype(BF16), vref[s0:s0 + n, 2 * j * dh:2 * (j + 1) * dh])
            acc = pv if m is None else ex(m - m_new) * acc + pv
            m = m_new
        o = acc[:, :dh] / acc[:, dh:]
        outs += [o[gi * tq:(gi + 1) * tq] for gi in range(g)]
    o_ref[...] = jnp.concatenate(outs, axis=1).astype(o_ref.dtype)


def _attn_call(q, k, v_ext, row0, nb, t, hq, hk, base2, ctx=None):
    tq = min(TQ, t)
    wq, wk = hq * HEAD_DIM, hk * HEAD_DIM
    qb0, kb0 = row0 // tq, row0 // t
    nq = t // tq
    in_specs = [pl.BlockSpec((tq, wq), lambda b, i: (qb0 + b * nq + i, 0)),
                pl.BlockSpec((t, wk), lambda b, i: (kb0 + b, 0)),
                pl.BlockSpec((t, 2 * wk), lambda b, i: (kb0 + b, 0))]
    args = [q, k, v_ext]
    if ctx is not None:
        ck, cv, layer = ctx
        p = ck.shape[2]
        in_specs += [pl.BlockSpec((None, None, p, wk), lambda b, i: (b, layer, 0, 0)),
                     pl.BlockSpec((None, None, p, 2 * wk), lambda b, i: (b, layer, 0, 0))]
        args += [ck, cv]
    return pl.pallas_call(
        functools.partial(_attn_kernel, hq=hq, hk=hk, has_ctx=ctx is not None, base2=base2),
        grid=(nb, nq),
        in_specs=in_specs,
        out_specs=pl.BlockSpec((tq, wq), lambda b, i: (b * nq + i, 0)),
        out_shape=jax.ShapeDtypeStruct((nb * t, wq), BF16),
        compiler_params=_cparams(("parallel", "parallel")),
        name="attn",
    )(*args)


def _nbr_bias_table(rpb):
    o = np.arange(NB_ROWS)[:, None, None, None]
    cc = np.arange(GRID_W)[None, :, None, None]
    i = np.arange(NB_ROWS)[None, None, :, None]
    j = np.arange(GRID_W)[None, None, None, :]
    col_start = np.clip(cc - NB_COLS // 2, 0, GRID_W - NB_COLS)
    valid = (j >= col_start) & (j < col_start + NB_COLS)
    sel_r = ((i - o + NB_ROWS - 1)[..., None] == np.arange(2 * NB_ROWS - 1)).astype(np.float32)[:, 0, :, 0]
    sel_c = (((j - cc + NB_COLS - 1)[..., None] == np.arange(2 * NB_COLS - 1)) & valid[..., None])
    sel_c = sel_c.astype(np.float32)[0, :, 0]
    tab = jnp.einsum('lhrd,oir,cjd->lhocij', rpb.astype(F32), jnp.asarray(sel_r), jnp.asarray(sel_c),
                     precision=HIGHEST)
    tab = tab * LOG2E + jnp.asarray(np.where(valid, 0.0, NEG).astype(np.float32))[None, None, :, :, :, :]
    return tab.reshape(rpb.shape[0], rpb.shape[1], NB_ROWS, GRID_W, NB_ROWS * GRID_W)


def _nbr_kernel(q_ref, k_ref, v_ref, ck_ref, cv_ref, bias_ref, o_ref, *, rows):
    r = pl.program_id(1)
    rs = jnp.clip(r - NB_ROWS // 2, 0, rows - NB_ROWS)
    start = pl.multiple_of(rs * GRID_W, GRID_W)
    kw = k_ref[pl.ds(start, NB_ROWS * GRID_W), :]
    vw = v_ref[pl.ds(start, NB_ROWS * GRID_W), :]
    dh = HEAD_DIM
    sls = [slice(h * dh, (h + 1) * dh) for h in range(B_HEADS)]
    vsl = [slice(2 * h * dh, 2 * (h + 1) * dh) for h in range(B_HEADS)]
    rowmax = lambda x: jnp.max(x, axis=-1, keepdims=True)
    qs = [q_ref[:, sl] for sl in sls]
    s_loc = [_dot_t(q, kw[:, sl]) + bias_ref[h] for h, (q, sl) in enumerate(zip(qs, sls))]
    s_ctx = [_dot_t(q, ck_ref[:, sl]) for q, sl in zip(qs, sls)]
    m = _each(lambda a, b: jnp.maximum(rowmax(a), rowmax(b)), s_loc, s_ctx)
    p_loc = _each(lambda a, mm: jnp.exp2(a - mm).astype(BF16), s_loc, m)
    p_ctx = _each(lambda a, mm: jnp.exp2(a - mm).astype(BF16), s_ctx, m)
    o = [_dot(a, vw[:, sl]) + _dot(b, cv_ref[:, sl]) for a, b, sl in zip(p_loc, p_ctx, vsl)]
    o_ref[...] = jnp.concatenate([x[:, :dh] / x[:, dh:] for x in o], axis=1).astype(o_ref.dtype)


def _nbr_call(q, k, v_ext, row0, nb, t, ck, cv, bias, layer):
    rows = t // GRID_W
    assert rows >= NB_ROWS
    w = B_HEADS * HEAD_DIM
    p = ck.shape[2]
    nwin = NB_ROWS * GRID_W

    def bias_map(b, r):
        return (layer, 0, r - jnp.clip(r - NB_ROWS // 2, 0, rows - NB_ROWS), 0, 0)

    return pl.pallas_call(
        functools.partial(_nbr_kernel, rows=rows),
        grid=(nb, rows),
        in_specs=[pl.BlockSpec((GRID_W, w), lambda b, r: (row0 // GRID_W + b * rows + r, 0)),
                  pl.BlockSpec((t, w), lambda b, r: (row0 // t + b, 0)),
                  pl.BlockSpec((t, 2 * w), lambda b, r: (row0 // t + b, 0)),
                  pl.BlockSpec((None, None, p, w), lambda b, r: (b, layer, 0, 0)),
                  pl.BlockSpec((None, None, p, 2 * w), lambda b, r: (b, layer, 0, 0)),
                  pl.BlockSpec((None, B_HEADS, None, GRID_W, nwin), bias_map)],
        out_specs=pl.BlockSpec((GRID_W, w), lambda b, r: (b * rows + r, 0)),
        out_shape=jax.ShapeDtypeStruct((nb * t, w), BF16),
        compiler_params=_cparams(("parallel", "arbitrary")),
        name="nbr_attn",
    )(q, k, v_ext, ck, cv, bias)


def _head_mask():
    i = np.arange(HW) // HEAD_DIM
    return jnp.asarray((i[:, None] == i[None, :]).astype(np.float32))


def _scan_rows(rev):
    t = lax.broadcasted_iota(jnp.int32, (CHUNK, HW), 0)
    return (CHUNK - 1 - t) if rev else t


def _scan_lanes(rev):
    s = lax.broadcasted_iota(jnp.int32, (CHUNK, HW), 1) % CHUNK
    return (CHUNK - 1 - s) if rev else s


def _sh(x, d, rev):
    s = (-d if rev else d) % CHUNK
    return x if s == 0 else pltpu.roll(x, s, axis=0)


def _cumsum_scan(x, pt, rev):
    d = 1
    while d < CHUNK:
        x = x + jnp.where(pt >= d, _sh(x, d, rev), 0.0)
        d *= 2
    return x


def _block_end(x, pt, n, rev):
    r = (n - 1) - pt % n
    bit = 1
    while bit < n:
        x = jnp.where((r & bit) != 0, _sh(x, -bit, rev), x)
        bit *= 2
    return x


def _pick16(x, j, rev):
    jj = (15 - j) if rev else j
    x3 = x.reshape(CHUNK // 16, 16, HW)
    return jnp.broadcast_to(x3[:, jj:jj + 1, :], x3.shape).reshape(CHUNK, HW)


def _row(x, p, rev):
    t = (CHUNK - 1 - p) if rev else p
    return x[t:t + 1, :]


def _bd(x, mask_bf):
    xb = x.astype(BF16)
    return jnp.concatenate([xb] * (HW // CHUNK), axis=0) * mask_bf


def _gla_chunks(ops_list, lbs, st_refs, mask, revs):
    mask_bf = mask.astype(BF16)
    bf = lambda x: x.astype(BF16)

    def prepare(ops, lb, rev):
        q_raw, v, f_raw = ops
        pt = _scan_rows(rev)
        q = _silu(q_raw)
        kk = jnp.minimum((1.0 - lb) * jax.nn.sigmoid(-f_raw), MAX_INPUT_GATE)
        b = _cumsum_scan(jnp.log1p(-kk), pt, rev)
        e4 = _block_end(b, pt, 4, rev)
        r4 = jnp.where(pt >= 4, _sh(e4, 4, rev), 0.0)
        r16 = jnp.where(pt >= 16, _sh(_pick16(b, 15, rev), 16, rev), 0.0)
        ks01 = [_bd(kk * jnp.exp(r4 - b), mask_bf)]
        ks2 = []
        for i in range(1, 4):
            ks01.append(_bd(kk * jnp.exp(jnp.minimum(_pick16(r4, 4 * i, rev) - b, 0.0)), mask_bf))
            ks2.append(_bd(kk * jnp.exp(jnp.minimum(_row(b, 16 * i - 1, rev) - b, 0.0)), mask_bf))
        return dict(pt=pt, ps=_scan_lanes(rev), v=v, kk=kk, b=b, b_last=_row(b, CHUNK - 1, rev),
                    q0=bf(q * jnp.exp(b - r4)), q2=bf(q * jnp.exp(b - r16)), qe=bf(q * jnp.exp(b)),
                    k01=jnp.concatenate(ks01, axis=0), k2=jnp.concatenate(ks2, axis=0))

    def select(c, r01, r2):
        pt, ps = c['pt'], c['ps']
        pt4, ps4, pt16, ps16 = pt // 4, ps // 4, pt // 16, ps // 16
        attn = jnp.where((pt4 == ps4) & (ps <= pt), r01[:, :HW], 0.0)
        for i in range(1, 4):
            attn = jnp.where((pt16 == ps16) & (ps4 < pt4) & (pt4 % 4 == i), r01[:, i * HW:(i + 1) * HW], attn)
            attn = jnp.where((ps16 < pt16) & (pt16 == i), r2[:, (i - 1) * HW:i * HW], attn)
        return bf(attn)

    cs = _each(prepare, ops_list, lbs, revs)
    r01 = [_dot_t(c['q0'], c['k01']) for c in cs]
    r2 = [_dot_t(c['q2'], c['k2']) for c in cs]
    attn = _each(select, cs, r01, r2)
    st = [s[...] for s in st_refs]
    o = _each(lambda c, s, aa: _dot_t(c['qe'], bf(s)) + _dot(aa, _bd(c['v'], mask_bf)), cs, st, attn)
    upd = [lax.dot_general(bf(c['v']), bf(c['kk'] * jnp.exp(c['b_last'] - c['b'])), (((0,), (0,)), ((), ())),
                           preferred_element_type=F32) for c in cs]
    for s_ref, c, s, u in zip(st_refs, cs, st, upd):
        s_ref[...] = jnp.exp(c['b_last']) * s + mask * u
    return o


def _scan_kernel(*refs, chunk_fn, n_in, n_const, group, has_state):
    n_op = group * 2 * n_in
    ins, consts, rest = refs[:n_op], refs[n_op:n_op + n_const], refs[n_op + n_const:]
    if has_state:
        s0_ref, of_ref, ob_ref, sout_ref, st_s = rest
    else:
        of_ref, ob_ref, sout_ref, st_s = rest
    c = pl.program_id(1)

    @pl.when(c == 0)
    def _():
        st_s[...] = s0_ref[...] if has_state else jnp.zeros(st_s.shape, F32)

    chains = [(g, direction) for g in range(group) for direction in (0, 1)]
    ops_list = [[r[...] for r in ins[(g * 2 + dr) * n_in:(g * 2 + dr + 1) * n_in]] for g, dr in chains]
    outs = chunk_fn(ops_list, consts, [st_s.at[g, dr] for g, dr in chains], [dr for _, dr in chains])
    for (g, dr), o in zip(chains, outs):
        (ob_ref if dr else of_ref)[g] = o

    @pl.when(c == pl.num_programs(1) - 1)
    def _():
        sout_ref[...] = st_s[...]


def _scan_call(name, chunk_fn, ops_f, ops_b, consts, row0, nb, t, s0_bd, group):
    assert nb % group == 0 and len(ops_f) == len(ops_b)
    nc = t // CHUNK
    r0 = row0 // CHUNK
    has_state = s0_bd is not None
    in_specs, args = [], []
    for g in range(group):
        for direction, ops in ((0, ops_f), (1, ops_b)):
            for arr, w, cb in ops:
                if direction == 0:
                    imap = functools.partial(lambda b, c, g, cb: (r0 + (b * group + g) * nc + c, cb), g=g, cb=cb)
                else:
                    imap = functools.partial(lambda b, c, g, cb: (r0 + (b * group + g) * nc + nc - 1 - c, cb),
                                             g=g, cb=cb)
                in_specs.append(pl.BlockSpec((CHUNK, w), imap))
                args.append(arr)
    for cst in consts:
        in_specs.append(pl.BlockSpec(cst.shape, functools.partial(lambda b, c, nd: (0,) * nd, nd=cst.ndim)))
        args.append(cst)
    state_spec = pl.BlockSpec((group, 2, HW, HW), lambda b, c: (b, 0, 0, 0))
    if has_state:
        s0_all, layer = s0_bd
        in_specs.append(pl.BlockSpec((group, None, 2, HW, HW), lambda b, c: (b, layer, 0, 0, 0)))
        args.append(s0_all)
    o_f, o_b, s_out = pl.pallas_call(
        functools.partial(_scan_kernel, chunk_fn=chunk_fn, n_in=len(ops_f), n_const=len(consts), group=group,
                          has_state=has_state),
        grid=(nb // group, nc),
        in_specs=in_specs,
        out_specs=[pl.BlockSpec((group, CHUNK, HW), lambda b, c: (b, c, 0)),
                   pl.BlockSpec((group, CHUNK, HW), lambda b, c: (b, nc - 1 - c, 0)),
                   state_spec],
        out_shape=[jax.ShapeDtypeStruct((nb, t, HW), F32), jax.ShapeDtypeStruct((nb, t, HW), F32),
                   jax.ShapeDtypeStruct((nb, 2, HW, HW), F32)],
        scratch_shapes=[pltpu.VMEM((group, 2, HW, HW), F32)],
        compiler_params=_cparams(("parallel", "arbitrary")),
        name=name,
    )(*args)
    return o_f.reshape(nb * t, HW), o_b.reshape(nb * t, HW), s_out


def _hgrn_chunks_fn(ops_list, consts, st_refs, directions):
    lb_ref, mask_ref = consts
    return _gla_chunks(ops_list, [lb_ref[dr:dr + 1, :] for dr in directions], st_refs, mask_ref[...],
                       [dr == 1 for dr in directions])


def _state_to_bd(s):
    eye = jnp.eye(C_HEADS, dtype=F32)
    return jnp.einsum('...hkv,hg->...hvgk', s.astype(F32), eye).reshape(s.shape[:-3] + (HW, HW))


def _state_from_bd(sbd):
    s6 = sbd.reshape(sbd.shape[:-2] + (C_HEADS, HEAD_DIM, C_HEADS, HEAD_DIM))
    diag = jnp.stack([s6[..., h, :, h, :] for h in range(C_HEADS)], axis=-3)
    return jnp.swapaxes(diag, -1, -2)


def _hgrn_call(z, row0, nb, t, lb, s0_bd, group):
    ops = lambda fcol: [(z, HW, COL['cq'] // HW), (z, HW, COL['ci'] // HW), (z, HW, fcol // HW)]
    return _scan_call("hgrn_scan", _hgrn_chunks_fn, ops(COL['cff']), ops(COL['cfb']), [lb, _head_mask()],
                      row0, nb, t, s0_bd, group)


def _gate_norm_kernel(ofp_ref, ofs_ref, obp_ref, obs_ref, g_ref, nw_ref, p_ref, o_ref, *, n_first):
    i = pl.program_id(0)
    o = _two_path(i, n_first, ofp_ref, ofs_ref) + _two_path(i, n_first, obp_ref, obs_ref)
    y = o * lax.rsqrt(_head_mean_sq(o, p_ref[...]) + EPS) * nw_ref[...]
    o_ref[...] = (y * _silu(g_ref[...])).astype(o_ref.dtype)


def _gate_norm_call(of_p, of_s, ob_p, ob_s, z, gcol, nw, n_first):
    n = of_p.shape[0] + of_s.shape[0]
    row = lambda i: (i, 0)
    return pl.pallas_call(
        functools.partial(_gate_norm_kernel, n_first=n_first),
        grid=(n // TM,),
        in_specs=_two_path_specs(HW, n_first) + _two_path_specs(HW, n_first)
        + [pl.BlockSpec((TM, HW), lambda i: (i, gcol // HW)),
           pl.BlockSpec((1, HW), lambda i: (0, 0)),
           pl.BlockSpec((HW, HW), lambda i: (0, 0))],
        out_specs=pl.BlockSpec((TM, HW), row),
        out_shape=jax.ShapeDtypeStruct((n, HW), BF16),
        compiler_params=_cparams(("parallel",)),
        name="gate_norm",
    )(of_p, of_s, ob_p, ob_s, z, jnp.tile(nw, HW // HEAD_DIM).reshape(1, HW), _seg_matrix(HW))


DQKV_W = 3 * HW
HALO = 8


def _softplus(x):
    return jnp.maximum(x, 0.0) + jnp.log1p(jnp.exp(-jnp.abs(x)))


def _delta_prep_kernel(x_ref, xp_ref, xn_ref, cw_ref, dab_ref, ex_ref, al_ref, dtb_ref, p_ref, qkv_ref, gb_ref,
                       *, n_single, tiles_per_seq):
    i = pl.program_id(0)
    j = jnp.maximum(i - n_single, 0) % tiles_per_seq
    first = (i < n_single) | (j == 0)
    last = (i < n_single) | (j == tiles_per_seq - 1)
    x = x_ref[...]
    tm = x.shape[0]
    prev = jnp.where(first, 0.0, xp_ref[...])
    nxt = jnp.where(last, 0.0, xn_ref[...])
    xe = jnp.concatenate([prev, x, nxt], axis=0)
    y = None
    for tap in range(CONV_K):
        lo = HALO + tap - CONV_K // 2
        term = cw_ref[tap:tap + 1, :] * xe[lo:lo + tm]
        y = term if y is None else y + term
    y = _silu(y)
    qk = y[:, :2 * HW]
    ssq = _seg_sum(qk * qk, p_ref[...])
    qkn = qk * lax.rsqrt(ssq + EPS)
    qkv_ref[...] = jnp.concatenate([qkn[:, :HW] * (HEAD_DIM ** -0.5), qkn[:, HW:], y[:, 2 * HW:]], axis=1)
    e = jnp.dot(dab_ref[...], ex_ref[...], preferred_element_type=F32, precision=HIGHEST)
    g = -jnp.exp(al_ref[...]) * _softplus(e[:, :2 * HW] + dtb_ref[...])
    gb_ref[...] = jnp.concatenate([g, jax.nn.sigmoid(e[:, 2 * HW:])], axis=1)


def _delta_prep_call(z, conv_w, a_log, dt_bias, n_single, tiles_per_seq):
    n = z.shape[0]
    cb = COL['dq'] // DQKV_W
    hb = TM // HALO
    nhb = n // HALO
    ex = np.zeros((LANES, 4 * HW), np.float32)
    for r in range(4 * D_HEADS):
        ex[r, r * HEAD_DIM:(r + 1) * HEAD_DIM] = 1.0
    cw = jnp.zeros((8, DQKV_W), F32).at[:CONV_K].set(conv_w.astype(F32))
    const = lambda i: (0, 0)
    return pl.pallas_call(
        functools.partial(_delta_prep_kernel, n_single=n_single, tiles_per_seq=tiles_per_seq),
        grid=(n // TM,),
        in_specs=[pl.BlockSpec((TM, DQKV_W), lambda i: (i, cb)),
                  pl.BlockSpec((HALO, DQKV_W), lambda i: (jnp.maximum(i * hb - 1, 0), cb)),
                  pl.BlockSpec((HALO, DQKV_W), lambda i: (jnp.minimum((i + 1) * hb, nhb - 1), cb)),
                  pl.BlockSpec((8, DQKV_W), const),
                  pl.BlockSpec((TM, LANES), lambda i: (i, COL['dab'] // LANES)),
                  pl.BlockSpec((LANES, 4 * HW), const),
                  pl.BlockSpec((1, 2 * HW), const), pl.BlockSpec((1, 2 * HW), const),
                  pl.BlockSpec((2 * HW, 2 * HW), const)],
        out_specs=[pl.BlockSpec((TM, DQKV_W), lambda i: (i, 0)), pl.BlockSpec((TM, 4 * HW), lambda i: (i, 0))],
        out_shape=[jax.ShapeDtypeStruct((n, DQKV_W), F32), jax.ShapeDtypeStruct((n, 4 * HW), F32)],
        compiler_params=_cparams(("parallel",)),
        name="delta_prep",
    )(z, z, z, cw, z, jnp.asarray(ex),
      jnp.repeat(a_log.astype(F32).reshape(-1), HEAD_DIM).reshape(1, 2 * HW),
      jnp.repeat(dt_bias.astype(F32).reshape(-1), HEAD_DIM).reshape(1, 2 * HW), _seg_matrix(2 * HW))


def _delta_consts():
    t = np.arange(CHUNK)[:, None]
    s = (np.arange(HW) % CHUNK)[None, :]
    s16 = t // 16 == s // 16
    s32 = t // 32 == s // 32
    mats = [t == s, s16, s32 & ~s16, ~s32]
    return jnp.asarray(np.stack(mats).astype(np.float32))


def _tri_inv(acs, mask_bf, eye, m16, m32, m64):
    bf = lambda x: x.astype(BF16)
    mm = lambda x, y: _dot(bf(x), _bd(y, mask_bf))
    a16 = [a * m16 for a in acs]
    tin = [eye - a for a in a16]
    p = _each(mm, a16, a16)
    for step in range(3):
        tin = _each(lambda t, pp: t + mm(t, pp), tin, p)
        if step < 2:
            p = _each(lambda pp: mm(pp, pp), p)
    for m in (m32, m64):
        inner = _each(lambda a, t: mm(a * m, t), acs, tin)
        tin = _each(lambda t, i_n: t - mm(t, i_n), tin, inner)
    return tin


def _delta_chunks(ops_list, st_refs, mask_ref, cmask_ref, revs):
    mask = mask_ref[...]
    mask_bf = mask.astype(BF16)
    bf = lambda x: x.astype(BF16)
    pts = [_scan_rows(r) for r in revs]
    pss = [_scan_lanes(r) for r in revs]
    qs = [o[0][:, :HW] for o in ops_list]
    ks = [o[0][:, HW:2 * HW] for o in ops_list]
    vs = [o[0][:, 2 * HW:] for o in ops_list]
    betas = [o[2] for o in ops_list]
    gcs = _each(_cumsum_scan, [o[1] for o in ops_list], pts, revs)
    t_idx = lax.broadcasted_iota(jnp.int32, (CHUNK, HW), 0)
    s_idx = lax.broadcasted_iota(jnp.int32, (CHUNK, HW), 1) % CHUNK
    diag = t_idx == s_idx
    gcl = [jnp.sum(jnp.where(diag, gc, 0.0), axis=0, keepdims=True) for gc in gcs]
    gam = _each(lambda gc, gl: jnp.exp(jnp.minimum(gc - gl, 0.0)), gcs, gcl)
    g_last = _each(lambda gc, r: _row(gc, CHUNK - 1, r), gcs, revs)
    egc = [jnp.exp(gc) for gc in gcs]
    kb = _each(lambda k, b: k * b, ks, betas)
    r = _each(lambda k_b, q, k: _dot_t(bf(jnp.concatenate([k_b, q], axis=0)), _bd(k, mask_bf)), kb, qs, ks)
    a = _each(lambda rr, gm, ps, pt: jnp.where(ps < pt, rr[:CHUNK] * gm, 0.0), r, gam, pss, pts)
    aq = _each(lambda rr, gm, ps, pt: jnp.where(ps <= pt, rr[CHUNK:] * gm, 0.0), r, gam, pss, pts)
    tin = _tri_inv(a, mask_bf, cmask_ref[0], cmask_ref[1], cmask_ref[2], cmask_ref[3])
    rhs = _each(lambda v, b, k_b, e: jnp.concatenate([_bd(v * b, mask_bf), _bd(k_b * e, mask_bf)], axis=1),
                vs, betas, kb, egc)
    uw = _each(lambda t, rh: _dot(bf(t), rh), tin, rhs)
    st = [s[...] for s in st_refs]
    ws_qs = _each(lambda x, q, e, s: _dot_t(bf(jnp.concatenate([x[:, HW:], q * e], axis=0)), bf(s)),
                  uw, qs, egc, st)
    v_new = _each(lambda x, y: x[:, :HW] - y[:CHUNK], uw, ws_qs)
    o = _each(lambda y, aa, vn: y[CHUNK:] + _dot(bf(aa), _bd(vn, mask_bf)), ws_qs, aq, v_new)
    ke = _each(lambda k, gl, gc: bf(k * jnp.exp(gl - gc)), ks, g_last, gcs)
    upd = _each(lambda vn, kk: lax.dot_general(bf(vn), kk, (((0,), (0,)), ((), ())), preferred_element_type=F32),
                v_new, ke)
    for s_ref, gl, s, u in zip(st_refs, g_last, st, upd):
        s_ref[...] = jnp.exp(gl) * s + mask * u
    return o


def _delta_chunks_fn(ops_list, consts, st_refs, directions):
    return _delta_chunks(ops_list, st_refs, consts[0], consts[1], [d == 1 for d in directions])


def _delta_call(qkv, gb, row0, nb, t, s0_bd, group):
    ops_f = [(qkv, DQKV_W, 0), (gb, HW, 0), (gb, HW, 2)]
    ops_b = [(qkv, DQKV_W, 0), (gb, HW, 1), (gb, HW, 3)]
    return _scan_call("delta_scan", _delta_chunks_fn, ops_f, ops_b, [_head_mask(), _delta_consts()], row0, nb, t, s0_bd, group)


def kernel(x_prompt, x_sample, cache_a_k, cache_a_v, cache_b_k, cache_b_v, state_hgrn, state_delta, c, c_ctx, w_mod, b_mod, norm1, norm2, w_in, w_gate, a_qn, a_kn, b_qn, b_kn, b_rpb, c_lb, c_norm, d_conv, d_alog, d_dtbias, d_norm, w_branch, w_out, w_router, b_router, w_gu, b_gu, w_down, b_down):
    nbp, tp, d = x_prompt.shape
    nbs, ts, _ = x_sample.shape
    depth = w_mod.shape[0]
    past = cache_a_k.shape[2]
    n_p = nbp * tp
    n = n_p + nbs * ts
    assert tp == TM and ts % TM == 0 and n_p % ts == 0 and 1 + nbs <= 8 and n % TM_MM == 0
    assert d == ROW_TILE * LANES
    npt, tps = n_p // TM, ts // TM
    grp_p = next(g for g in (4, 2, 1) if nbp % g == 0)
    grp_s = next(g for g in (2, 1) if nbs % g == 0)

    def cond_index(i):
        return jnp.where(i < npt, 0, 1 + (i - npt) // tps)

    x = jnp.concatenate([x_prompt.reshape(n_p, d), x_sample.reshape(nbs * ts, d)], axis=0)
    conds = jnp.zeros((8, d), F32).at[0].set(c_ctx).at[1:1 + nbs].set(c)
    mod3 = _mod_call(conds, w_mod, b_mod).reshape(depth * 8, 1, N_MOD * d)

    pl_ = jax.nn.softmax(c_lb.astype(F32), axis=0)
    lbs = jnp.cumsum(pl_, axis=0) - pl_[0:1]
    dqkv0 = COL['cg'] + HW
    da0 = dqkv0 + DQKV_W
    dg0 = da0 + 4 * D_HEADS
    w_in_p = jnp.concatenate([w_in[:, :, :dqkv0], w_in[:, :, dg0:dg0 + HW], w_in[:, :, dqkv0:da0],
                              w_in[:, :, da0:dg0],
                              jnp.zeros((depth, d, IN_W_PAD - dg0 - HW), w_in.dtype)], axis=2).astype(BF16)
    w_gate_b = w_gate.astype(BF16)
    w_branch_b = w_branch.astype(BF16)
    w_out_b = w_out.astype(BF16)
    w_router_p = jnp.zeros((depth, d, LANES), F32).at[:, :, :N_EXPERTS].set(w_router)
    b_router_p = jnp.full((depth, 1, LANES), NEG, F32).at[:, 0, :N_EXPERTS].set(b_router)

    tt = jnp.arange(ts, dtype=jnp.int32)
    half = HEAD_DIM // 2
    inv_freq = 1.0 / (ROPE_THETA ** (jnp.arange(0, half, 2, dtype=F32) / half))
    ang_r = (tt // GRID_W).astype(F32)[:, None] * inv_freq[None, :]
    ang_c = (tt % GRID_W).astype(F32)[:, None] * inv_freq[None, :]
    ang = jnp.concatenate([ang_r, ang_r, ang_c, ang_c], axis=-1)
    sign = jnp.asarray(np.where((np.arange(HEAD_DIM) % 32) < 16, -1.0, 1.0).astype(np.float32))
    cos_all = jnp.concatenate([jnp.ones((n_p, HEAD_DIM), F32), jnp.tile(jnp.cos(ang), (nbs, 1))], axis=0)
    sin_all = jnp.concatenate([jnp.zeros((n_p, HEAD_DIM), F32), jnp.tile(jnp.sin(ang) * sign, (nbs, 1))], axis=0)
    rope_tabs = (jnp.tile(cos_all, (1, 2)), jnp.tile(sin_all, (1, 2)))
    scale = HEAD_DIM ** -0.5
    wa, wb = A_KV_HEADS * HEAD_DIM, B_HEADS * HEAD_DIM

    cak = cache_a_k.reshape(nbs, depth, past, wa).astype(BF16)
    cav = _ones_extended(cache_a_v.reshape(nbs, depth, past, wa), A_KV_HEADS)
    cbk = cache_b_k.reshape(nbs, depth, past, wb).astype(BF16)
    cbv = _ones_extended(cache_b_v.reshape(nbs, depth, past, wb), B_HEADS)
    nbr_bias = _nbr_bias_table(b_rpb)
    sc0 = _state_to_bd(state_hgrn)
    sd0 = _state_to_bd(state_delta)
    b_gu4 = b_gu.reshape(depth, N_EXPERTS, 1, b_gu.shape[-1])
    b_down4 = b_down.reshape(depth, N_EXPERTS, 1, d)

    ak_l, av_l, bk_l, bv_l, sc_l, sd_l = [], [], [], [], [], []
    for l in range(depth):
        cond_of_tile = functools.partial(lambda i, base: base + cond_index(i), base=8 * l)
        h = _norm_call(x, norm1[l].reshape(1, d), mod3, cond_of_tile)
        z = _mm_call(h, w_in_p, l)

        (qa,) = _prep_call(z, COL['aq'], A_HEADS * HEAD_DIM, a_qn[l], scale * LOG2E, rope_tabs, False)
        ka_n, ka, va_ext = _prep_call(z, COL['ak'], wa, a_kn[l], 1.0, rope_tabs, True, v_col=COL['av'])
        oa_p = _attn_call(qa, ka, va_ext, 0, nbp, tp, A_HEADS, A_KV_HEADS, True)
        oa_s = _attn_call(qa, ka, va_ext, n_p, nbs, ts, A_HEADS, A_KV_HEADS, True, ctx=(cak, cav, l))

        (qb,) = _prep_call(z, COL['bq'], wb, b_qn[l], scale * LOG2E, None, False)
        kb_n, kb, vb_ext = _prep_call(z, COL['bk'], wb, b_kn[l], 1.0, None, True, v_col=COL['bv'])
        ob_p = _attn_call(qb, kb, vb_ext, 0, nbp, tp, B_HEADS, B_HEADS, True)
        ob_s = _nbr_call(qb, kb, vb_ext, n_p, nbs, ts, cbk, cbv, nbr_bias, l)

        cf_p, cb_p, sc_p = _hgrn_call(z, 0, nbp, tp, lbs[l], None, grp_p)
        cf_s, cb_s, _ = _hgrn_call(z, n_p, nbs, ts, lbs[l], (sc0, l), grp_s)
        oc = _gate_norm_call(cf_p, cf_s, cb_p, cb_s, z, COL['cg'], c_norm[l], npt)

        qkv, gb = _delta_prep_call(z, d_conv[l], d_alog[l], d_dtbias[l], npt, tps)
        df_p, db_p, sd_p = _delta_call(qkv, gb, 0, nbp, tp, None, grp_p)
        df_s, db_s, _ = _delta_call(qkv, gb, n_p, nbs, ts, (sd0, l), grp_s)
        od = _gate_norm_call(df_p, df_s, db_p, db_s, z, COL['dg'], d_norm[l], npt)

        x = _merge_call(x, h, oa_p, oa_s, ob_p, ob_s, oc, od, w_gate_b, w_branch_b, w_out_b, l, mod3, cond_of_tile,
                        npt)

        h_rows, gates, sel, counts = _router_call(x, norm2[l].reshape(1, d), mod3, cond_of_tile,
                                                  w_router_p, b_router_p, l)
        dest, block_e, n_used, next_e, last_blk, n_blocks = _slot_plan(sel, counts, n)
        x_slots = _dispatch_call(dest, last_blk, n_used, h_rows, n_blocks * MOE_TM)
        y_slots = _gmm_call(block_e, n_used, next_e, x_slots, w_gu, b_gu4, w_down, b_down4, l)
        x = _combine_call(dest, x, gates, mod3, cond_of_tile, y_slots)

        ak_l.append(ka_n[:n_p].reshape(nbp, tp, A_KV_HEADS, HEAD_DIM))
        av_l.append(z[:n_p, COL['av']:COL['av'] + wa].reshape(nbp, tp, A_KV_HEADS, HEAD_DIM))
        bk_l.append(kb_n[:n_p].reshape(nbp, tp, B_HEADS, HEAD_DIM))
        bv_l.append(z[:n_p, COL['bv']:COL['bv'] + wb].reshape(nbp, tp, B_HEADS, HEAD_DIM))
        sc_l.append(sc_p)
        sd_l.append(sd_p)

    return (x[:n_p].reshape(nbp, tp, d), x[n_p:].reshape(nbs, ts, d),
            jnp.stack(ak_l, axis=1), jnp.stack(av_l, axis=1), jnp.stack(bk_l, axis=1), jnp.stack(bv_l, axis=1),
            _state_from_bd(jnp.stack(sc_l, axis=1)), _state_from_bd(jnp.stack(sd_l, axis=1)))
```

```python
import functools

import numpy as np
import jax
import jax.numpy as jnp
from jax import lax
from jax.experimental import pallas as pl
from jax.experimental.pallas import tpu as pltpu

F32 = jnp.float32
BF16 = jnp.bfloat16
HIGHEST = lax.Precision.HIGHEST

GRID_W = 64
HEAD_DIM = 64
A_HEADS = 8
A_KV_HEADS = 2
B_HEADS = 4
NB_ROWS = 8
NB_COLS = 16
C_HEADS = 4
D_HEADS = 4
CONV_K = 5
CHUNK = 64
N_EXPERTS = 32
TOP_K = 4
SWIGLU_LIMIT = 7.0
SWIGLU_ALPHA = 1.702
ROPE_THETA = 10000.0
EPS = 1e-6
MAX_INPUT_GATE = 1.0 - 1e-6
N_MOD = 6
HW = 256

LANES = 128
VMEM_LIMIT = 56 * 1024 * 1024
NEG = -1e30

TM = 256
PREP_TM = 1024
TM_MM = 512
TN_MM = 1024
TQ = 256
MOE_TM = 256

COL = dict(aq=0, ak=512, av=640, bq=768, bk=1024, bv=1280, cq=1536, ci=1792, cff=2048, cfb=2304,
           cg=2560, dg=2816, dq=3072, dk=3328, dv=3584, dab=3840)
IN_W_PAD = 4096


def _cparams(sem, vmem=VMEM_LIMIT):
    return pltpu.CompilerParams(dimension_semantics=sem, vmem_limit_bytes=vmem)


def _silu(x):
    return x * jax.nn.sigmoid(x)


def _dot(a, b):
    return jnp.dot(a, b, preferred_element_type=F32)


def _dot_t(a, b):
    return lax.dot_general(a, b, (((1,), (1,)), ((), ())), preferred_element_type=F32)


def _each(f, *lists):
    return [f(*a) for a in zip(*lists)]


def _mod_kernel(c_ref, w_ref, b_ref, o_ref):
    a = _silu(c_ref[...]).astype(BF16)
    o_ref[...] = _dot(a, w_ref[...].astype(BF16)) + b_ref[...]


def _mod_call(conds, w_mod, b_mod):
    depth, d, n = w_mod.shape
    tn = 1024
    return pl.pallas_call(
        _mod_kernel,
        grid=(depth, n // tn),
        in_specs=[pl.BlockSpec((8, d), lambda l, j: (0, 0)),
                  pl.BlockSpec((None, d, tn), lambda l, j: (l, 0, j)),
                  pl.BlockSpec((None, 1, tn), lambda l, j: (l, 0, j))],
        out_specs=pl.BlockSpec((None, 8, tn), lambda l, j: (l, 0, j)),
        out_shape=jax.ShapeDtypeStruct((depth, 8, n), F32),
        compiler_params=_cparams(("parallel", "parallel")),
        name="mod",
    )(conds, w_mod, b_mod.reshape(depth, 1, n))


def _mod_norm(x, g, m, k_shift, k_scale, d):
    y = x * lax.rsqrt(jnp.mean(x * x, axis=-1, keepdims=True) + EPS) * g
    return y * (1.0 + m[:, k_scale * d:(k_scale + 1) * d]) + m[:, k_shift * d:(k_shift + 1) * d]


def _norm_kernel(x_ref, g_ref, mod_ref, h_ref, *, d):
    h_ref[...] = _mod_norm(x_ref[...], g_ref[...], mod_ref[...], 0, 1, d).astype(BF16)


def _norm_call(x, g, mod3, cond_of_tile, tile):
    n, d = x.shape
    return pl.pallas_call(
        functools.partial(_norm_kernel, d=d),
        grid=(n // tile,),
        in_specs=[pl.BlockSpec((tile, d), lambda i: (i, 0)),
                  pl.BlockSpec((1, d), lambda i: (0, 0)),
                  pl.BlockSpec((None, 1, N_MOD * d), lambda i: (cond_of_tile(i), 0, 0))],
        out_specs=pl.BlockSpec((tile, d), lambda i: (i, 0)),
        out_shape=jax.ShapeDtypeStruct((n, d), BF16),
        compiler_params=_cparams(("parallel",)),
        name="norm1",
    )(x, g, mod3)


def _mm_kernel(a_ref, w_ref, o_ref):
    o_ref[...] = _dot(a_ref[...], w_ref[...]).astype(o_ref.dtype)


def _mm_call(a, w, l, out_dtype=F32):
    m, k = a.shape
    n = w.shape[2]
    return pl.pallas_call(
        _mm_kernel,
        grid=(n // TN_MM, m // TM_MM),
        in_specs=[pl.BlockSpec((TM_MM, k), lambda j, i: (i, 0)),
                  pl.BlockSpec((None, k, TN_MM), lambda j, i: (l, 0, j))],
        out_specs=pl.BlockSpec((TM_MM, TN_MM), lambda j, i: (i, j)),
        out_shape=jax.ShapeDtypeStruct((m, n), out_dtype),
        compiler_params=_cparams(("parallel", "parallel")),
        name="in_proj",
    )(a, w)


def _two_path(i, n_first, first_ref, second_ref):
    return jnp.where(i < n_first, first_ref[...], second_ref[...])


def _two_path_specs(width, n_first, tile=TM):
    return [pl.BlockSpec((tile, width), lambda i: (jnp.minimum(i, n_first - 1), 0)),
            pl.BlockSpec((tile, width), lambda i: (jnp.maximum(i - n_first, 0), 0))]


def _merge_kernel(x_ref, h_ref, oap_ref, oas_ref, obp_ref, obs_ref, oc_ref, od_ref, wg_ref, wb_ref, wo_ref, mod_ref,
                  o_ref, *, d, n_first):
    i = pl.program_id(0)
    h = h_ref[...]
    outs = (_two_path(i, n_first, oap_ref, oas_ref), _two_path(i, n_first, obp_ref, obs_ref), oc_ref[...], od_ref[...])
    merged = None
    off = 0
    for j in range(4):
        wdt = outs[j].shape[1]
        gate = jax.nn.sigmoid(_dot(h, wg_ref[:, j * d:(j + 1) * d]))
        br = gate * _dot(outs[j], wb_ref[off:off + wdt, :])
        merged = br if merged is None else merged + br
        off += wdt
    y = _dot(merged.astype(BF16), wo_ref[...])
    o_ref[...] = x_ref[...] + mod_ref[...][:, 2 * d:3 * d] * y


def _merge_call(x, h, oa_p, oa_s, ob_p, ob_s, oc, od, w_gate, w_branch, w_out, l, mod3, cond_of_tile, n_first):
    n, d = x.shape
    row = lambda i: (i, 0)
    const = lambda i: (l, 0, 0)
    once = pl.Buffered(1)
    return pl.pallas_call(
        functools.partial(_merge_kernel, d=d, n_first=n_first),
        grid=(n // TM,),
        in_specs=[pl.BlockSpec((TM, d), row), pl.BlockSpec((TM, d), row)]
        + _two_path_specs(oa_p.shape[1], n_first) + _two_path_specs(ob_p.shape[1], n_first)
        + [pl.BlockSpec((TM, oc.shape[1]), row), pl.BlockSpec((TM, od.shape[1]), row),
           pl.BlockSpec((None,) + w_gate.shape[1:], const, pipeline_mode=once),
           pl.BlockSpec((None,) + w_branch.shape[1:], const, pipeline_mode=once),
           pl.BlockSpec((None,) + w_out.shape[1:], const, pipeline_mode=once),
           pl.BlockSpec((None, 1, N_MOD * d), lambda i: (cond_of_tile(i), 0, 0))],
        out_specs=pl.BlockSpec((TM, d), row),
        out_shape=jax.ShapeDtypeStruct((n, d), F32),
        compiler_params=_cparams(("parallel",)),
        name="merge",
    )(x, h, oa_p, oa_s, ob_p, ob_s, oc, od, w_gate, w_branch, w_out, mod3)


def _router_kernel(x_ref, g_ref, mod_ref, wr_ref, br_ref, ltri_ref, h_ref, gate_ref, sel_ref, cnt_ref, run_s, *, d):
    i = pl.program_id(0)

    @pl.when(i == 0)
    def _():
        run_s[...] = jnp.zeros(run_s.shape, F32)

    h2 = _mod_norm(x_ref[...], g_ref[...], mod_ref[...], 3, 4, d)
    nsub = d // LANES
    for sub in range(nsub):
        h_ref[pl.ds(sub, x_ref.shape[0], stride=nsub), :] = h2[:, sub * LANES:(sub + 1) * LANES]
    logits = jnp.dot(h2, wr_ref[...], preferred_element_type=F32, precision=HIGHEST) + br_ref[...]
    lane = lax.broadcasted_iota(jnp.int32, logits.shape, 1)
    vals, idxs = [], []
    cur = logits
    for _ in range(TOP_K):
        m = jnp.max(cur, axis=-1, keepdims=True)
        ix = jnp.min(jnp.where(cur == m, lane, LANES), axis=-1, keepdims=True)
        vals.append(m)
        idxs.append(ix)
        cur = jnp.where(lane == ix, -jnp.inf, cur)
    es = [jnp.exp(v - vals[0]) for v in vals]
    tot = es[0] + es[1] + es[2] + es[3]
    gates = jnp.zeros(logits.shape, F32)
    sel = jnp.zeros(logits.shape, jnp.int32)
    base = run_s[...]
    ltri = ltri_ref[...]
    for k in range(TOP_K):
        onehot = jnp.where(lane == idxs[k], 1.0, 0.0)
        before = base + _dot(ltri, onehot.astype(BF16))
        rank = jnp.sum(onehot * before, axis=-1, keepdims=True).astype(jnp.int32)
        base = base + jnp.sum(onehot, axis=0, keepdims=True)
        gates = jnp.where(lane == k, es[k] / tot, gates)
        sel = jnp.where(lane == k, idxs[k], sel)
        sel = jnp.where(lane == TOP_K + k, rank, sel)
    run_s[...] = base
    gate_ref[...] = gates
    sel_ref[...] = sel
    cnt_ref[...] = base


def _router_call(x, g, mod3, cond_of_tile, w_router_p, b_router_p, l):
    n, d = x.shape
    row = lambda i: (i, 0)
    const = lambda i: (0, 0)
    layer = lambda i: (l, 0, 0)
    ltri = jnp.asarray(np.tril(np.ones((TM, TM), np.float32), -1)).astype(BF16)
    return pl.pallas_call(
        functools.partial(_router_kernel, d=d),
        grid=(n // TM,),
        in_specs=[pl.BlockSpec((TM, d), row),
                  pl.BlockSpec((1, d), const),
                  pl.BlockSpec((None, 1, N_MOD * d), lambda i: (cond_of_tile(i), 0, 0)),
                  pl.BlockSpec((None, d, LANES), layer),
                  pl.BlockSpec((None, 1, LANES), layer),
                  pl.BlockSpec((TM, TM), const)],
        out_specs=[pl.BlockSpec((TM * (d // LANES), LANES), row), pl.BlockSpec((TM, LANES), row),
                   pl.BlockSpec((TM, LANES), row), pl.BlockSpec((1, LANES), const)],
        out_shape=[jax.ShapeDtypeStruct((n * (d // LANES), LANES), F32), jax.ShapeDtypeStruct((n, LANES), F32),
                   jax.ShapeDtypeStruct((n, LANES), jnp.int32), jax.ShapeDtypeStruct((1, LANES), F32)],
        scratch_shapes=[pltpu.VMEM((1, LANES), F32)],
        compiler_params=_cparams(("arbitrary",)),
        name="router",
    )(x, g, mod3, w_router_p, b_router_p, ltri)


def _slot_plan(sel, counts, n_tok):
    cnt = counts[0, :N_EXPERTS].astype(jnp.int32)
    padded = (cnt + MOE_TM - 1) // MOE_TM * MOE_TM
    pad_end = jnp.cumsum(padded)
    pad_start = pad_end - padded
    dest = (pad_start[sel[:, :TOP_K]] + sel[:, TOP_K:2 * TOP_K]).reshape(n_tok * TOP_K)
    n_blocks = n_tok * TOP_K // MOE_TM + N_EXPERTS
    blk = jnp.arange(n_blocks, dtype=jnp.int32)
    block_e = jnp.minimum(jnp.sum((pad_end[None, :] <= (blk * MOE_TM)[:, None]).astype(jnp.int32), axis=1),
                          N_EXPERTS - 1)
    n_used = (pad_end[-1:] // MOE_TM).astype(jnp.int32)
    later = (blk[None, :] > blk[:, None]) & (block_e[None, :] != block_e[:, None]) & (blk[None, :] < n_used[0])
    first_later = jnp.argmax(later, axis=1)
    next_e = jnp.where(jnp.any(later, axis=1), block_e[first_later], -1).astype(jnp.int32)
    last_blk = jnp.where(padded > 0, pad_end // MOE_TM - 1, -1).astype(jnp.int32)
    return dest, block_e, n_used, next_e, last_blk, n_blocks


ROW_TILE = 8


def _row_slice(ref, r):
    return ref.at[pl.ds(pl.multiple_of(r * ROW_TILE, ROW_TILE), ROW_TILE)]


def _dispatch_kernel(dest_ref, last_ref, nu_ref, h_ref, o_ref, zero_s, sem, zsem):
    rows = h_ref.shape[0]
    tm = rows // ROW_TILE
    grp = zero_s.shape[0]
    n_groups = o_ref.shape[0] // grp
    base = pl.program_id(0) * (tm * TOP_K)

    @pl.when(pl.program_id(0) == 0)
    def _():
        zero_s[...] = jnp.zeros(zero_s.shape, zero_s.dtype)

        def fill(group):
            return pltpu.make_async_copy(zero_s, o_ref.at[pl.ds(pl.multiple_of(group * grp, grp), grp)], zsem)

        targets = [(last_ref[e] >= 0, last_ref[e]) for e in range(N_EXPERTS)]
        targets += [(nu_ref[0] + j < n_groups, nu_ref[0] + j) for j in range(N_EXPERTS)]
        for do, group in targets:
            @pl.when(do)
            def _():
                fill(group).start()
        for do, group in targets:
            @pl.when(do)
            def _():
                fill(group).wait()

    def body(t, carry):
        for k in range(TOP_K):
            pltpu.make_async_copy(_row_slice(h_ref, t), _row_slice(o_ref, dest_ref[base + t * TOP_K + k]),
                                  sem).start()
        return carry

    lax.fori_loop(0, tm, body, 0)
    for _ in range(TOP_K):
        pltpu.make_async_copy(h_ref, o_ref.at[pl.ds(0, rows)], sem).wait()


def _dispatch_call(dest, last_blk, n_used, h_rows, n_slots):
    n = h_rows.shape[0] // ROW_TILE
    grid_spec = pltpu.PrefetchScalarGridSpec(
        num_scalar_prefetch=3,
        grid=(n // TM,),
        in_specs=[pl.BlockSpec((TM * ROW_TILE, LANES), lambda i, dst, lst, nu: (i, 0))],
        out_specs=pl.BlockSpec(memory_space=pl.ANY),
        scratch_shapes=[pltpu.VMEM((MOE_TM * ROW_TILE, LANES), F32), pltpu.SemaphoreType.DMA(()),
                        pltpu.SemaphoreType.DMA(())],
    )
    return pl.pallas_call(
        _dispatch_kernel,
        grid_spec=grid_spec,
        out_shape=jax.ShapeDtypeStruct((n_slots * ROW_TILE, LANES), F32),
        compiler_params=_cparams(("arbitrary",)),
        name="dispatch",
    )(dest, last_blk, n_used, h_rows)


def _gmm_kernel(be_ref, nu_ref, nxt_ref, x_ref, wgu_hbm, bgu_ref, wd_hbm, bd_ref, y_ref, wgu_f, wd_f, wgu_s, wd_s,
                sems, *, f, layer):
    i = pl.program_id(0)
    used = i < nu_ref[0]
    changed = (i == 0) | (be_ref[i] != be_ref[jnp.maximum(i - 1, 0)])
    tm = x_ref.shape[0] // ROW_TILE

    def weight_copies(e):
        return (pltpu.make_async_copy(wgu_hbm.at[layer, e], wgu_f, sems.at[0]),
                pltpu.make_async_copy(wd_hbm.at[layer, e], wd_f, sems.at[1]))

    @pl.when(used & (i == 0))
    def _():
        for cp in weight_copies(be_ref[0]):
            cp.start()

    @pl.when(used & changed)
    def _():
        for cp in weight_copies(be_ref[i]):
            cp.wait()
        wgu_s[...] = wgu_f[...].astype(BF16)
        wd_s[...] = wd_f[...].astype(BF16)

        @pl.when(nxt_ref[i] >= 0)
        def _():
            for cp in weight_copies(nxt_ref[i]):
                cp.start()

    @pl.when(used)
    def _():
        x = jnp.concatenate([x_ref[pl.ds(sub, tm, stride=ROW_TILE), :] for sub in range(ROW_TILE)], axis=1)
        gu = _dot(x.astype(BF16), wgu_s[...]) + bgu_ref[...]
        gate = jnp.minimum(gu[:, :f], SWIGLU_LIMIT)
        up = jnp.clip(gu[:, f:], -SWIGLU_LIMIT, SWIGLU_LIMIT)
        act = gate * jax.nn.sigmoid(SWIGLU_ALPHA * gate) * (up + 1.0)
        y = _dot(act.astype(BF16), wd_s[...]) + bd_ref[...]
        for sub in range(ROW_TILE):
            y_ref[pl.ds(sub, tm, stride=ROW_TILE), :] = y[:, sub * LANES:(sub + 1) * LANES]

    @pl.when(jnp.logical_not(used))
    def _():
        y_ref[...] = jnp.zeros(y_ref.shape, y_ref.dtype)


def _gmm_call(block_e, n_used, next_e, x_slots, w_gu, b_gu, w_down, b_down, l):
    n_slots = x_slots.shape[0] // ROW_TILE
    depth, e, d, f2 = w_gu.shape
    f = f2 // 2
    n_blocks = n_slots // MOE_TM
    xrow = lambda i, be, nu, nx: (jnp.minimum(i, nu[0] - 1), 0)
    yrow = lambda i, be, nu, nx: (i, 0)
    bsel = lambda i, be, nu, nx: (l, be[i], 0, 0)
    grid_spec = pltpu.PrefetchScalarGridSpec(
        num_scalar_prefetch=3,
        grid=(n_blocks,),
        in_specs=[pl.BlockSpec((MOE_TM * ROW_TILE, LANES), xrow),
                  pl.BlockSpec(memory_space=pl.ANY),
                  pl.BlockSpec((None, None, 1, f2), bsel),
                  pl.BlockSpec(memory_space=pl.ANY),
                  pl.BlockSpec((None, None, 1, d), bsel)],
        out_specs=pl.BlockSpec((MOE_TM * ROW_TILE, LANES), yrow),
        scratch_shapes=[pltpu.VMEM((d, f2), F32), pltpu.VMEM((f, d), F32), pltpu.VMEM((d, f2), BF16),
                        pltpu.VMEM((f, d), BF16), pltpu.SemaphoreType.DMA((2,))],
    )
    return pl.pallas_call(
        functools.partial(_gmm_kernel, f=f, layer=l),
        grid_spec=grid_spec,
        out_shape=jax.ShapeDtypeStruct((n_slots * ROW_TILE, LANES), F32),
        compiler_params=_cparams(("arbitrary",)),
        name="experts",
    )(block_e, n_used, next_e, x_slots, w_gu, b_gu, w_down, b_down)


def _combine_kernel(dest_ref, x_ref, gate_ref, mod_ref, y_ref, o_ref, buf, sems, *, d):
    tm = x_ref.shape[0]
    rows = tm * ROW_TILE
    i = pl.program_id(0)
    n_steps = pl.num_programs(0)

    def issue(step, slot):
        base = step * (tm * TOP_K)

        def body(t, carry):
            for k in range(TOP_K):
                dst = buf.at[pl.ds(pl.multiple_of((slot * TOP_K + k) * rows + t * ROW_TILE, ROW_TILE), ROW_TILE)]
                pltpu.make_async_copy(_row_slice(y_ref, dest_ref[base + t * TOP_K + k]), dst, sems.at[slot]).start()
            return carry

        lax.fori_loop(0, tm, body, 0)

    @pl.when(i == 0)
    def _():
        issue(0, 0)

    @pl.when(i + 1 < n_steps)
    def _():
        issue(i + 1, (i + 1) % 2)

    slot = i % 2
    for k in range(TOP_K):
        off = pl.multiple_of((slot * TOP_K + k) * rows, ROW_TILE)
        pltpu.make_async_copy(y_ref.at[pl.ds(0, rows)], buf.at[pl.ds(off, rows)], sems.at[slot]).wait()
    g = gate_ref[...]
    gk = [jnp.broadcast_to(g[:, k:k + 1], (tm, LANES)) for k in range(TOP_K)]
    g2 = mod_ref[...][:, 5 * d:6 * d]
    for sub in range(ROW_TILE):
        sl = slice(sub * LANES, (sub + 1) * LANES)
        acc = None
        for k in range(TOP_K):
            off = pl.multiple_of((slot * TOP_K + k) * rows, ROW_TILE)
            term = gk[k] * buf[pl.ds(off + sub, tm, stride=ROW_TILE), :]
            acc = term if acc is None else acc + term
        o_ref[:, sl] = x_ref[:, sl] + g2[:, sl] * acc


def _combine_call(dest, x, gates, mod3, cond_of_tile, y_slots):
    n, d = x.shape
    row = lambda i, dst: (i, 0)
    grid_spec = pltpu.PrefetchScalarGridSpec(
        num_scalar_prefetch=1,
        grid=(n // TM,),
        in_specs=[pl.BlockSpec((TM, d), row),
                  pl.BlockSpec((TM, LANES), row),
                  pl.BlockSpec((None, 1, N_MOD * d), lambda i, dst: (cond_of_tile(i), 0, 0)),
                  pl.BlockSpec(memory_space=pl.ANY)],
        out_specs=pl.BlockSpec((TM, d), row),
        scratch_shapes=[pltpu.VMEM((2 * TOP_K * TM * ROW_TILE, LANES), F32), pltpu.SemaphoreType.DMA((2,))],
    )
    return pl.pallas_call(
        functools.partial(_combine_kernel, d=d),
        grid_spec=grid_spec,
        out_shape=jax.ShapeDtypeStruct((n, d), F32),
        compiler_params=_cparams(("arbitrary",)),
        name="combine",
    )(dest, x, gates, mod3, y_slots)


def _seg_matrix(w):
    i = np.arange(w) // HEAD_DIM
    return jnp.asarray((i[:, None] == i[None, :]).astype(np.float32)).astype(BF16)


def _seg_sum(y, p):
    hi = y.astype(BF16)
    lo = (y - hi.astype(F32)).astype(BF16)
    return _dot(hi, p) + _dot(lo, p)


def _head_mean_sq(x, p):
    return _seg_sum(x * x, p) * (1.0 / HEAD_DIM)


def _prep_kernel(z_ref, nw_ref, p_ref, *rest, scale, rope, want_norm, with_v):
    if rope:
        cos_ref, sin_ref = rest[0], rest[1]
        rest = rest[2:]
    x = z_ref[...]
    w = x.shape[1]
    if with_v:
        v = rest[0][...].astype(BF16)
        ones = jnp.ones((v.shape[0], HEAD_DIM), BF16)
        pieces = []
        for hd in range(w // HEAD_DIM):
            pieces += [v[:, hd * HEAD_DIM:(hd + 1) * HEAD_DIM], ones]
        rest[-1][...] = jnp.concatenate(pieces, axis=1)
        rest = rest[1:-1]
    y = x * lax.rsqrt(_head_mean_sq(x, p_ref[...]) + EPS) * nw_ref[...]
    if want_norm:
        rest[0][...] = y
    out_ref = rest[-1]
    if rope:
        reps = w // LANES
        cos = jnp.concatenate([cos_ref[...]] * reps, axis=1) if reps > 1 else cos_ref[...]
        sin = jnp.concatenate([sin_ref[...]] * reps, axis=1) if reps > 1 else sin_ref[...]
        lane = lax.broadcasted_iota(jnp.int32, y.shape, 1)
        first = (lane % 32) < 16
        rot = jnp.where(first, pltpu.roll(y, w - 16, axis=1), pltpu.roll(y, 16, axis=1))
        y = y * cos + rot * sin
    out_ref[...] = (y * scale).astype(BF16)


def _prep_call(z, col, w, nw, scale, rope_tabs, want_norm, v_col=None):
    n = z.shape[0]
    cb = col // w
    tm = next(c for c in (PREP_TM, TM) if n % c == 0)
    row = lambda i: (i, 0)
    in_specs = [pl.BlockSpec((tm, w), lambda i: (i, cb)),
                pl.BlockSpec((1, w), lambda i: (0, 0)),
                pl.BlockSpec((w, w), lambda i: (0, 0))]
    args = [z, jnp.tile(nw, w // HEAD_DIM).reshape(1, w), _seg_matrix(w)]
    if rope_tabs is not None:
        in_specs += [pl.BlockSpec((tm, LANES), row), pl.BlockSpec((tm, LANES), row)]
        args += list(rope_tabs)
    if v_col is not None:
        in_specs.append(pl.BlockSpec((tm, w), lambda i: (i, v_col // w)))
        args.append(z)
    out_specs, out_shape = [], []
    if want_norm:
        out_specs.append(pl.BlockSpec((tm, w), row))
        out_shape.append(jax.ShapeDtypeStruct((n, w), F32))
    out_specs.append(pl.BlockSpec((tm, w), row))
    out_shape.append(jax.ShapeDtypeStruct((n, w), BF16))
    if v_col is not None:
        out_specs.append(pl.BlockSpec((tm, 2 * w), row))
        out_shape.append(jax.ShapeDtypeStruct((n, 2 * w), BF16))
    return pl.pallas_call(
        functools.partial(_prep_kernel, scale=scale, rope=rope_tabs is not None, want_norm=want_norm,
                          with_v=v_col is not None),
        grid=(n // tm,),
        in_specs=in_specs, out_specs=out_specs, out_shape=out_shape,
        compiler_params=_cparams(("parallel",)),
        name="qk_prep",
    )(*args)


KEY_BLOCK = 512
LOG2E = 1.4426950408889634


def _ones_extended(v, heads):
    vb = v.astype(BF16).reshape(v.shape[:-1] + (heads, HEAD_DIM))
    return jnp.concatenate([vb, jnp.ones_like(vb)], axis=-1).reshape(v.shape[:-1] + (heads * 2 * HEAD_DIM,))


def _attn_kernel(q_ref, k_ref, v_ref, *rest, hq, hk, has_ctx, base2):
    if has_ctx:
        ck_ref, cv_ref, o_ref = rest
    else:
        (o_ref,) = rest
    g = hq // hk
    tq = q_ref.shape[0]
    t = k_ref.shape[0]
    dh = HEAD_DIM
    kb = min(KEY_BLOCK, t)
    ex = jnp.exp2 if base2 else jnp.exp
    blocks = [(ck_ref, cv_ref, 0, ck_ref.shape[0])] if has_ctx else []
    blocks += [(k_ref, v_ref, s0, kb) for s0 in range(0, t, kb)]
    outs = []
    for j in range(hk):
        q4 = jnp.concatenate([q_ref[:, (j * g + gi) * dh:(j * g + gi + 1) * dh] for gi in range(g)], axis=0)
        m = acc = None
        for kref, vref, s0, n in blocks:
            s = _dot_t(q4, kref[s0:s0 + n, j * dh:(j + 1) * dh])
            bm = jnp.max(s, axis=-1, keepdims=True)
            m_new = bm if m is None else jnp.maximum(m, bm)
            pv = _dot(ex(s - m_new).astype(BF16), vref[s0:s0 + n, 2 * j * dh:2 * (j + 1) * dh])
            acc = pv if m is None else ex(m - m_new) * acc + pv
            m = m_new
        o = acc[:, :dh] / acc[:, dh:]
        outs += [o[gi * tq:(gi + 1) * tq] for gi in range(g)]
    o_ref[...] = jnp.concatenate(outs, axis=1).astype(o_ref.dtype)


def _attn_call(q, k, v_ext, row0, nb, t, hq, hk, base2, ctx=None):
    tq = min(TQ, t)
    wq, wk = hq * HEAD_DIM, hk * HEAD_DIM
    qb0, kb0 = row0 // tq, row0 // t
    nq = t // tq
    in_specs = [pl.BlockSpec((tq, wq), lambda b, i: (qb0 + b * nq + i, 0)),
                pl.BlockSpec((t, wk), lambda b, i: (kb0 + b, 0)),
                pl.BlockSpec((t, 2 * wk), lambda b, i: (kb0 + b, 0))]
    args = [q, k, v_ext]
    if ctx is not None:
        ck, cv, layer = ctx
        p = ck.shape[2]
        in_specs += [pl.BlockSpec((None, None, p, wk), lambda b, i: (b, layer, 0, 0)),
                     pl.BlockSpec((None, None, p, 2 * wk), lambda b, i: (b, layer, 0, 0))]
        args += [ck, cv]
    return pl.pallas_call(
        functools.partial(_attn_kernel, hq=hq, hk=hk, has_ctx=ctx is not None, base2=base2),
        grid=(nb, nq),
        in_specs=in_specs,
        out_specs=pl.BlockSpec((tq, wq), lambda b, i: (b * nq + i, 0)),
        out_shape=jax.ShapeDtypeStruct((nb * t, wq), BF16),
        compiler_params=_cparams(("parallel", "parallel")),
        name="attn",
    )(*args)


def _nbr_bias_table(rpb):
    o = np.arange(NB_ROWS)[:, None, None, None]
    cc = np.arange(GRID_W)[None, :, None, None]
    i = np.arange(NB_ROWS)[None, None, :, None]
    j = np.arange(GRID_W)[None, None, None, :]
    col_start = np.clip(cc - NB_COLS // 2, 0, GRID_W - NB_COLS)
    valid = (j >= col_start) & (j < col_start + NB_COLS)
    sel_r = ((i - o + NB_ROWS - 1)[..., None] == np.arange(2 * NB_ROWS - 1)).astype(np.float32)[:, 0, :, 0]
    sel_c = (((j - cc + NB_COLS - 1)[..., None] == np.arange(2 * NB_COLS - 1)) & valid[..., None])
    sel_c = sel_c.astype(np.float32)[0, :, 0]
    tab = jnp.einsum('lhrd,oir,cjd->lhocij', rpb.astype(F32), jnp.asarray(sel_r), jnp.asarray(sel_c),
                     precision=HIGHEST)
    tab = tab * LOG2E + jnp.asarray(np.where(valid, 0.0, NEG).astype(np.float32))[None, None, :, :, :, :]
    return tab.reshape(rpb.shape[0], rpb.shape[1], NB_ROWS, GRID_W, NB_ROWS * GRID_W)


def _nbr_kernel(q_ref, k_ref, v_ref, ck_ref, cv_ref, bias_ref, o_ref, *, rows):
    r = pl.program_id(1)
    rs = jnp.clip(r - NB_ROWS // 2, 0, rows - NB_ROWS)
    start = pl.multiple_of(rs * GRID_W, GRID_W)
    kw = k_ref[pl.ds(start, NB_ROWS * GRID_W), :]
    vw = v_ref[pl.ds(start, NB_ROWS * GRID_W), :]
    dh = HEAD_DIM
    sls = [slice(h * dh, (h + 1) * dh) for h in range(B_HEADS)]
    vsl = [slice(2 * h * dh, 2 * (h + 1) * dh) for h in range(B_HEADS)]
    rowmax = lambda x: jnp.max(x, axis=-1, keepdims=True)
    qs = [q_ref[:, sl] for sl in sls]
    s_loc = [_dot_t(q, kw[:, sl]) + bias_ref[h] for h, (q, sl) in enumerate(zip(qs, sls))]
    s_ctx = [_dot_t(q, ck_ref[:, sl]) for q, sl in zip(qs, sls)]
    m = _each(lambda a, b: jnp.maximum(rowmax(a), rowmax(b)), s_loc, s_ctx)
    p_loc = _each(lambda a, mm: jnp.exp2(a - mm).astype(BF16), s_loc, m)
    p_ctx = _each(lambda a, mm: jnp.exp2(a - mm).astype(BF16), s_ctx, m)
    o = [_dot(a, vw[:, sl]) + _dot(b, cv_ref[:, sl]) for a, b, sl in zip(p_loc, p_ctx, vsl)]
    o_ref[...] = jnp.concatenate([x[:, :dh] / x[:, dh:] for x in o], axis=1).astype(o_ref.dtype)


def _nbr_call(q, k, v_ext, row0, nb, t, ck, cv, bias, layer):
    rows = t // GRID_W
    assert rows >= NB_ROWS
    w = B_HEADS * HEAD_DIM
    p = ck.shape[2]
    nwin = NB_ROWS * GRID_W

    def bias_map(b, r):
        return (layer, 0, r - jnp.clip(r - NB_ROWS // 2, 0, rows - NB_ROWS), 0, 0)

    return pl.pallas_call(
        functools.partial(_nbr_kernel, rows=rows),
        grid=(nb, rows),
        in_specs=[pl.BlockSpec((GRID_W, w), lambda b, r: (row0 // GRID_W + b * rows + r, 0)),
                  pl.BlockSpec((t, w), lambda b, r: (row0 // t + b, 0)),
                  pl.BlockSpec((t, 2 * w), lambda b, r: (row0 // t + b, 0)),
                  pl.BlockSpec((None, None, p, w), lambda b, r: (b, layer, 0, 0)),
                  pl.BlockSpec((None, None, p, 2 * w), lambda b, r: (b, layer, 0, 0)),
                  pl.BlockSpec((None, B_HEADS, None, GRID_W, nwin), bias_map)],
        out_specs=pl.BlockSpec((GRID_W, w), lambda b, r: (b * rows + r, 0)),
        out_shape=jax.ShapeDtypeStruct((nb * t, w), BF16),
        compiler_params=_cparams(("parallel", "arbitrary")),
        name="nbr_attn",
    )(q, k, v_ext, ck, cv, bias)


def _head_mask():
    i = np.arange(HW) // HEAD_DIM
    return jnp.asarray((i[:, None] == i[None, :]).astype(np.float32))


def _scan_rows(rev):
    t = lax.broadcasted_iota(jnp.int32, (CHUNK, HW), 0)
    return (CHUNK - 1 - t) if rev else t


def _scan_lanes(rev):
    s = lax.broadcasted_iota(jnp.int32, (CHUNK, HW), 1) % CHUNK
    return (CHUNK - 1 - s) if rev else s


def _sh(x, d, rev):
    s = (-d if rev else d) % CHUNK
    return x if s == 0 else pltpu.roll(x, s, axis=0)


def _cumsum_scan(x, pt, rev):
    d = 1
    while d < CHUNK:
        x = x + jnp.where(pt >= d, _sh(x, d, rev), 0.0)
        d *= 2
    return x


def _block_end(x, pt, n, rev):
    r = (n - 1) - pt % n
    bit = 1
    while bit < n:
        x = jnp.where((r & bit) != 0, _sh(x, -bit, rev), x)
        bit *= 2
    return x


def _pick16(x, j, rev):
    jj = (15 - j) if rev else j
    x3 = x.reshape(CHUNK // 16, 16, HW)
    return jnp.broadcast_to(x3[:, jj:jj + 1, :], x3.shape).reshape(CHUNK, HW)


def _row(x, p, rev):
    t = (CHUNK - 1 - p) if rev else p
    return x[t:t + 1, :]


def _bd(x, mask_bf):
    xb = x.astype(BF16)
    return jnp.concatenate([xb] * (HW // CHUNK), axis=0) * mask_bf


def _gla_chunks(ops_list, lbs, st_refs, mask, revs):
    mask_bf = mask.astype(BF16)
    bf = lambda x: x.astype(BF16)

    def prepare(ops, lb, rev):
        q_raw, v, f_raw = ops
        pt = _scan_rows(rev)
        q = _silu(q_raw)
        kk = jnp.minimum((1.0 - lb) * jax.nn.sigmoid(-f_raw), MAX_INPUT_GATE)
        b = _cumsum_scan(jnp.log1p(-kk), pt, rev)
        e4 = _block_end(b, pt, 4, rev)
        r4 = jnp.where(pt >= 4, _sh(e4, 4, rev), 0.0)
        r16 = jnp.where(pt >= 16, _sh(_pick16(b, 15, rev), 16, rev), 0.0)
        ks01 = [_bd(kk * jnp.exp(r4 - b), mask_bf)]
        ks2 = []
        for i in range(1, 4):
            ks01.append(_bd(kk * jnp.exp(jnp.minimum(_pick16(r4, 4 * i, rev) - b, 0.0)), mask_bf))
            ks2.append(_bd(kk * jnp.exp(jnp.minimum(_row(b, 16 * i - 1, rev) - b, 0.0)), mask_bf))
        return dict(pt=pt, ps=_scan_lanes(rev), v=v, kk=kk, b=b, b_last=_row(b, CHUNK - 1, rev),
                    q0=bf(q * jnp.exp(b - r4)), q2=bf(q * jnp.exp(b - r16)), qe=bf(q * jnp.exp(b)),
                    k01=jnp.concatenate(ks01, axis=0), k2=jnp.concatenate(ks2, axis=0))

    def select(c, r01, r2):
        pt, ps = c['pt'], c['ps']
        pt4, ps4, pt16, ps16 = pt // 4, ps // 4, pt // 16, ps // 16
        attn = jnp.where((pt4 == ps4) & (ps <= pt), r01[:, :HW], 0.0)
        for i in range(1, 4):
            attn = jnp.where((pt16 == ps16) & (ps4 < pt4) & (pt4 % 4 == i), r01[:, i * HW:(i + 1) * HW], attn)
            attn = jnp.where((ps16 < pt16) & (pt16 == i), r2[:, (i - 1) * HW:i * HW], attn)
        return bf(attn)

    cs = _each(prepare, ops_list, lbs, revs)
    r01 = [_dot_t(c['q0'], c['k01']) for c in cs]
    r2 = [_dot_t(c['q2'], c['k2']) for c in cs]
    attn = _each(select, cs, r01, r2)
    st = [s[...] for s in st_refs]
    o = _each(lambda c, s, aa: _dot_t(c['qe'], bf(s)) + _dot(aa, _bd(c['v'], mask_bf)), cs, st, attn)
    upd = [lax.dot_general(bf(c['v']), bf(c['kk'] * jnp.exp(c['b_last'] - c['b'])), (((0,), (0,)), ((), ())),
                           preferred_element_type=F32) for c in cs]
    for s_ref, c, s, u in zip(st_refs, cs, st, upd):
        s_ref[...] = jnp.exp(c['b_last']) * s + mask * u
    return o


def _scan_kernel(*refs, chunk_fn, n_in, n_const, group, has_state):
    n_op = group * 2 * n_in
    ins, consts, rest = refs[:n_op], refs[n_op:n_op + n_const], refs[n_op + n_const:]
    if has_state:
        s0_ref, of_ref, ob_ref, sout_ref, st_s = rest
    else:
        of_ref, ob_ref, sout_ref, st_s = rest
    c = pl.program_id(1)

    @pl.when(c == 0)
    def _():
        st_s[...] = s0_ref[...] if has_state else jnp.zeros(st_s.shape, F32)

    chains = [(g, direction) for g in range(group) for direction in (0, 1)]
    ops_list = [[r[...] for r in ins[(g * 2 + dr) * n_in:(g * 2 + dr + 1) * n_in]] for g, dr in chains]
    outs = chunk_fn(ops_list, consts, [st_s.at[g, dr] for g, dr in chains], [dr for _, dr in chains])
    for (g, dr), o in zip(chains, outs):
        (ob_ref if dr else of_ref)[g] = o

    @pl.when(c == pl.num_programs(1) - 1)
    def _():
        sout_ref[...] = st_s[...]


def _scan_call(name, chunk_fn, ops_f, ops_b, consts, row0, nb, t, s0_bd, group):
    assert nb % group == 0 and len(ops_f) == len(ops_b)
    nc = t // CHUNK
    r0 = row0 // CHUNK
    has_state = s0_bd is not None
    in_specs, args = [], []
    for g in range(group):
        for direction, ops in ((0, ops_f), (1, ops_b)):
            for arr, w, cb in ops:
                if direction == 0:
                    imap = functools.partial(lambda b, c, g, cb: (r0 + (b * group + g) * nc + c, cb), g=g, cb=cb)
                else:
                    imap = functools.partial(lambda b, c, g, cb: (r0 + (b * group + g) * nc + nc - 1 - c, cb),
                                             g=g, cb=cb)
                in_specs.append(pl.BlockSpec((CHUNK, w), imap))
                args.append(arr)
    for cst in consts:
        in_specs.append(pl.BlockSpec(cst.shape, functools.partial(lambda b, c, nd: (0,) * nd, nd=cst.ndim)))
        args.append(cst)
    state_spec = pl.BlockSpec((group, 2, HW, HW), lambda b, c: (b, 0, 0, 0))
    if has_state:
        s0_all, layer = s0_bd
        in_specs.append(pl.BlockSpec((group, None, 2, HW, HW), lambda b, c: (b, layer, 0, 0, 0)))
        args.append(s0_all)
    o_f, o_b, s_out = pl.pallas_call(
        functools.partial(_scan_kernel, chunk_fn=chunk_fn, n_in=len(ops_f), n_const=len(consts), group=group,
                          has_state=has_state),
        grid=(nb // group, nc),
        in_specs=in_specs,
        out_specs=[pl.BlockSpec((group, CHUNK, HW), lambda b, c: (b, c, 0)),
                   pl.BlockSpec((group, CHUNK, HW), lambda b, c: (b, nc - 1 - c, 0)),
                   state_spec],
        out_shape=[jax.ShapeDtypeStruct((nb, t, HW), F32), jax.ShapeDtypeStruct((nb, t, HW), F32),
                   jax.ShapeDtypeStruct((nb, 2, HW, HW), F32)],
        scratch_shapes=[pltpu.VMEM((group, 2, HW, HW), F32)],
        compiler_params=_cparams(("parallel", "arbitrary")),
        name=name,
    )(*args)
    return o_f.reshape(nb * t, HW), o_b.reshape(nb * t, HW), s_out


def _hgrn_chunks_fn(ops_list, consts, st_refs, directions):
    lb_ref, mask_ref = consts
    return _gla_chunks(ops_list, [lb_ref[dr:dr + 1, :] for dr in directions], st_refs, mask_ref[...],
                       [dr == 1 for dr in directions])


def _state_to_bd(s):
    eye = jnp.eye(C_HEADS, dtype=F32)
    return jnp.einsum('...hkv,hg->...hvgk', s.astype(F32), eye).reshape(s.shape[:-3] + (HW, HW))


def _state_from_bd(sbd):
    s6 = sbd.reshape(sbd.shape[:-2] + (C_HEADS, HEAD_DIM, C_HEADS, HEAD_DIM))
    diag = jnp.stack([s6[..., h, :, h, :] for h in range(C_HEADS)], axis=-3)
    return jnp.swapaxes(diag, -1, -2)


def _hgrn_call(z, row0, nb, t, lb, s0_bd, group):
    ops = lambda fcol: [(z, HW, COL['cq'] // HW), (z, HW, COL['ci'] // HW), (z, HW, fcol // HW)]
    return _scan_call("hgrn_scan", _hgrn_chunks_fn, ops(COL['cff']), ops(COL['cfb']), [lb, _head_mask()],
                      row0, nb, t, s0_bd, group)


def _gate_norm_kernel(ofp_ref, ofs_ref, obp_ref, obs_ref, g_ref, nw_ref, p_ref, o_ref, *, n_first):
    i = pl.program_id(0)
    o = _two_path(i, n_first, ofp_ref, ofs_ref) + _two_path(i, n_first, obp_ref, obs_ref)
    y = o * lax.rsqrt(_head_mean_sq(o, p_ref[...]) + EPS) * nw_ref[...]
    o_ref[...] = (y * _silu(g_ref[...])).astype(o_ref.dtype)


def _gate_norm_call(of_p, of_s, ob_p, ob_s, z, gcol, nw, tile):
    n = of_p.shape[0] + of_s.shape[0]
    n_first = of_p.shape[0] // tile
    row = lambda i: (i, 0)
    return pl.pallas_call(
        functools.partial(_gate_norm_kernel, n_first=n_first),
        grid=(n // tile,),
        in_specs=_two_path_specs(HW, n_first, tile) + _two_path_specs(HW, n_first, tile)
        + [pl.BlockSpec((tile, HW), lambda i: (i, gcol // HW)),
           pl.BlockSpec((1, HW), lambda i: (0, 0)),
           pl.BlockSpec((HW, HW), lambda i: (0, 0))],
        out_specs=pl.BlockSpec((tile, HW), row),
        out_shape=jax.ShapeDtypeStruct((n, HW), BF16),
        compiler_params=_cparams(("parallel",)),
        name="gate_norm",
    )(of_p, of_s, ob_p, ob_s, z, jnp.tile(nw, HW // HEAD_DIM).reshape(1, HW), _seg_matrix(HW))


DQKV_W = 3 * HW
HALO = 8


def _softplus(x):
    return jnp.maximum(x, 0.0) + jnp.log1p(jnp.exp(-jnp.abs(x)))


def _delta_prep_kernel(x_ref, xp_ref, xn_ref, cw_ref, dab_ref, ex_ref, al_ref, dtb_ref, p_ref, qkv_ref, gb_ref,
                       *, n_single, tiles_per_seq):
    i = pl.program_id(0)
    j = jnp.maximum(i - n_single, 0) % tiles_per_seq
    first = (i < n_single) | (j == 0)
    last = (i < n_single) | (j == tiles_per_seq - 1)
    x = x_ref[...]
    tm = x.shape[0]
    prev = jnp.where(first, 0.0, xp_ref[...])
    nxt = jnp.where(last, 0.0, xn_ref[...])
    xe = jnp.concatenate([prev, x, nxt], axis=0)
    y = None
    for tap in range(CONV_K):
        lo = HALO + tap - CONV_K // 2
        term = cw_ref[tap:tap + 1, :] * xe[lo:lo + tm]
        y = term if y is None else y + term
    y = _silu(y)
    qk = y[:, :2 * HW]
    ssq = _seg_sum(qk * qk, p_ref[...])
    qkn = qk * lax.rsqrt(ssq + EPS)
    qkv_ref[...] = jnp.concatenate([qkn[:, :HW] * (HEAD_DIM ** -0.5), qkn[:, HW:], y[:, 2 * HW:]], axis=1)
    e = jnp.dot(dab_ref[...], ex_ref[...], preferred_element_type=F32, precision=HIGHEST)
    g = -jnp.exp(al_ref[...]) * _softplus(e[:, :2 * HW] + dtb_ref[...])
    gb_ref[...] = jnp.concatenate([g, jax.nn.sigmoid(e[:, 2 * HW:])], axis=1)


def _delta_prep_call(z, conv_w, a_log, dt_bias, n_single, tiles_per_seq):
    n = z.shape[0]
    cb = COL['dq'] // DQKV_W
    hb = TM // HALO
    nhb = n // HALO
    ex = np.zeros((LANES, 4 * HW), np.float32)
    for r in range(4 * D_HEADS):
        ex[r, r * HEAD_DIM:(r + 1) * HEAD_DIM] = 1.0
    cw = jnp.zeros((8, DQKV_W), F32).at[:CONV_K].set(conv_w.astype(F32))
    const = lambda i: (0, 0)
    return pl.pallas_call(
        functools.partial(_delta_prep_kernel, n_single=n_single, tiles_per_seq=tiles_per_seq),
        grid=(n // TM,),
        in_specs=[pl.BlockSpec((TM, DQKV_W), lambda i: (i, cb)),
                  pl.BlockSpec((HALO, DQKV_W), lambda i: (jnp.maximum(i * hb - 1, 0), cb)),
                  pl.BlockSpec((HALO, DQKV_W), lambda i: (jnp.minimum((i + 1) * hb, nhb - 1), cb)),
                  pl.BlockSpec((8, DQKV_W), const),
                  pl.BlockSpec((TM, LANES), lambda i: (i, COL['dab'] // LANES)),
                  pl.BlockSpec((LANES, 4 * HW), const),
                  pl.BlockSpec((1, 2 * HW), const), pl.BlockSpec((1, 2 * HW), const),
                  pl.BlockSpec((2 * HW, 2 * HW), const)],
        out_specs=[pl.BlockSpec((TM, DQKV_W), lambda i: (i, 0)), pl.BlockSpec((TM, 4 * HW), lambda i: (i, 0))],
        out_shape=[jax.ShapeDtypeStruct((n, DQKV_W), F32), jax.ShapeDtypeStruct((n, 4 * HW), F32)],
        compiler_params=_cparams(("parallel",)),
        name="delta_prep",
    )(z, z, z, cw, z, jnp.asarray(ex),
      jnp.repeat(a_log.astype(F32).reshape(-1), HEAD_DIM).reshape(1, 2 * HW),
      jnp.repeat(dt_bias.astype(F32).reshape(-1), HEAD_DIM).reshape(1, 2 * HW), _seg_matrix(2 * HW))


def _delta_consts():
    t = np.arange(CHUNK)[:, None]
    s = (np.arange(HW) % CHUNK)[None, :]
    s16 = t // 16 == s // 16
    s32 = t // 32 == s // 32
    mats = [t == s, s16, s32 & ~s16, ~s32]
    return jnp.asarray(np.stack(mats).astype(np.float32))


def _tri_inv(acs, mask_bf, eye, m16, m32, m64):
    bf = lambda x: x.astype(BF16)
    mm = lambda x, y: _dot(bf(x), _bd(y, mask_bf))
    a16 = [a * m16 for a in acs]
    tin = [eye - a for a in a16]
    p = _each(mm, a16, a16)
    for step in range(3):
        tin = _each(lambda t, pp: t + mm(t, pp), tin, p)
        if step < 2:
            p = _each(lambda pp: mm(pp, pp), p)
    for m in (m32, m64):
        inner = _each(lambda a, t: mm(a * m, t), acs, tin)
        tin = _each(lambda t, i_n: t - mm(t, i_n), tin, inner)
    return tin


def _delta_chunks(ops_list, st_refs, mask_ref, cmask_ref, revs):
    mask = mask_ref[...]
    mask_bf = mask.astype(BF16)
    bf = lambda x: x.astype(BF16)
    pts = [_scan_rows(r) for r in revs]
    pss = [_scan_lanes(r) for r in revs]
    qs = [o[0][:, :HW] for o in ops_list]
    ks = [o[0][:, HW:2 * HW] for o in ops_list]
    vs = [o[0][:, 2 * HW:] for o in ops_list]
    betas = [o[2] for o in ops_list]
    gcs = _each(_cumsum_scan, [o[1] for o in ops_list], pts, revs)
    t_idx = lax.broadcasted_iota(jnp.int32, (CHUNK, HW), 0)
    s_idx = lax.broadcasted_iota(jnp.int32, (CHUNK, HW), 1) % CHUNK
    diag = t_idx == s_idx
    gcl = [jnp.sum(jnp.where(diag, gc, 0.0), axis=0, keepdims=True) for gc in gcs]
    gam = _each(lambda gc, gl: jnp.exp(jnp.minimum(gc - gl, 0.0)), gcs, gcl)
    g_last = _each(lambda gc, r: _row(gc, CHUNK - 1, r), gcs, revs)
    egc = [jnp.exp(gc) for gc in gcs]
    kb = _each(lambda k, b: k * b, ks, betas)
    r = _each(lambda k_b, q, k: _dot_t(bf(jnp.concatenate([k_b, q], axis=0)), _bd(k, mask_bf)), kb, qs, ks)
    a = _each(lambda rr, gm, ps, pt: jnp.where(ps < pt, rr[:CHUNK] * gm, 0.0), r, gam, pss, pts)
    aq = _each(lambda rr, gm, ps, pt: jnp.where(ps <= pt, rr[CHUNK:] * gm, 0.0), r, gam, pss, pts)
    tin = _tri_inv(a, mask_bf, cmask_ref[0], cmask_ref[1], cmask_ref[2], cmask_ref[3])
    rhs = _each(lambda v, b, k_b, e: jnp.concatenate([_bd(v * b, mask_bf), _bd(k_b * e, mask_bf)], axis=1),
                vs, betas, kb, egc)
    uw = _each(lambda t, rh: _dot(bf(t), rh), tin, rhs)
    st = [s[...] for s in st_refs]
    ws_qs = _each(lambda x, q, e, s: _dot_t(bf(jnp.concatenate([x[:, HW:], q * e], axis=0)), bf(s)),
                  uw, qs, egc, st)
    v_new = _each(lambda x, y: x[:, :HW] - y[:CHUNK], uw, ws_qs)
    o = _each(lambda y, aa, vn: y[CHUNK:] + _dot(bf(aa), _bd(vn, mask_bf)), ws_qs, aq, v_new)
    ke = _each(lambda k, gl, gc: bf(k * jnp.exp(gl - gc)), ks, g_last, gcs)
    upd = _each(lambda vn, kk: lax.dot_general(bf(vn), kk, (((0,), (0,)), ((), ())), preferred_element_type=F32),
                v_new, ke)
    for s_ref, gl, s, u in zip(st_refs, g_last, st, upd):
        s_ref[...] = jnp.exp(gl) * s + mask * u
    return o


def _delta_chunks_fn(ops_list, consts, st_refs, directions):
    return _delta_chunks(ops_list, st_refs, consts[0], consts[1], [d == 1 for d in directions])


def _delta_call(qkv, gb, row0, nb, t, s0_bd, group):
    ops_f = [(qkv, DQKV_W, 0), (gb, HW, 0), (gb, HW, 2)]
    ops_b = [(qkv, DQKV_W, 0), (gb, HW, 1), (gb, HW, 3)]
    return _scan_call("delta_scan", _delta_chunks_fn, ops_f, ops_b, [_head_mask(), _delta_consts()], row0, nb, t, s0_bd, group)


def kernel(x_prompt, x_sample, cache_a_k, cache_a_v, cache_b_k, cache_b_v, state_hgrn, state_delta, c, c_ctx, w_mod, b_mod, norm1, norm2, w_in, w_gate, a_qn, a_kn, b_qn, b_kn, b_rpb, c_lb, c_norm, d_conv, d_alog, d_dtbias, d_norm, w_branch, w_out, w_router, b_router, w_gu, b_gu, w_down, b_down):
    nbp, tp, d = x_prompt.shape
    nbs, ts, _ = x_sample.shape
    depth = w_mod.shape[0]
    past = cache_a_k.shape[2]
    n_p = nbp * tp
    n = n_p + nbs * ts
    assert tp == TM and ts % TM == 0 and n_p % ts == 0 and 1 + nbs <= 8 and n % TM_MM == 0
    assert d == ROW_TILE * LANES
    npt, tps = n_p // TM, ts // TM
    grp_p = next(g for g in (4, 2, 1) if nbp % g == 0)
    grp_s = next(g for g in (2, 1) if nbs % g == 0)

    def cond_index(i):
        return jnp.where(i < npt, 0, 1 + (i - npt) // tps)

    big = PREP_TM if (n_p % PREP_TM == 0 and ts % PREP_TM == 0) else TM

    def cond_index_big(i):
        return jnp.where(i < n_p // big, 0, 1 + (i - n_p // big) // (ts // big))

    x = jnp.concatenate([x_prompt.reshape(n_p, d), x_sample.reshape(nbs * ts, d)], axis=0)
    conds = jnp.zeros((8, d), F32).at[0].set(c_ctx).at[1:1 + nbs].set(c)
    mod3 = _mod_call(conds, w_mod, b_mod).reshape(depth * 8, 1, N_MOD * d)

    pl_ = jax.nn.softmax(c_lb.astype(F32), axis=0)
    lbs = jnp.cumsum(pl_, axis=0) - pl_[0:1]
    dqkv0 = COL['cg'] + HW
    da0 = dqkv0 + DQKV_W
    dg0 = da0 + 4 * D_HEADS
    w_in_p = jnp.concatenate([w_in[:, :, :dqkv0], w_in[:, :, dg0:dg0 + HW], w_in[:, :, dqkv0:da0],
                              w_in[:, :, da0:dg0],
                              jnp.zeros((depth, d, IN_W_PAD - dg0 - HW), w_in.dtype)], axis=2).astype(BF16)
    w_gate_b = w_gate.astype(BF16)
    w_branch_b = w_branch.astype(BF16)
    w_out_b = w_out.astype(BF16)
    w_router_p = jnp.zeros((depth, d, LANES), F32).at[:, :, :N_EXPERTS].set(w_router)
    b_router_p = jnp.full((depth, 1, LANES), NEG, F32).at[:, 0, :N_EXPERTS].set(b_router)

    tt = jnp.arange(ts, dtype=jnp.int32)
    half = HEAD_DIM // 2
    inv_freq = 1.0 / (ROPE_THETA ** (jnp.arange(0, half, 2, dtype=F32) / half))
    ang_r = (tt // GRID_W).astype(F32)[:, None] * inv_freq[None, :]
    ang_c = (tt % GRID_W).astype(F32)[:, None] * inv_freq[None, :]
    ang = jnp.concatenate([ang_r, ang_r, ang_c, ang_c], axis=-1)
    sign = jnp.asarray(np.where((np.arange(HEAD_DIM) % 32) < 16, -1.0, 1.0).astype(np.float32))
    cos_all = jnp.concatenate([jnp.ones((n_p, HEAD_DIM), F32), jnp.tile(jnp.cos(ang), (nbs, 1))], axis=0)
    sin_all = jnp.concatenate([jnp.zeros((n_p, HEAD_DIM), F32), jnp.tile(jnp.sin(ang) * sign, (nbs, 1))], axis=0)
    rope_tabs = (jnp.tile(cos_all, (1, 2)), jnp.tile(sin_all, (1, 2)))
    scale = HEAD_DIM ** -0.5
    wa, wb = A_KV_HEADS * HEAD_DIM, B_HEADS * HEAD_DIM

    cak = cache_a_k.reshape(nbs, depth, past, wa).astype(BF16)
    cav = _ones_extended(cache_a_v.reshape(nbs, depth, past, wa), A_KV_HEADS)
    cbk = cache_b_k.reshape(nbs, depth, past, wb).astype(BF16)
    cbv = _ones_extended(cache_b_v.reshape(nbs, depth, past, wb), B_HEADS)
    nbr_bias = _nbr_bias_table(b_rpb)
    sc0 = _state_to_bd(state_hgrn)
    sd0 = _state_to_bd(state_delta)
    b_gu4 = b_gu.reshape(depth, N_EXPERTS, 1, b_gu.shape[-1])
    b_down4 = b_down.reshape(depth, N_EXPERTS, 1, d)

    ak_l, av_l, bk_l, bv_l, sc_l, sd_l = [], [], [], [], [], []
    for l in range(depth):
        cond_of_tile = functools.partial(lambda i, base: base + cond_index(i), base=8 * l)
        cond_big = functools.partial(lambda i, base: base + cond_index_big(i), base=8 * l)
        h = _norm_call(x, norm1[l].reshape(1, d), mod3, cond_big, big)
        z = _mm_call(h, w_in_p, l)

        (qa,) = _prep_call(z, COL['aq'], A_HEADS * HEAD_DIM, a_qn[l], scale * LOG2E, rope_tabs, False)
        ka_n, ka, va_ext = _prep_call(z, COL['ak'], wa, a_kn[l], 1.0, rope_tabs, True, v_col=COL['av'])
        oa_p = _attn_call(qa, ka, va_ext, 0, nbp, tp, A_HEADS, A_KV_HEADS, True)
        oa_s = _attn_call(qa, ka, va_ext, n_p, nbs, ts, A_HEADS, A_KV_HEADS, True, ctx=(cak, cav, l))

        (qb,) = _prep_call(z, COL['bq'], wb, b_qn[l], scale * LOG2E, None, False)
        kb_n, kb, vb_ext = _prep_call(z, COL['bk'], wb, b_kn[l], 1.0, None, True, v_col=COL['bv'])
        ob_p = _attn_call(qb, kb, vb_ext, 0, nbp, tp, B_HEADS, B_HEADS, True)
        ob_s = _nbr_call(qb, kb, vb_ext, n_p, nbs, ts, cbk, cbv, nbr_bias, l)

        cf_p, cb_p, sc_p = _hgrn_call(z, 0, nbp, tp, lbs[l], None, grp_p)
        cf_s, cb_s, _ = _hgrn_call(z, n_p, nbs, ts, lbs[l], (sc0, l), grp_s)
        oc = _gate_norm_call(cf_p, cf_s, cb_p, cb_s, z, COL['cg'], c_norm[l], big)

        qkv, gb = _delta_prep_call(z, d_conv[l], d_alog[l], d_dtbias[l], npt, tps)
        df_p, db_p, sd_p = _delta_call(qkv, gb, 0, nbp, tp, None, grp_p)
        df_s, db_s, _ = _delta_call(qkv, gb, n_p, nbs, ts, (sd0, l), grp_s)
        od = _gate_norm_call(df_p, df_s, db_p, db_s, z, COL['dg'], d_norm[l], big)

        x = _merge_call(x, h, oa_p, oa_s, ob_p, ob_s, oc, od, w_gate_b, w_branch_b, w_out_b, l, mod3, cond_of_tile,
                        npt)

        h_rows, gates, sel, counts = _router_call(x, norm2[l].reshape(1, d), mod3, cond_of_tile,
                                                  w_router_p, b_router_p, l)
        dest, block_e, n_used, next_e, last_blk, n_blocks = _slot_plan(sel, counts, n)
        x_slots = _dispatch_call(dest, last_blk, n_used, h_rows, n_blocks * MOE_TM)
        y_slots = _gmm_call(block_e, n_used, next_e, x_slots, w_gu, b_gu4, w_down, b_down4, l)
        x = _combine_call(dest, x, gates, mod3, cond_of_tile, y_slots)

        ak_l.append(ka_n[:n_p].reshape(nbp, tp, A_KV_HEADS, HEAD_DIM))
        av_l.append(z[:n_p, COL['av']:COL['av'] + wa].reshape(nbp, tp, A_KV_HEADS, HEAD_DIM))
        bk_l.append(kb_n[:n_p].reshape(nbp, tp, B_HEADS, HEAD_DIM))
        bv_l.append(z[:n_p, COL['bv']:COL['bv'] + wb].reshape(nbp, tp, B_HEADS, HEAD_DIM))
        sc_l.append(sc_p)
        sd_l.append(sd_p)

    return (x[:n_p].reshape(nbp, tp, d), x[n_p:].reshape(nbs, ts, d),
            jnp.stack(ak_l, axis=1), jnp.stack(av_l, axis=1), jnp.stack(bk_l, axis=1), jnp.stack(bv_l, axis=1),
            _state_from_bd(jnp.stack(sc_l, axis=1)), _state_from_bd(jnp.stack(sd_l, axis=1)))
```
